```python
import math
import jax, jax.numpy as jnp
from jax import lax
import numpy as np

D_MODEL = 2048
BATCH = 4
SEQ = 2048
DEPTH = 4

N_A_LAYERS = DEPTH // 2
N_B_LAYERS = DEPTH - N_A_LAYERS
HG_EXPAND = 128
HG_HEADS = D_MODEL // HG_EXPAND
HG_DK = HG_EXPAND
HG_DV = D_MODEL // HG_HEADS
HG_CHUNK = 64
ATT_HEAD_DIM = 64
ATT_Q_HEADS = D_MODEL // ATT_HEAD_DIM
ATT_KV_HEADS = ATT_Q_HEADS // 8
ATT_GROUP = ATT_Q_HEADS // ATT_KV_HEADS
WINDOW = 128
ATT_BLOCK = WINDOW
N_BUCKETS = 32
REL_MAX_DISTANCE = 128
N_GROUPS = 4
EXPERTS_PER_GROUP = 8
N_EXPERTS = N_GROUPS * EXPERTS_PER_GROUP
TOP_K = 2
D_EXPERT = D_MODEL // 4
MOE_BLOCK = 128
DEEPNORM_ALPHA = (2 * DEPTH) ** 0.25
DEEPNORM_BETA = (8 * DEPTH) ** -0.25
LN_EPS = 1e-5
RMS_EPS = 1e-6
NEG_BIG = -1e30
MIN_FORGET = 1e-30

kernel_name = "yoco_hgrn2_swa_sink_hmoe_deepnorm"

F32 = jnp.float32


def _layernorm(x, g, b):
    xf = x.astype(F32)
    mu = jnp.mean(xf, -1, keepdims=True)
    var = jnp.mean(jnp.square(xf - mu), -1, keepdims=True)
    y = (xf - mu) * lax.rsqrt(var + LN_EPS) * g.astype(F32) + b.astype(F32)
    return y.astype(x.dtype)


def _hgrn2(x, w_in, lower_bound, norm_g, w_out):
    bsz, seq, _ = x.shape
    n_chunks = seq // HG_CHUNK
    q, f, i, g = jnp.split(x @ w_in, 4, axis=-1)
    f = f.astype(F32)
    lb = lower_bound
    sig = jax.nn.sigmoid(f)
    forget = lb + (1.0 - lb) * sig
    log_forget = jnp.log(jnp.maximum(forget, MIN_FORGET))
    k = (1.0 - lb) * (1.0 - sig)

    def to_chunks(t):
        return t.astype(F32).reshape(bsz, n_chunks, HG_CHUNK, HG_HEADS, -1).transpose(1, 0, 3, 2, 4)

    qc, kc, vc, gc = to_chunks(q), to_chunks(k), to_chunks(i), to_chunks(log_forget)
    causal = jnp.tril(jnp.ones((HG_CHUNK, HG_CHUNK), bool))

    def chunk_step(state, inp):
        q_c, k_c, v_c, g_c = inp
        b = jnp.cumsum(g_c, axis=2)
        b_last = b[:, :, -1:, :]
        o_inter = jnp.einsum('bhtk,bhkv->bhtv', q_c * jnp.exp(b), state)
        rel = jnp.where(causal[None, None, :, :, None],
                        b[:, :, :, None, :] - b[:, :, None, :, :], NEG_BIG)
        scores = jnp.einsum('bhtk,bhsk,bhtsk->bhts', q_c, k_c, jnp.exp(rel))
        o_intra = jnp.einsum('bhts,bhsv->bhtv', scores, v_c)
        new_state = (jnp.exp(b_last[:, :, 0, :, None]) * state
                     + jnp.einsum('bhsk,bhsv->bhkv', k_c * jnp.exp(b_last - b), v_c))
        return new_state, o_inter + o_intra

    s0 = jnp.zeros((bsz, HG_HEADS, HG_DK, HG_DV), F32)
    _, o = lax.scan(chunk_step, s0, (qc, kc, vc, gc))
    o = o.transpose(1, 0, 3, 2, 4).reshape(bsz, seq, HG_HEADS, HG_DV)
    o = o * lax.rsqrt(jnp.mean(jnp.square(o), -1, keepdims=True) + RMS_EPS)
    o = o * norm_g.astype(F32).reshape(HG_HEADS, HG_DV)
    o = o.reshape(bsz, seq, D_MODEL) * jax.nn.silu(g.astype(F32))
    return o.astype(x.dtype) @ w_out


def _t5_bucket(dist):
    n = jnp.clip(dist, 0, REL_MAX_DISTANCE - 1)
    max_exact = N_BUCKETS // 2
    large = max_exact + (jnp.log(jnp.maximum(n, max_exact).astype(F32) / max_exact)
                         / math.log(REL_MAX_DISTANCE / max_exact)
                         * (N_BUCKETS - max_exact)).astype(jnp.int32)
    large = jnp.minimum(large, N_BUCKETS - 1)
    return jnp.where(n < max_exact, n, large)


def _band_bias_and_mask(rel_bias, seq):
    n_blocks = seq // ATT_BLOCK
    qi = jnp.arange(ATT_BLOCK)[:, None]
    kj = jnp.arange(2 * ATT_BLOCK)[None, :]
    dist = qi + ATT_BLOCK - kj
    in_window = (dist >= 0) & (dist < WINDOW)
    bias = rel_bias.astype(F32)[_t5_bucket(dist)]
    bias = bias.transpose(2, 0, 1).reshape(1, ATT_KV_HEADS, ATT_GROUP, 1, ATT_BLOCK, 2 * ATT_BLOCK)
    blk = jnp.arange(n_blocks)[:, None, None]
    mask = in_window[None] & ((blk > 0) | (kj[None] >= ATT_BLOCK))
    return bias, mask


def _shared_kv(x, w_kv):
    bsz, seq, _ = x.shape
    n_blocks = seq // ATT_BLOCK
    k, v = jnp.split(x @ w_kv, 2, axis=-1)

    def windows(t):
        t = t.reshape(bsz, seq, ATT_KV_HEADS, ATT_HEAD_DIM)
        t = jnp.pad(t, ((0, 0), (ATT_BLOCK, 0), (0, 0), (0, 0)))
        t = t.reshape(bsz, n_blocks + 1, ATT_BLOCK, ATT_KV_HEADS, ATT_HEAD_DIM)
        return jnp.concatenate([t[:, :-1], t[:, 1:]], axis=2)

    return windows(k), windows(v)


def _swa_sink(x, k_win, v_win, w_q, sinks, w_out, bias, mask):
    bsz, seq, _ = x.shape
    n_blocks = seq // ATT_BLOCK
    q = (x @ w_q).reshape(bsz, n_blocks, ATT_BLOCK, ATT_KV_HEADS, ATT_GROUP, ATT_HEAD_DIM)
    s = jnp.einsum('bnqhgd,bnkhd->bhgnqk', q, k_win).astype(F32) * (ATT_HEAD_DIM ** -0.5) + bias
    s = jnp.where(mask, s, NEG_BIG)
    sink = sinks.astype(F32).reshape(1, ATT_KV_HEADS, ATT_GROUP, 1, 1, 1)
    m = jnp.maximum(jnp.max(s, -1, keepdims=True), sink)
    p = jnp.exp(s - m)
    w = p / (jnp.sum(p, -1, keepdims=True) + jnp.exp(sink - m))
    o = jnp.einsum('bhgnqk,bnkhd->bnqhgd', w.astype(x.dtype), v_win)
    return o.reshape(bsz, seq, ATT_Q_HEADS * ATT_HEAD_DIM) @ w_out


def _hier_moe(x, w_rg, b_rg, w_re, b_re, w_gate, w_up, w_down):
    bsz, seq, d = x.shape
    n_tok = bsz * seq
    xf = x.reshape(n_tok, d)
    g_logits = (xf @ w_rg).astype(F32) + b_rg.astype(F32)
    g_prob = jax.nn.softmax(g_logits, -1)
    grp = jnp.argmax(g_logits, -1).astype(jnp.int32)
    p_grp = jnp.take_along_axis(g_prob, grp[:, None], -1)
    e_logits = ((xf @ w_re).astype(F32) + b_re.astype(F32)).reshape(n_tok, N_GROUPS, EXPERTS_PER_GROUP)
    e_in_grp = jnp.take_along_axis(e_logits, grp[:, None, None], 1)[:, 0]
    top_v, top_i = lax.top_k(e_in_grp, TOP_K)
    gates = p_grp * jax.nn.softmax(top_v, -1)
    expert = grp[:, None] * EXPERTS_PER_GROUP + top_i.astype(jnp.int32)

    n_asg = n_tok * TOP_K
    e_flat = expert.reshape(-1)
    g_flat = gates.reshape(-1)
    tok_flat = jnp.repeat(jnp.arange(n_tok, dtype=jnp.int32), TOP_K)
    order = jnp.argsort(e_flat)
    e_sorted = e_flat[order]
    counts = jnp.zeros((N_EXPERTS,), jnp.int32).at[e_flat].add(1)
    starts = jnp.cumsum(counts) - counts
    padded = (counts + MOE_BLOCK - 1) // MOE_BLOCK * MOE_BLOCK
    pad_ends = jnp.cumsum(padded)
    pad_starts = pad_ends - padded
    rank = jnp.arange(n_asg, dtype=jnp.int32) - starts[e_sorted]
    dest = pad_starts[e_sorted] + rank
    n_blocks = -(-n_asg // MOE_BLOCK) + N_EXPERTS
    n_slots = n_blocks * MOE_BLOCK
    slot_tok = jnp.full((n_slots,), n_tok, jnp.int32).at[dest].set(tok_flat[order])
    slot_gate = jnp.zeros((n_slots,), F32).at[dest].set(g_flat[order])
    blk_start = jnp.arange(n_blocks, dtype=jnp.int32) * MOE_BLOCK
    blk_expert = jnp.minimum(jnp.searchsorted(pad_ends, blk_start, side='right'), N_EXPERTS - 1)
    x_pad = jnp.concatenate([xf, jnp.zeros((1, d), xf.dtype)], 0)
    xs = x_pad[slot_tok].reshape(n_blocks, MOE_BLOCK, d)

    def expert_block(args):
        xb, e = args
        h = jax.nn.silu(xb @ w_gate[e]) * (xb @ w_up[e])
        return h @ w_down[e]

    ys = lax.map(expert_block, (xs, blk_expert)).reshape(n_slots, d)
    out = jnp.zeros((n_tok + 1, d), F32).at[slot_tok].add(ys.astype(F32) * slot_gate[:, None])[:n_tok]
    return out.reshape(bsz, seq, d).astype(x.dtype)


def setup_inputs(seed: int = 0) -> dict:
    key = jax.random.key(seed)
    ks = jax.random.split(key, 24)
    d = D_MODEL
    kvw = ATT_KV_HEADS * ATT_HEAD_DIM

    def nrm(k, shape, fan_in, scale=1.0):
        return jax.random.normal(k, shape, F32) * (scale * fan_in ** -0.5)

    x = jax.random.normal(ks[0], (BATCH, SEQ, d), F32)
    a_w_in = nrm(ks[1], (N_A_LAYERS, d, 4 * d), d)
    a_w_in = a_w_in.at[:, :, 2 * d:3 * d].multiply(DEEPNORM_BETA)
    a_lower_bound = jax.random.normal(ks[2], (N_A_LAYERS, d), F32)
    a_norm_g = 1.0 + 0.02 * jax.random.normal(ks[3], (N_A_LAYERS, d), F32)
    a_w_out = nrm(ks[4], (N_A_LAYERS, d, d), d, DEEPNORM_BETA)
    b_w_kv = nrm(ks[5], (d, 2 * kvw), d)
    b_w_kv = b_w_kv.at[:, kvw:].multiply(DEEPNORM_BETA)
    b_w_q = nrm(ks[6], (N_B_LAYERS, d, ATT_Q_HEADS * ATT_HEAD_DIM), d)
    b_sinks = jax.random.normal(ks[7], (N_B_LAYERS, ATT_Q_HEADS), F32)
    b_w_out = nrm(ks[8], (N_B_LAYERS, ATT_Q_HEADS * ATT_HEAD_DIM, d), d, DEEPNORM_BETA)
    rel_bias = 0.5 * jax.random.normal(ks[9], (N_BUCKETS, ATT_Q_HEADS), F32)
    moe_w_rg = nrm(ks[10], (DEPTH, d, N_GROUPS), d)
    moe_b_rg = 0.01 * jax.random.normal(ks[11], (DEPTH, N_GROUPS), F32)
    moe_w_re = nrm(ks[12], (DEPTH, d, N_EXPERTS), d)
    moe_b_re = 0.01 * jax.random.normal(ks[13], (DEPTH, N_EXPERTS), F32)
    moe_w_gate = nrm(ks[14], (DEPTH, N_EXPERTS, d, D_EXPERT), d)
    moe_w_up = nrm(ks[15], (DEPTH, N_EXPERTS, d, D_EXPERT), d)
    moe_w_down = nrm(ks[16], (DEPTH, N_EXPERTS, D_EXPERT, d), D_EXPERT, DEEPNORM_BETA)
    ln_g = 1.0 + 0.02 * jax.random.normal(ks[17], (2 * DEPTH, d), F32)
    ln_b = 0.02 * jax.random.normal(ks[18], (2 * DEPTH, d), F32)
    return {"x": x, "a_w_in": a_w_in, "a_lower_bound": a_lower_bound, "a_norm_g": a_norm_g,
            "a_w_out": a_w_out, "b_w_kv": b_w_kv, "b_w_q": b_w_q, "b_sinks": b_sinks,
            "b_w_out": b_w_out, "rel_bias": rel_bias, "moe_w_rg": moe_w_rg, "moe_b_rg": moe_b_rg,
            "moe_w_re": moe_w_re, "moe_b_re": moe_b_re, "moe_w_gate": moe_w_gate,
            "moe_w_up": moe_w_up, "moe_w_down": moe_w_down, "ln_g": ln_g, "ln_b": ln_b}


def reference(x, a_w_in, a_lower_bound, a_norm_g, a_w_out, b_w_kv, b_w_q, b_sinks, b_w_out,
              rel_bias, moe_w_rg, moe_b_rg, moe_w_re, moe_b_re, moe_w_gate, moe_w_up,
              moe_w_down, ln_g, ln_b):
    lb_sm = jax.nn.softmax(a_lower_bound.astype(F32), axis=0)
    lower_bounds = jnp.cumsum(lb_sm, axis=0) - lb_sm[0]
    att_bias, att_mask = _band_bias_and_mask(rel_bias, x.shape[1])
    k_win = None
    v_win = None
    for layer in range(DEPTH):
        if layer < N_A_LAYERS:
            h = _hgrn2(x, a_w_in[layer], lower_bounds[layer], a_norm_g[layer], a_w_out[layer])
        else:
            if layer == N_A_LAYERS:
                k_win, v_win = _shared_kv(x, b_w_kv)
            j = layer - N_A_LAYERS
            h = _swa_sink(x, k_win, v_win, b_w_q[j], b_sinks[j], b_w_out[j], att_bias, att_mask)
        x = _layernorm(DEEPNORM_ALPHA * x + h, ln_g[2 * layer], ln_b[2 * layer])
        f = _hier_moe(x, moe_w_rg[layer], moe_b_rg[layer], moe_w_re[layer], moe_b_re[layer],
                      moe_w_gate[layer], moe_w_up[layer], moe_w_down[layer])
        x = _layernorm(DEEPNORM_ALPHA * x + f, ln_g[2 * layer + 1], ln_b[2 * layer + 1])
    return x
```

```python
import functools
import math

import numpy as np
import jax
import jax.numpy as jnp
from jax import lax
from jax.experimental import pallas as pl
from jax.experimental.pallas import tpu as pltpu

F32 = jnp.float32
BF16 = jnp.bfloat16

D_MODEL = 2048
DEPTH = 4
N_A_LAYERS = DEPTH // 2
HG_HEADS = 16
HG_DK = 128
HG_DV = 128
ATT_HEAD_DIM = 64
ATT_Q_HEADS = 32
ATT_KV_HEADS = 4
ATT_GROUP = ATT_Q_HEADS // ATT_KV_HEADS
WINDOW = 128
N_BUCKETS = 32
REL_MAX_DISTANCE = 128
N_GROUPS = 4
EXPERTS_PER_GROUP = 8
N_EXPERTS = N_GROUPS * EXPERTS_PER_GROUP
TOP_K = 2
D_EXPERT = D_MODEL // 4
DEEPNORM_ALPHA = (2 * DEPTH) ** 0.25
LN_EPS = 1e-5
RMS_EPS = 1e-6
NEG_BIG = -1e30
MIN_FORGET = 1e-30

LANES = 128
HG_CHUNK = 128
HG_LEVELS = 7
MOE_TILE = 256
ROUTER_COLS = 128
VMEM_LIMIT = 56 * 1024 * 1024


def _params(*sem):
    return pltpu.CompilerParams(dimension_semantics=sem, vmem_limit_bytes=VMEM_LIMIT)


def _mm_kernel(x_ref, w_ref, o_ref, wb_ref):
    @pl.when(pl.program_id(1) == 0)
    def _():
        wb_ref[...] = w_ref[...].astype(BF16)

    o_ref[...] = jnp.dot(x_ref[...], wb_ref[...], preferred_element_type=F32).astype(o_ref.dtype)


def _matmul(x, w, out_dtype, tm=1024, tn=1024):
    m, k = x.shape
    n = w.shape[1]
    tm, tn = min(tm, m), min(tn, n)
    return pl.pallas_call(
        _mm_kernel,
        grid=(n // tn, m // tm),
        in_specs=[pl.BlockSpec((tm, k), lambda j, i: (i, 0)),
                  pl.BlockSpec((k, tn), lambda j, i: (0, j))],
        out_specs=pl.BlockSpec((tm, tn), lambda j, i: (i, j)),
        out_shape=jax.ShapeDtypeStruct((m, n), out_dtype),
        scratch_shapes=[pltpu.VMEM((k, tn), BF16)],
        compiler_params=_params("arbitrary", "arbitrary"),
    )(x, w)


def _ln_kernel(x_ref, h_ref, g_ref, b_ref, wr_ref, br_ref, xo_ref, xb_ref, lg_ref):
    v = DEEPNORM_ALPHA * x_ref[...] + h_ref[...]
    mu = jnp.mean(v, -1, keepdims=True)
    d = v - mu
    var = jnp.mean(d * d, -1, keepdims=True)
    y = d * lax.rsqrt(var + LN_EPS) * g_ref[...] + b_ref[...]
    xo_ref[...] = y
    xb_ref[...] = y.astype(BF16)
    lg_ref[...] = jnp.dot(y, wr_ref[...], preferred_element_type=F32,
                          precision=lax.Precision.HIGHEST) + br_ref[...]


def _ln_router(x, h, g, b, wr, br, tm=256):
    t, d = x.shape
    row = lambda i: (i, 0)
    const = lambda i: (0, 0)
    return pl.pallas_call(
        _ln_kernel,
        grid=(t // tm,),
        in_specs=[pl.BlockSpec((tm, d), row), pl.BlockSpec((tm, d), row),
                  pl.BlockSpec((1, d), const), pl.BlockSpec((1, d), const),
                  pl.BlockSpec((d, ROUTER_COLS), const), pl.BlockSpec((1, ROUTER_COLS), const)],
        out_specs=[pl.BlockSpec((tm, d), row), pl.BlockSpec((tm, d), row),
                   pl.BlockSpec((tm, ROUTER_COLS), row)],
        out_shape=[jax.ShapeDtypeStruct((t, d), F32), jax.ShapeDtypeStruct((t, d), BF16),
                   jax.ShapeDtypeStruct((t, ROUTER_COLS), F32)],
        compiler_params=_params("arbitrary"),
    )(x, h, g.reshape(1, d), b.reshape(1, d), wr, br)


def _hgrn_level_matrix():
    c = HG_CHUNK
    t = np.arange(c)[:, None]
    s = np.arange(c)[None, :]
    mats = [(s <= t).astype(np.float32)]
    for lvl in range(HG_LEVELS):
        m = 1 << lvl
        r = (t // (2 * m)) * (2 * m) + m - 1
        mats.append((s <= t).astype(np.float32) - (s <= r).astype(np.float32))
    return np.concatenate(mats, axis=0)


def _hgrn_kernel(q_ref, f_ref, i_ref, g_ref, lb_ref, ng_ref, lm_ref, o_ref, st_ref):
    c = HG_CHUNK

    @pl.when(pl.program_id(2) == 0)
    def _():
        st_ref[...] = jnp.zeros_like(st_ref)

    q = q_ref[...]
    v = i_ref[...]
    lb = lb_ref[...]
    sig = jax.nn.sigmoid(f_ref[...])
    logf = jnp.log(jnp.maximum(lb + (1.0 - lb) * sig, MIN_FORGET))
    k = (1.0 - lb) * (1.0 - sig)

    g_hi = logf.astype(BF16)
    g_lo = (logf - g_hi.astype(F32)).astype(BF16)
    a2 = jnp.dot(lm_ref[...], jnp.concatenate([g_hi, g_lo], axis=1), preferred_element_type=F32)
    a = a2[:, :LANES] + a2[:, LANES:]
    b = a[:c]
    b_last = b[c - 1:c, :]

    st = st_ref[...]
    nt = (((1,), (1,)), ((), ()))
    o = lax.dot_general((q * jnp.exp(b)).astype(BF16), st.astype(BF16), nt, preferred_element_type=F32)

    row = lax.broadcasted_iota(jnp.int32, (c, LANES), 0)
    ti = lax.broadcasted_iota(jnp.int32, (c, c), 0)
    si = lax.broadcasted_iota(jnp.int32, (c, c), 1)
    scores = jnp.where(ti == si, lax.dot_general(q.astype(BF16), k.astype(BF16), nt, preferred_element_type=F32), 0.0)
    for lvl in range(HG_LEVELS):
        e = jnp.exp(-jnp.abs(a[(lvl + 1) * c:(lvl + 2) * c]))
        upper = ((row >> lvl) & 1) == 1
        qm = jnp.where(upper, q * e, 0.0).astype(BF16)
        km = jnp.where(upper, 0.0, k * e).astype(BF16)
        s_l = lax.dot_general(qm, km, nt, preferred_element_type=F32)
        scores = scores + jnp.where((ti >> (lvl + 1)) == (si >> (lvl + 1)), s_l, 0.0)
    vb = v.astype(BF16)
    o = o + jnp.dot(scores.astype(BF16), vb, preferred_element_type=F32)

    ks = (k * jnp.exp(b_last - b)).astype(BF16)
    st_ref[...] = st * jnp.exp(b_last) + jnp.dot(v.T.astype(BF16), ks, preferred_element_type=F32)

    o = o * lax.rsqrt(jnp.mean(o * o, -1, keepdims=True) + RMS_EPS) * ng_ref[...]
    gate = g_ref[...]
    o_ref[...] = (o * (gate * jax.nn.sigmoid(gate))).astype(o_ref.dtype)


def _hgrn(proj, lb, norm_g, bsz, seq):
    t = bsz * seq
    nc = seq // HG_CHUNK
    c = HG_CHUNK
    lm = jnp.asarray(_hgrn_level_matrix(), BF16)

    def part(p):
        return pl.BlockSpec((c, HG_DK), lambda b, h, n: (b * nc + n, p * HG_HEADS + h))

    head = pl.BlockSpec((1, HG_DK), lambda b, h, n: (0, h))
    return pl.pallas_call(
        _hgrn_kernel,
        grid=(bsz, HG_HEADS, nc),
        in_specs=[part(0), part(1), part(2), part(3), head, head,
                  pl.BlockSpec(((HG_LEVELS + 1) * c, c), lambda b, h, n: (0, 0))],
        out_specs=pl.BlockSpec((c, HG_DV), lambda b, h, n: (b * nc + n, h)),
        out_shape=jax.ShapeDtypeStruct((t, D_MODEL), BF16),
        scratch_shapes=[pltpu.VMEM((HG_DV, HG_DK), F32)],
        compiler_params=_params("arbitrary", "arbitrary", "arbitrary"),
    )(proj, proj, proj, proj, lb.reshape(1, D_MODEL), norm_g.reshape(1, D_MODEL), lm)


def _swa_kernel(sink_ref, q_ref, kvp_ref, kvc_ref, bias_ref, o_ref):
    n = pl.program_id(1)
    w = WINDOW
    qi = lax.broadcasted_iota(jnp.int32, (w, 2 * w), 0)
    kj = lax.broadcasted_iota(jnp.int32, (w, 2 * w), 1)
    dist = qi + w - kj
    mask = (dist >= 0) & (dist < w) & ((n > 0) | (kj >= w))
    kvw = ATT_KV_HEADS * ATT_HEAD_DIM
    nt = (((1,), (1,)), ((), ()))
    for h in range(ATT_Q_HEADS):
        kh = h // ATT_GROUP
        ks = slice(kh * ATT_HEAD_DIM, (kh + 1) * ATT_HEAD_DIM)
        vs = slice(kvw + kh * ATT_HEAD_DIM, kvw + (kh + 1) * ATT_HEAD_DIM)
        kwin = jnp.concatenate([kvp_ref[:, ks], kvc_ref[:, ks]], axis=0)
        vwin = jnp.concatenate([kvp_ref[:, vs], kvc_ref[:, vs]], axis=0)
        qh = q_ref[:, h * ATT_HEAD_DIM:(h + 1) * ATT_HEAD_DIM]
        s = lax.dot_general(qh, kwin, nt, preferred_element_type=F32) * (ATT_HEAD_DIM ** -0.5) + bias_ref[h]
        s = jnp.where(mask, s, NEG_BIG)
        sink = sink_ref[h]
        m = jnp.maximum(jnp.max(s, -1, keepdims=True), sink)
        p = jnp.exp(s - m)
        denom = jnp.sum(p, -1, keepdims=True) + jnp.exp(sink - m)
        oh = jnp.dot(p.astype(BF16), vwin, preferred_element_type=F32) / denom
        o_ref[:, h * ATT_HEAD_DIM:(h + 1) * ATT_HEAD_DIM] = oh.astype(o_ref.dtype)


def _swa(q, kv, bias, sinks, bsz, seq):
    t = bsz * seq
    nb = seq // WINDOW
    kvw2 = 2 * ATT_KV_HEADS * ATT_HEAD_DIM
    return pl.pallas_call(
        _swa_kernel,
        grid=(bsz, nb),
        in_specs=[pl.BlockSpec(memory_space=pltpu.SMEM),
                  pl.BlockSpec((WINDOW, D_MODEL), lambda b, n: (b * nb + n, 0)),
                  pl.BlockSpec((WINDOW, kvw2), lambda b, n: (b * nb + jnp.maximum(n - 1, 0), 0)),
                  pl.BlockSpec((WINDOW, kvw2), lambda b, n: (b * nb + n, 0)),
                  pl.BlockSpec((ATT_Q_HEADS, WINDOW, 2 * WINDOW), lambda b, n: (0, 0, 0))],
        out_specs=pl.BlockSpec((WINDOW, D_MODEL), lambda b, n: (b * nb + n, 0)),
        out_shape=jax.ShapeDtypeStruct((t, D_MODEL), BF16),
        compiler_params=_params("arbitrary", "arbitrary"),
    )(sinks, q, kv, kv, bias)


def _t5_bucket(dist):
    n = jnp.clip(dist, 0, REL_MAX_DISTANCE - 1)
    max_exact = N_BUCKETS // 2
    large = max_exact + (jnp.log(jnp.maximum(n, max_exact).astype(F32) / max_exact)
                         / math.log(REL_MAX_DISTANCE / max_exact)
                         * (N_BUCKETS - max_exact)).astype(jnp.int32)
    large = jnp.minimum(large, N_BUCKETS - 1)
    return jnp.where(n < max_exact, n, large)


def _band_bias(rel_bias):
    qi = jnp.arange(WINDOW)[:, None]
    kj = jnp.arange(2 * WINDOW)[None, :]
    return rel_bias.astype(F32)[_t5_bucket(qi + WINDOW - kj)].transpose(2, 0, 1)


def _moe_kernel(te_ref, nu_ref, xs_ref, wg_ref, wu_ref, wd_ref, ys_ref, wgb_ref, wub_ref, wdb_ref):
    i = pl.program_id(0)

    @pl.when(i < nu_ref[0])
    def _():
        @pl.when((i == 0) | (te_ref[i] != te_ref[jnp.maximum(i - 1, 0)]))
        def _():
            wgb_ref[...] = wg_ref[...].astype(BF16)
            wub_ref[...] = wu_ref[...].astype(BF16)
            wdb_ref[...] = wd_ref[...].astype(BF16)

        x = xs_ref[...]
        hg = jnp.dot(x, wgb_ref[...], preferred_element_type=F32)
        hu = jnp.dot(x, wub_ref[...], preferred_element_type=F32)
        hidden = (hg * jax.nn.sigmoid(hg) * hu).astype(BF16)
        ys_ref[...] = jnp.dot(hidden, wdb_ref[...], preferred_element_type=F32)

    @pl.when(i >= nu_ref[0])
    def _():
        ys_ref[...] = jnp.zeros_like(ys_ref)


def _moe_experts(xs, tile_expert, n_used, w_gate, w_up, w_down):
    n_slots, d = xs.shape
    n_tiles = n_slots // MOE_TILE
    return pl.pallas_call(
        _moe_kernel,
        grid_spec=pltpu.PrefetchScalarGridSpec(
            num_scalar_prefetch=2,
            grid=(n_tiles,),
            in_specs=[pl.BlockSpec((MOE_TILE, d), lambda i, te, nu: (i, 0)),
                      pl.BlockSpec((None, d, D_EXPERT), lambda i, te, nu: (te[i], 0, 0)),
                      pl.BlockSpec((None, d, D_EXPERT), lambda i, te, nu: (te[i], 0, 0)),
                      pl.BlockSpec((None, D_EXPERT, d), lambda i, te, nu: (te[i], 0, 0))],
            out_specs=pl.BlockSpec((MOE_TILE, d), lambda i, te, nu: (i, 0)),
            scratch_shapes=[pltpu.VMEM((d, D_EXPERT), BF16), pltpu.VMEM((d, D_EXPERT), BF16),
                            pltpu.VMEM((D_EXPERT, d), BF16)]),
        out_shape=jax.ShapeDtypeStruct((n_slots, d), F32),
        compiler_params=_params("arbitrary"),
    )(tile_expert, n_used, xs, w_gate, w_up, w_down)


def _hier_moe(xb, logits, w_gate, w_up, w_down):
    n_tok, d = xb.shape
    g_logits = logits[:, :N_GROUPS]
    g_prob = jax.nn.softmax(g_logits, -1)
    grp = jnp.argmax(g_logits, -1).astype(jnp.int32)
    p_grp = jnp.take_along_axis(g_prob, grp[:, None], -1)
    e_logits = logits[:, N_GROUPS:N_GROUPS + N_EXPERTS].reshape(n_tok, N_GROUPS, EXPERTS_PER_GROUP)
    e_in_grp = jnp.take_along_axis(e_logits, grp[:, None, None], 1)[:, 0]
    top_v, top_i = lax.top_k(e_in_grp, TOP_K)
    gates = p_grp * jax.nn.softmax(top_v, -1)
    expert = grp[:, None] * EXPERTS_PER_GROUP + top_i.astype(jnp.int32)

    n_asg = n_tok * TOP_K
    e_flat = expert.reshape(-1)
    tok_flat = jnp.repeat(jnp.arange(n_tok, dtype=jnp.int32), TOP_K)
    order = jnp.argsort(e_flat)
    e_sorted = e_flat[order]
    counts = jnp.zeros((N_EXPERTS,), jnp.int32).at[e_flat].add(1)
    starts = jnp.cumsum(counts) - counts
    padded = (counts + MOE_TILE - 1) // MOE_TILE * MOE_TILE
    pad_ends = jnp.cumsum(padded)
    pad_starts = pad_ends - padded
    dest = pad_starts[e_sorted] + jnp.arange(n_asg, dtype=jnp.int32) - starts[e_sorted]
    n_tiles = -(-n_asg // MOE_TILE) + N_EXPERTS
    n_slots = n_tiles * MOE_TILE
    slot_tok = jnp.full((n_slots,), n_tok, jnp.int32).at[dest].set(tok_flat[order])
    asg_slot = jnp.zeros((n_asg,), jnp.int32).at[order].set(dest)
    n_used = (pad_ends[-1] // MOE_TILE).astype(jnp.int32)
    tile_start = jnp.arange(n_tiles, dtype=jnp.int32) * MOE_TILE
    tile_expert = jnp.minimum(jnp.searchsorted(pad_ends, tile_start, side='right'), N_EXPERTS - 1)
    tile_expert = jnp.where(tile_start < pad_ends[-1], tile_expert,
                            tile_expert[jnp.maximum(n_used - 1, 0)]).astype(jnp.int32)

    x_pad = jnp.concatenate([xb, jnp.zeros((1, d), xb.dtype)], 0)
    xs = x_pad[slot_tok]
    ys = _moe_experts(xs, tile_expert, n_used.reshape(1), w_gate, w_up, w_down)
    y_tok = ys[asg_slot].reshape(n_tok, TOP_K, d)
    return y_tok[:, 0] * gates[:, 0:1] + y_tok[:, 1] * gates[:, 1:2]


def kernel(x, a_w_in, a_lower_bound, a_norm_g, a_w_out, b_w_kv, b_w_q, b_sinks, b_w_out, rel_bias,
           moe_w_rg, moe_b_rg, moe_w_re, moe_b_re, moe_w_gate, moe_w_up, moe_w_down, ln_g, ln_b):
    bsz, seq, d = x.shape
    t = bsz * seq
    lb_sm = jax.nn.softmax(a_lower_bound.astype(F32), axis=0)
    lower_bounds = jnp.cumsum(lb_sm, axis=0) - lb_sm[0]
    att_bias = _band_bias(rel_bias)
    pad = ROUTER_COLS - N_GROUPS - N_EXPERTS
    w_router = jnp.concatenate([moe_w_rg, moe_w_re, jnp.zeros((DEPTH, d, pad), F32)], axis=-1)
    b_router = jnp.concatenate([moe_b_rg, moe_b_re, jnp.zeros((DEPTH, pad), F32)], axis=-1)
    zero_w = jnp.zeros((d, ROUTER_COLS), F32)
    zero_b = jnp.zeros((1, ROUTER_COLS), F32)

    xf = x.reshape(t, d).astype(F32)
    xb = xf.astype(BF16)
    kv = None
    for layer in range(DEPTH):
        if layer < N_A_LAYERS:
            proj = _matmul(xb, a_w_in[layer], F32)
            o = _hgrn(proj, lower_bounds[layer], a_norm_g[layer], bsz, seq)
            h = _matmul(o, a_w_out[layer], F32)
        else:
            j = layer - N_A_LAYERS
            if kv is None:
                kv = _matmul(xb, b_w_kv, BF16)
            q = _matmul(xb, b_w_q[j], BF16)
            o = _swa(q, kv, att_bias, b_sinks[j].astype(F32), bsz, seq)
            h = _matmul(o, b_w_out[j], F32)
        xf, xb, logits = _ln_router(xf, h, ln_g[2 * layer], ln_b[2 * layer],
                                    w_router[layer], b_router[layer].reshape(1, ROUTER_COLS))
        f = _hier_moe(xb, logits, moe_w_gate[layer], moe_w_up[layer], moe_w_down[layer])
        xf, xb, _ = _ln_router(xf, f, ln_g[2 * layer + 1], ln_b[2 * layer + 1], zero_w, zero_b)
    return xf.reshape(bsz, seq, d).astype(x.dtype)
```

```python
import math

import numpy as np
import jax
import jax.numpy as jnp
from jax import lax
from jax.experimental import pallas as pl
from jax.experimental.pallas import tpu as pltpu

F32 = jnp.float32
BF16 = jnp.bfloat16

D_MODEL = 2048
DEPTH = 4
N_A_LAYERS = DEPTH // 2
HG_HEADS = 16
HG_DK = 128
HG_DV = 128
ATT_HEAD_DIM = 64
ATT_Q_HEADS = 32
ATT_KV_HEADS = 4
ATT_GROUP = ATT_Q_HEADS // ATT_KV_HEADS
WINDOW = 128
N_BUCKETS = 32
REL_MAX_DISTANCE = 128
N_GROUPS = 4
EXPERTS_PER_GROUP = 8
N_EXPERTS = N_GROUPS * EXPERTS_PER_GROUP
TOP_K = 2
D_EXPERT = D_MODEL // 4
DEEPNORM_ALPHA = (2 * DEPTH) ** 0.25
LN_EPS = 1e-5
RMS_EPS = 1e-6
NEG_BIG = -1e30
MIN_FORGET = 1e-30

LANES = 128
HG_CHUNK = 128
HG_LEVELS = 7
MOE_TILE = 256
ROUTER_COLS = 128
VMEM_LIMIT = 56 * 1024 * 1024


def _params(*sem):
    return pltpu.CompilerParams(dimension_semantics=sem, vmem_limit_bytes=VMEM_LIMIT)


def _mm_kernel(x_ref, w_ref, o_ref, wb_ref):
    @pl.when(pl.program_id(1) == 0)
    def _():
        wb_ref[...] = w_ref[...].astype(BF16)

    o_ref[...] = jnp.dot(x_ref[...], wb_ref[...], preferred_element_type=F32).astype(o_ref.dtype)


def _matmul(x, w, layer, out_dtype, tm=1024, tn=1024):
    m, k = x.shape
    n = w.shape[2]
    tm, tn = min(tm, m), min(tn, n)
    return pl.pallas_call(
        _mm_kernel,
        grid=(n // tn, m // tm),
        in_specs=[pl.BlockSpec((tm, k), lambda j, i: (i, 0)),
                  pl.BlockSpec((None, k, tn), lambda j, i: (layer, 0, j))],
        out_specs=pl.BlockSpec((tm, tn), lambda j, i: (i, j)),
        out_shape=jax.ShapeDtypeStruct((m, n), out_dtype),
        scratch_shapes=[pltpu.VMEM((k, tn), BF16)],
        compiler_params=_params("arbitrary", "arbitrary"),
        name="matmul",
    )(x, w)


def _layernorm(v, g, b):
    mu = jnp.mean(v, -1, keepdims=True)
    d = v - mu
    var = jnp.mean(d * d, -1, keepdims=True)
    return d * lax.rsqrt(var + LN_EPS) * g + b


def _ln_router_kernel(x_ref, h_ref, g_ref, b_ref, wr_ref, br_ref, xo_ref, lg_ref):
    y = _layernorm(DEEPNORM_ALPHA * x_ref[...] + h_ref[...], g_ref[...], b_ref[...])
    xo_ref[...] = y
    lg_ref[...] = jnp.dot(y, wr_ref[...], preferred_element_type=F32,
                          precision=lax.Precision.HIGHEST) + br_ref[...]


def _ln_router(x, h, g, b, wr, br, tm=256):
    t, d = x.shape
    row = lambda i: (i, 0)
    const = lambda i: (0, 0)
    return pl.pallas_call(
        _ln_router_kernel,
        grid=(t // tm,),
        in_specs=[pl.BlockSpec((tm, d), row), pl.BlockSpec((tm, d), row),
                  pl.BlockSpec((1, d), const), pl.BlockSpec((1, d), const),
                  pl.BlockSpec((d, ROUTER_COLS), const), pl.BlockSpec((1, ROUTER_COLS), const)],
        out_specs=[pl.BlockSpec((tm, d), row), pl.BlockSpec((tm, ROUTER_COLS), row)],
        out_shape=[jax.ShapeDtypeStruct((t, d), F32), jax.ShapeDtypeStruct((t, ROUTER_COLS), F32)],
        compiler_params=_params("arbitrary"),
        name="ln_router",
    )(x, h, g.reshape(1, d), b.reshape(1, d), wr, br)


def _ln_moe_kernel(x_ref, y0_ref, y1_ref, g_ref, b_ref, xo_ref, xb_ref):
    y = _layernorm(DEEPNORM_ALPHA * x_ref[...] + (y0_ref[...] + y1_ref[...]), g_ref[...], b_ref[...])
    xo_ref[...] = y
    xb_ref[...] = y.astype(BF16)


def _ln_moe(x, y_planes, g, b, tm=256):
    t, d = x.shape
    row = lambda i: (i, 0)
    const = lambda i: (0, 0)
    return pl.pallas_call(
        _ln_moe_kernel,
        grid=(t // tm,),
        in_specs=[pl.BlockSpec((tm, d), row), pl.BlockSpec((tm, d), row),
                  pl.BlockSpec((tm, d), lambda i: (i + t // tm, 0)),
                  pl.BlockSpec((1, d), const), pl.BlockSpec((1, d), const)],
        out_specs=[pl.BlockSpec((tm, d), row), pl.BlockSpec((tm, d), row)],
        out_shape=[jax.ShapeDtypeStruct((t, d), F32), jax.ShapeDtypeStruct((t, d), BF16)],
        compiler_params=_params("arbitrary"),
        name="ln_moe",
    )(x, y_planes, y_planes, g.reshape(1, d), b.reshape(1, d))


def _hgrn_level_matrix():
    c = HG_CHUNK
    t = np.arange(c)[:, None]
    s = np.arange(c)[None, :]
    mats = [(s <= t).astype(np.float32)]
    for lvl in range(HG_LEVELS):
        m = 1 << lvl
        r = (t // (2 * m)) * (2 * m) + m - 1
        mats.append((s <= t).astype(np.float32) - (s <= r).astype(np.float32))
    return np.concatenate(mats, axis=0)


def _hgrn_kernel(q_ref, f_ref, i_ref, g_ref, lb_ref, ng_ref, lm_ref, o_ref, st_ref):
    c = HG_CHUNK

    @pl.when(pl.program_id(2) == 0)
    def _():
        st_ref[...] = jnp.zeros_like(st_ref)

    q = q_ref[...]
    v = i_ref[...]
    lb = lb_ref[...]
    sig = jax.nn.sigmoid(f_ref[...])
    logf = jnp.log(jnp.maximum(lb + (1.0 - lb) * sig, MIN_FORGET))
    k = (1.0 - lb) * (1.0 - sig)

    g_hi = logf.astype(BF16)
    g_lo = (logf - g_hi.astype(F32)).astype(BF16)
    a2 = jnp.dot(lm_ref[...], jnp.concatenate([g_hi, g_lo], axis=1), preferred_element_type=F32)
    a = a2[:, :LANES] + a2[:, LANES:]
    b = a[:c]
    b_last = b[c - 1:c, :]

    st = st_ref[...]
    nt = (((1,), (1,)), ((), ()))
    o = lax.dot_general((q * jnp.exp(b)).astype(BF16), st.astype(BF16), nt, preferred_element_type=F32)

    row = lax.broadcasted_iota(jnp.int32, (c, LANES), 0)
    ti = lax.broadcasted_iota(jnp.int32, (c, c), 0)
    si = lax.broadcasted_iota(jnp.int32, (c, c), 1)
    scores = jnp.where(ti == si, lax.dot_general(q.astype(BF16), k.astype(BF16), nt, preferred_element_type=F32), 0.0)
    for lvl in range(HG_LEVELS):
        e = jnp.exp(-jnp.abs(a[(lvl + 1) * c:(lvl + 2) * c]))
        upper = ((row >> lvl) & 1) == 1
        qm = jnp.where(upper, q * e, 0.0).astype(BF16)
        km = jnp.where(upper, 0.0, k * e).astype(BF16)
        s_l = lax.dot_general(qm, km, nt, preferred_element_type=F32)
        scores = scores + jnp.where((ti >> (lvl + 1)) == (si >> (lvl + 1)), s_l, 0.0)
    vb = v.astype(BF16)
    o = o + jnp.dot(scores.astype(BF16), vb, preferred_element_type=F32)

    ks = (k * jnp.exp(b_last - b)).astype(BF16)
    st_ref[...] = st * jnp.exp(b_last) + jnp.dot(v.T.astype(BF16), ks, preferred_element_type=F32)

    o = o * lax.rsqrt(jnp.mean(o * o, -1, keepdims=True) + RMS_EPS) * ng_ref[...]
    gate = g_ref[...]
    o_ref[...] = (o * (gate * jax.nn.sigmoid(gate))).astype(o_ref.dtype)


def _hgrn(proj, lb, norm_g, bsz, seq):
    t = bsz * seq
    nc = seq // HG_CHUNK
    c = HG_CHUNK
    lm = jnp.asarray(_hgrn_level_matrix(), BF16)

    def part(p):
        return pl.BlockSpec((c, HG_DK), lambda b, h, n: (b * nc + n, p * HG_HEADS + h))

    head = pl.BlockSpec((1, HG_DK), lambda b, h, n: (0, h))
    return pl.pallas_call(
        _hgrn_kernel,
        grid=(bsz, HG_HEADS, nc),
        in_specs=[part(0), part(1), part(2), part(3), head, head,
                  pl.BlockSpec(((HG_LEVELS + 1) * c, c), lambda b, h, n: (0, 0))],
        out_specs=pl.BlockSpec((c, HG_DV), lambda b, h, n: (b * nc + n, h)),
        out_shape=jax.ShapeDtypeStruct((t, D_MODEL), BF16),
        scratch_shapes=[pltpu.VMEM((HG_DV, HG_DK), F32)],
        compiler_params=_params("arbitrary", "arbitrary", "arbitrary"),
        name="hgrn2",
    )(proj, proj, proj, proj, lb.reshape(1, D_MODEL), norm_g.reshape(1, D_MODEL), lm)


def _swa_kernel(sink_ref, q_ref, kvp_ref, kvc_ref, bias_ref, o_ref):
    n = pl.program_id(1)
    w = WINDOW
    qi = lax.broadcasted_iota(jnp.int32, (w, 2 * w), 0)
    kj = lax.broadcasted_iota(jnp.int32, (w, 2 * w), 1)
    dist = qi + w - kj
    mask = (dist >= 0) & (dist < w) & ((n > 0) | (kj >= w))
    kvw = ATT_KV_HEADS * ATT_HEAD_DIM
    nt = (((1,), (1,)), ((), ()))
    for h in range(ATT_Q_HEADS):
        kh = h // ATT_GROUP
        ks = slice(kh * ATT_HEAD_DIM, (kh + 1) * ATT_HEAD_DIM)
        vs = slice(kvw + kh * ATT_HEAD_DIM, kvw + (kh + 1) * ATT_HEAD_DIM)
        kwin = jnp.concatenate([kvp_ref[:, ks], kvc_ref[:, ks]], axis=0)
        vwin = jnp.concatenate([kvp_ref[:, vs], kvc_ref[:, vs]], axis=0)
        qh = q_ref[:, h * ATT_HEAD_DIM:(h + 1) * ATT_HEAD_DIM]
        s = lax.dot_general(qh, kwin, nt, preferred_element_type=F32) * (ATT_HEAD_DIM ** -0.5) + bias_ref[h]
        s = jnp.where(mask, s, NEG_BIG)
        sink = sink_ref[h]
        m = jnp.maximum(jnp.max(s, -1, keepdims=True), sink)
        p = jnp.exp(s - m)
        denom = jnp.sum(p, -1, keepdims=True) + jnp.exp(sink - m)
        oh = jnp.dot(p.astype(BF16), vwin, preferred_element_type=F32) / denom
        o_ref[:, h * ATT_HEAD_DIM:(h + 1) * ATT_HEAD_DIM] = oh.astype(o_ref.dtype)


def _swa(q, kv, bias, sinks, bsz, seq):
    t = bsz * seq
    nb = seq // WINDOW
    kvw2 = 2 * ATT_KV_HEADS * ATT_HEAD_DIM
    return pl.pallas_call(
        _swa_kernel,
        grid=(bsz, nb),
        in_specs=[pl.BlockSpec(memory_space=pltpu.SMEM),
                  pl.BlockSpec((WINDOW, D_MODEL), lambda b, n: (b * nb + n, 0)),
                  pl.BlockSpec((WINDOW, kvw2), lambda b, n: (b * nb + jnp.maximum(n - 1, 0), 0)),
                  pl.BlockSpec((WINDOW, kvw2), lambda b, n: (b * nb + n, 0)),
                  pl.BlockSpec((ATT_Q_HEADS, WINDOW, 2 * WINDOW), lambda b, n: (0, 0, 0))],
        out_specs=pl.BlockSpec((WINDOW, D_MODEL), lambda b, n: (b * nb + n, 0)),
        out_shape=jax.ShapeDtypeStruct((t, D_MODEL), BF16),
        compiler_params=_params("arbitrary", "arbitrary"),
        name="swa",
    )(sinks, q, kv, kv, bias)


def _t5_bucket(dist):
    n = jnp.clip(dist, 0, REL_MAX_DISTANCE - 1)
    max_exact = N_BUCKETS // 2
    large = max_exact + (jnp.log(jnp.maximum(n, max_exact).astype(F32) / max_exact)
                         / math.log(REL_MAX_DISTANCE / max_exact)
                         * (N_BUCKETS - max_exact)).astype(jnp.int32)
    large = jnp.minimum(large, N_BUCKETS - 1)
    return jnp.where(n < max_exact, n, large)


def _band_bias(rel_bias):
    qi = jnp.arange(WINDOW)[:, None]
    kj = jnp.arange(2 * WINDOW)[None, :]
    return rel_bias.astype(F32)[_t5_bucket(qi + WINDOW - kj)].transpose(2, 0, 1)


def _moe_kernel(te_ref, nu_ref, src_ref, dst_ref,
                x_hbm, gate_ref, wg_ref, wu_ref, wd_ref, y_hbm,
                xbuf, ybuf, wgb_ref, wub_ref, wdb_ref, in_sem, out_sem):
    i = pl.program_id(0)
    n_used = nu_ref[0]
    rows = MOE_TILE

    def gather_row(tile, slot, r):
        return pltpu.make_async_copy(x_hbm.at[pl.ds(src_ref[tile * rows + r], 1)],
                                     xbuf.at[slot, pl.ds(r, 1)], in_sem.at[slot])

    def scatter_row(tile, r):
        return pltpu.make_async_copy(ybuf.at[pl.ds(r, 1)],
                                     y_hbm.at[pl.ds(dst_ref[tile * rows + r], 1)], out_sem.at[0])

    def start_gather(tile, slot):
        def body(r, carry):
            gather_row(tile, slot, r).start()
            return carry
        lax.fori_loop(0, rows, body, 0, unroll=8)

    def wait_gather(slot):
        pltpu.make_async_copy(xbuf.at[slot], xbuf.at[slot], in_sem.at[slot]).wait()

    def wait_scatter():
        pltpu.make_async_copy(ybuf, ybuf, out_sem.at[0]).wait()

    @pl.when(i < n_used)
    def _():
        slot = i % 2

        @pl.when(i == 0)
        def _():
            start_gather(0, 0)
            ybuf[...] = jnp.zeros_like(ybuf)
            dump = pltpu.make_async_copy(ybuf, y_hbm.at[pl.ds(y_hbm.shape[0] - rows, rows)], out_sem.at[0])
            dump.start()
            dump.wait()

        @pl.when(i + 1 < n_used)
        def _():
            start_gather(i + 1, 1 - slot)

        @pl.when((i == 0) | (te_ref[i] != te_ref[jnp.maximum(i - 1, 0)]))
        def _():
            wgb_ref[...] = wg_ref[...].astype(BF16)
            wub_ref[...] = wu_ref[...].astype(BF16)
            wdb_ref[...] = wd_ref[...].astype(BF16)

        wait_gather(slot)
        x = xbuf[slot].astype(BF16)
        hg = jnp.dot(x, wgb_ref[...], preferred_element_type=F32)
        hu = jnp.dot(x, wub_ref[...], preferred_element_type=F32)
        hidden = (hg * jax.nn.sigmoid(hg) * hu * gate_ref[...]).astype(BF16)

        @pl.when(i > 0)
        def _():
            wait_scatter()

        ybuf[...] = jnp.dot(hidden, wdb_ref[...], preferred_element_type=F32)

        def body(r, carry):
            scatter_row(i, r).start()
            return carry
        lax.fori_loop(0, rows, body, 0, unroll=8)

        @pl.when(i == n_used - 1)
        def _():
            wait_scatter()


def _moe_experts(xf, tile_expert, n_used, slot_src, slot_dst, slot_gate, w_gate, w_up, w_down, layer):
    t, d = xf.shape
    n_tiles = tile_expert.shape[0]
    wspec_in = pl.BlockSpec((None, None, d, D_EXPERT), lambda i, te, nu, s, dd: (layer, te[i], 0, 0))
    wspec_out = pl.BlockSpec((None, None, D_EXPERT, d), lambda i, te, nu, s, dd: (layer, te[i], 0, 0))
    return pl.pallas_call(
        _moe_kernel,
        grid_spec=pltpu.PrefetchScalarGridSpec(
            num_scalar_prefetch=4,
            grid=(n_tiles,),
            in_specs=[pl.BlockSpec(memory_space=pl.ANY),
                      pl.BlockSpec((MOE_TILE, 1), lambda i, te, nu, s, dd: (i, 0)),
                      wspec_in, wspec_in, wspec_out],
            out_specs=pl.BlockSpec(memory_space=pl.ANY),
            scratch_shapes=[pltpu.VMEM((2, MOE_TILE, d), F32), pltpu.VMEM((MOE_TILE, d), F32),
                            pltpu.VMEM((d, D_EXPERT), BF16), pltpu.VMEM((d, D_EXPERT), BF16),
                            pltpu.VMEM((D_EXPERT, d), BF16),
                            pltpu.SemaphoreType.DMA((2,)), pltpu.SemaphoreType.DMA((1,))]),
        out_shape=jax.ShapeDtypeStruct((TOP_K * t + MOE_TILE, d), F32),
        compiler_params=_params("arbitrary"),
        name="moe_experts",
    )(tile_expert, n_used, slot_src, slot_dst, xf, slot_gate, w_gate, w_up, w_down)


def _hier_moe(xf, logits, w_gate, w_up, w_down, layer):
    n_tok, d = xf.shape
    g_logits = logits[:, :N_GROUPS]
    g_prob = jax.nn.softmax(g_logits, -1)
    grp = jnp.argmax(g_logits, -1).astype(jnp.int32)
    p_grp = jnp.take_along_axis(g_prob, grp[:, None], -1)
    e_logits = logits[:, N_GROUPS:N_GROUPS + N_EXPERTS].reshape(n_tok, N_GROUPS, EXPERTS_PER_GROUP)
    e_in_grp = jnp.take_along_axis(e_logits, grp[:, None, None], 1)[:, 0]
    top_v, top_i = lax.top_k(e_in_grp, TOP_K)
    gates = p_grp * jax.nn.softmax(top_v, -1)
    expert = grp[:, None] * EXPERTS_PER_GROUP + top_i.astype(jnp.int32)

    n_asg = n_tok * TOP_K
    e_flat = expert.reshape(-1)
    order = jnp.argsort(e_flat).astype(jnp.int32)
    counts = jnp.sum((e_flat[:, None] == jnp.arange(N_EXPERTS, dtype=jnp.int32)[None, :]).astype(jnp.int32), 0)
    starts = jnp.cumsum(counts) - counts
    padded = (counts + MOE_TILE - 1) // MOE_TILE * MOE_TILE
    pad_ends = jnp.cumsum(padded)
    pad_starts = pad_ends - padded
    n_tiles = -(-n_asg // MOE_TILE) + N_EXPERTS
    n_used = (pad_ends[-1] // MOE_TILE).astype(jnp.int32)
    tile_start = jnp.arange(n_tiles, dtype=jnp.int32) * MOE_TILE
    tile_expert = jnp.minimum(jnp.searchsorted(pad_ends, tile_start, side='right'), N_EXPERTS - 1).astype(jnp.int32)
    tile_expert = jnp.where(tile_start < pad_ends[-1], tile_expert, tile_expert[jnp.maximum(n_used - 1, 0)])

    slot = jnp.arange(n_tiles * MOE_TILE, dtype=jnp.int32)
    slot_e = jnp.repeat(tile_expert, MOE_TILE)
    rank = slot - pad_starts[slot_e]
    valid = (rank < counts[slot_e]) & (slot < pad_ends[-1])
    asg = order[jnp.clip(starts[slot_e] + rank, 0, n_asg - 1)]
    slot_src = jnp.where(valid, asg // TOP_K, 0)
    slot_dst = jnp.where(valid, (asg % TOP_K) * n_tok + asg // TOP_K, TOP_K * n_tok + slot % MOE_TILE)
    slot_gate = jnp.where(valid, gates.reshape(-1)[asg], 0.0).reshape(-1, 1)
    return _moe_experts(xf, tile_expert, n_used.reshape(1), slot_src, slot_dst, slot_gate,
                        w_gate, w_up, w_down, layer)


def kernel(x, a_w_in, a_lower_bound, a_norm_g, a_w_out, b_w_kv, b_w_q, b_sinks, b_w_out, rel_bias,
           moe_w_rg, moe_b_rg, moe_w_re, moe_b_re, moe_w_gate, moe_w_up, moe_w_down, ln_g, ln_b):
    bsz, seq, d = x.shape
    t = bsz * seq
    lb_sm = jax.nn.softmax(a_lower_bound.astype(F32), axis=0)
    lower_bounds = jnp.cumsum(lb_sm, axis=0) - lb_sm[0]
    att_bias = _band_bias(rel_bias)
    pad = ROUTER_COLS - N_GROUPS - N_EXPERTS
    w_router = jnp.concatenate([moe_w_rg, moe_w_re, jnp.zeros((DEPTH, d, pad), F32)], axis=-1)
    b_router = jnp.concatenate([moe_b_rg, moe_b_re, jnp.zeros((DEPTH, pad), F32)], axis=-1)

    xf = x.reshape(t, d).astype(F32)
    xb = xf.astype(BF16)
    kv = None
    for layer in range(DEPTH):
        if layer < N_A_LAYERS:
            proj = _matmul(xb, a_w_in, layer, F32)
            o = _hgrn(proj, lower_bounds[layer], a_norm_g[layer], bsz, seq)
            h = _matmul(o, a_w_out, layer, F32)
        else:
            j = layer - N_A_LAYERS
            if kv is None:
                kv = _matmul(xb, b_w_kv[None], 0, BF16)
            q = _matmul(xb, b_w_q, j, BF16)
            o = _swa(q, kv, att_bias, b_sinks[j].astype(F32), bsz, seq)
            h = _matmul(o, b_w_out, j, F32)
        xf, logits = _ln_router(xf, h, ln_g[2 * layer], ln_b[2 * layer],
                                w_router[layer], b_router[layer].reshape(1, ROUTER_COLS))
        y_planes = _hier_moe(xf, logits, moe_w_gate, moe_w_up, moe_w_down, layer)
        xf, xb = _ln_moe(xf, y_planes, ln_g[2 * layer + 1], ln_b[2 * layer + 1])
    return xf.reshape(bsz, seq, d).astype(x.dtype)
```

```python
import math

import numpy as np
import jax
import jax.numpy as jnp
from jax import lax
from jax.experimental import pallas as pl
from jax.experimental.pallas import tpu as pltpu

F32 = jnp.float32
BF16 = jnp.bfloat16
U32 = jnp.uint32
I32 = jnp.int32

D_MODEL = 2048
DEPTH = 4
N_A_LAYERS = DEPTH // 2
HG_HEADS = 16
HG_DK = 128
HG_DV = 128
ATT_HEAD_DIM = 64
ATT_Q_HEADS = 32
ATT_KV_HEADS = 4
ATT_GROUP = ATT_Q_HEADS // ATT_KV_HEADS
WINDOW = 128
N_BUCKETS = 32
REL_MAX_DISTANCE = 128
N_GROUPS = 4
EXPERTS_PER_GROUP = 8
N_EXPERTS = N_GROUPS * EXPERTS_PER_GROUP
TOP_K = 2
D_EXPERT = D_MODEL // 4
DEEPNORM_ALPHA = (2 * DEPTH) ** 0.25
LN_EPS = 1e-5
RMS_EPS = 1e-6
NEG_BIG = -1e30
MIN_FORGET = 1e-30

LANES = 128
HG_CHUNK = 128
HG_LEVELS = 7
HG_HEADS_PER_STEP = 4
MOE_TILE = 256
ROUTER_COLS = 128
D_PACK = D_MODEL // 2
VMEM_LIMIT = 56 * 1024 * 1024
NT_DIMS = (((1,), (1,)), ((), ()))


def _params(*sem):
    return pltpu.CompilerParams(dimension_semantics=sem, vmem_limit_bytes=VMEM_LIMIT)


def _pack_bf16_pairs(y):
    half = y.shape[1] // 2
    lo = lax.bitcast_convert_type(y[:, :half].astype(BF16).astype(F32), U32)
    hi = lax.bitcast_convert_type(y[:, half:].astype(BF16).astype(F32), U32)
    return (hi & jnp.uint32(0xFFFF0000)) | (lo >> 16)


def _unpack_bf16_pairs(w):
    lo = lax.bitcast_convert_type(w << 16, F32)
    hi = lax.bitcast_convert_type(w & jnp.uint32(0xFFFF0000), F32)
    return jnp.concatenate([lo, hi], axis=1)


def _mm_kernel(x_ref, w_ref, o_ref, wb_ref):
    @pl.when(pl.program_id(1) == 0)
    def _():
        wb_ref[...] = w_ref[...].astype(BF16)

    o_ref[...] = jnp.dot(x_ref[...], wb_ref[...], preferred_element_type=F32).astype(o_ref.dtype)


def _matmul(x, w, layer, out_dtype, tm=1024, tn=1024):
    m, k = x.shape
    n = w.shape[2]
    tm, tn = min(tm, m), min(tn, n)
    return pl.pallas_call(
        _mm_kernel,
        grid=(n // tn, m // tm),
        in_specs=[pl.BlockSpec((tm, k), lambda j, i: (i, 0)),
                  pl.BlockSpec((None, k, tn), lambda j, i: (layer, 0, j))],
        out_specs=pl.BlockSpec((tm, tn), lambda j, i: (i, j)),
        out_shape=jax.ShapeDtypeStruct((m, n), out_dtype),
        scratch_shapes=[pltpu.VMEM((k, tn), BF16)],
        compiler_params=_params("arbitrary", "arbitrary"),
        name="matmul",
    )(x, w)


def _layernorm(v, g, b):
    mu = jnp.mean(v, -1, keepdims=True)
    d = v - mu
    var = jnp.mean(d * d, -1, keepdims=True)
    return d * lax.rsqrt(var + LN_EPS) * g + b


def _route(logits):
    lane = lax.broadcasted_iota(I32, logits.shape, 1)
    big = jnp.int32(ROUTER_COLS)
    is_grp = lane < N_GROUPS
    gl = jnp.where(is_grp, logits, -jnp.inf)
    g_max = jnp.max(gl, -1, keepdims=True)
    grp = jnp.min(jnp.where(is_grp & (gl == g_max), lane, big), -1, keepdims=True)
    p_grp = 1.0 / jnp.sum(jnp.exp(gl - g_max), -1, keepdims=True)
    in_grp = (lane >= N_GROUPS) & (((lane - N_GROUPS) >> 3) == grp)
    el = jnp.where(in_grp, logits, -jnp.inf)
    v1 = jnp.max(el, -1, keepdims=True)
    i1 = jnp.min(jnp.where(in_grp & (el == v1), lane, big), -1, keepdims=True)
    el2 = jnp.where(lane == i1, -jnp.inf, el)
    v2 = jnp.max(el2, -1, keepdims=True)
    i2 = jnp.min(jnp.where(in_grp & (lane != i1) & (el2 == v2), lane, big), -1, keepdims=True)
    ex = jnp.exp(v2 - v1)
    w1 = 1.0 / (1.0 + ex)
    return i1 - N_GROUPS, i2 - N_GROUPS, p_grp * w1, p_grp * (ex * w1)


def _ln_router_kernel(x_ref, h_ref, g_ref, b_ref, wr_ref, br_ref, xo_ref, xp_ref, rt_ref, cnt_ref):
    y = _layernorm(DEEPNORM_ALPHA * x_ref[...] + h_ref[...], g_ref[...], b_ref[...])
    xo_ref[...] = y
    xp_ref[...] = _pack_bf16_pairs(y)
    logits = jnp.dot(y, wr_ref[...], preferred_element_type=F32,
                     precision=lax.Precision.HIGHEST) + br_ref[...]
    e1, e2, g1, g2 = _route(logits)
    lane = lax.broadcasted_iota(I32, logits.shape, 1)
    rt_ref[...] = jnp.where(lane == 0, e1.astype(F32),
                            jnp.where(lane == 1, e2.astype(F32),
                                      jnp.where(lane == 2, g1, jnp.where(lane == 3, g2, 0.0))))

    @pl.when(pl.program_id(0) == 0)
    def _():
        cnt_ref[...] = jnp.zeros_like(cnt_ref)

    hits = (lane == e1).astype(F32) + (lane == e2).astype(F32)
    cnt_ref[...] += jnp.sum(hits, 0, keepdims=True)


def _ln_router(x, h, g, b, wr, br, tm=256):
    t, d = x.shape
    row = lambda i: (i, 0)
    const = lambda i: (0, 0)
    return pl.pallas_call(
        _ln_router_kernel,
        grid=(t // tm,),
        in_specs=[pl.BlockSpec((tm, d), row), pl.BlockSpec((tm, d), row),
                  pl.BlockSpec((1, d), const), pl.BlockSpec((1, d), const),
                  pl.BlockSpec((d, ROUTER_COLS), const), pl.BlockSpec((1, ROUTER_COLS), const)],
        out_specs=[pl.BlockSpec((tm, d), row), pl.BlockSpec((tm, D_PACK), row),
                   pl.BlockSpec((tm, ROUTER_COLS), row), pl.BlockSpec((1, ROUTER_COLS), const)],
        out_shape=[jax.ShapeDtypeStruct((t, d), F32), jax.ShapeDtypeStruct((t, D_PACK), U32),
                   jax.ShapeDtypeStruct((t, ROUTER_COLS), F32), jax.ShapeDtypeStruct((1, ROUTER_COLS), F32)],
        compiler_params=_params("arbitrary"),
        name="ln_router",
    )(x, h, g.reshape(1, d), b.reshape(1, d), wr, br)


def _ln_moe_kernel(x_ref, y0_ref, y1_ref, rt_ref, g_ref, b_ref, xo_ref, xb_ref):
    rt = rt_ref[...]
    f = rt[:, 2:3] * _unpack_bf16_pairs(y0_ref[...]) + rt[:, 3:4] * _unpack_bf16_pairs(y1_ref[...])
    y = _layernorm(DEEPNORM_ALPHA * x_ref[...] + f, g_ref[...], b_ref[...])
    xo_ref[...] = y
    xb_ref[...] = y.astype(BF16)


def _ln_moe(x, y_planes, route, g, b, tm=256):
    t, d = x.shape
    row = lambda i: (i, 0)
    const = lambda i: (0, 0)
    return pl.pallas_call(
        _ln_moe_kernel,
        grid=(t // tm,),
        in_specs=[pl.BlockSpec((tm, d), row), pl.BlockSpec((tm, D_PACK), row),
                  pl.BlockSpec((tm, D_PACK), lambda i: (i + t // tm, 0)),
                  pl.BlockSpec((tm, ROUTER_COLS), row),
                  pl.BlockSpec((1, d), const), pl.BlockSpec((1, d), const)],
        out_specs=[pl.BlockSpec((tm, d), row), pl.BlockSpec((tm, d), row)],
        out_shape=[jax.ShapeDtypeStruct((t, d), F32), jax.ShapeDtypeStruct((t, d), BF16)],
        compiler_params=_params("arbitrary"),
        name="ln_moe",
    )(x, y_planes, y_planes, route, g.reshape(1, d), b.reshape(1, d))


def _hgrn_constants():
    c = HG_CHUNK
    t = np.arange(c)[:, None]
    s = np.arange(c)[None, :]
    mats = [(s <= t).astype(np.float32)]
    masks = [(s == t).astype(np.float32)]
    for lvl in range(HG_LEVELS):
        m = 1 << lvl
        r = (t // (2 * m)) * (2 * m) + m - 1
        mats.append((s <= t).astype(np.float32) - (s <= r).astype(np.float32))
        masks.append(((t // (2 * m) == s // (2 * m)) & ((t // m) % 2 == 1) & ((s // m) % 2 == 0)).astype(np.float32))
    return np.concatenate(mats, axis=0), np.stack(masks, axis=0)


def _hgrn_kernel(q_ref, f_ref, i_ref, g_ref, lb_ref, ng_ref, lm_ref, pm_ref, o_ref, st_ref):
    c = HG_CHUNK
    nh = HG_HEADS_PER_STEP

    @pl.when(pl.program_id(2) == 0)
    def _():
        st_ref[...] = jnp.zeros_like(st_ref)

    lb = lb_ref[...]
    sig = jax.nn.sigmoid(f_ref[...])
    logf = jnp.log(jnp.maximum(lb + (1.0 - lb) * sig, MIN_FORGET))
    k_all = (1.0 - lb) * (1.0 - sig)

    g_hi = logf.astype(BF16)
    g_lo = (logf - g_hi.astype(F32)).astype(BF16)
    a2 = jnp.dot(lm_ref[...], jnp.concatenate([g_hi, g_lo], axis=1), preferred_element_type=F32)
    a_all = a2[:, :nh * LANES] + a2[:, nh * LANES:]

    for j in range(nh):
        cols = slice(j * LANES, (j + 1) * LANES)
        q = q_ref[:, cols]
        v = i_ref[:, cols]
        k = k_all[:, cols]
        a = a_all[:, cols]
        b = a[:c]
        b_last = b[c - 1:c, :]

        st = st_ref[j]
        o = lax.dot_general((q * jnp.exp(b)).astype(BF16), st.astype(BF16), NT_DIMS, preferred_element_type=F32)

        scores = pm_ref[0] * lax.dot_general(q.astype(BF16), k.astype(BF16), NT_DIMS, preferred_element_type=F32)
        for lvl in range(HG_LEVELS):
            e = jnp.exp(-jnp.abs(a[(lvl + 1) * c:(lvl + 2) * c]))
            s_l = lax.dot_general((q * e).astype(BF16), (k * e).astype(BF16), NT_DIMS, preferred_element_type=F32)
            scores = scores + pm_ref[lvl + 1] * s_l
        o = o + jnp.dot(scores.astype(BF16), v.astype(BF16), preferred_element_type=F32)

        ks = (k * jnp.exp(b_last - b)).astype(BF16)
        st_ref[j] = st * jnp.exp(b_last) + jnp.dot(v.T.astype(BF16), ks, preferred_element_type=F32)

        o = o * lax.rsqrt(jnp.mean(o * o, -1, keepdims=True) + RMS_EPS) * ng_ref[:, cols]
        gate = g_ref[:, cols]
        o_ref[:, cols] = (o * (gate * jax.nn.sigmoid(gate))).astype(o_ref.dtype)


def _hgrn(proj, lb, norm_g, bsz, seq):
    t = bsz * seq
    nc = seq // HG_CHUNK
    c = HG_CHUNK
    nh = HG_HEADS_PER_STEP
    w = nh * HG_DK
    hsteps = HG_HEADS // nh
    lm, pm = _hgrn_constants()

    def part(p):
        return pl.BlockSpec((c, w), lambda b, h, n: (b * nc + n, p * hsteps + h))

    head = pl.BlockSpec((1, w), lambda b, h, n: (0, h))
    return pl.pallas_call(
        _hgrn_kernel,
        grid=(bsz, hsteps, nc),
        in_specs=[part(0), part(1), part(2), part(3), head, head,
                  pl.BlockSpec(((HG_LEVELS + 1) * c, c), lambda b, h, n: (0, 0)),
                  pl.BlockSpec((HG_LEVELS + 1, c, c), lambda b, h, n: (0, 0, 0))],
        out_specs=pl.BlockSpec((c, w), lambda b, h, n: (b * nc + n, h)),
        out_shape=jax.ShapeDtypeStruct((t, D_MODEL), BF16),
        scratch_shapes=[pltpu.VMEM((nh, HG_DV, HG_DK), F32)],
        compiler_params=_params("arbitrary", "arbitrary", "arbitrary"),
        name="hgrn2",
    )(proj, proj, proj, proj, lb.reshape(1, D_MODEL), norm_g.reshape(1, D_MODEL),
      jnp.asarray(lm, BF16), jnp.asarray(pm, F32))


def _swa_kernel(sink_ref, q_ref, kvp_ref, kvc_ref, bias_ref, o_ref):
    n = pl.program_id(1)
    w = WINDOW
    hd = ATT_HEAD_DIM
    rows = ATT_GROUP * w
    qi = lax.broadcasted_iota(I32, (rows, 2 * w), 0) & (w - 1)
    kj = lax.broadcasted_iota(I32, (rows, 2 * w), 1)
    dist = qi + w - kj
    mask = (dist >= 0) & (dist < w) & ((n > 0) | (kj >= w))
    kvw = ATT_KV_HEADS * hd
    ones = jnp.ones((2 * w, hd), BF16)
    for g in range(ATT_KV_HEADS):
        kwin = jnp.concatenate([kvp_ref[:, g * hd:(g + 1) * hd], kvc_ref[:, g * hd:(g + 1) * hd]], axis=0)
        vwin = jnp.concatenate([kvp_ref[:, kvw + g * hd:kvw + (g + 1) * hd],
                                kvc_ref[:, kvw + g * hd:kvw + (g + 1) * hd]], axis=0)
        heads = range(g * ATT_GROUP, (g + 1) * ATT_GROUP)
        qg = jnp.concatenate([q_ref[:, h * hd:(h + 1) * hd] for h in heads], axis=0) * (hd ** -0.5)
        sink = jnp.concatenate([jnp.full((w, 1), sink_ref[h], F32) for h in heads], axis=0)
        s = lax.dot_general(qg.astype(BF16), kwin, NT_DIMS, preferred_element_type=F32)
        s = s + bias_ref[g * ATT_GROUP:(g + 1) * ATT_GROUP].reshape(rows, 2 * w)
        s = jnp.where(mask, s, NEG_BIG)
        m = jnp.maximum(jnp.max(s, -1, keepdims=True), sink)
        p = jnp.exp(s - m).astype(BF16)
        ov = jnp.dot(p, jnp.concatenate([vwin, ones], axis=1), preferred_element_type=F32)
        denom = ov[:, hd:] + jnp.exp(sink - m)
        og = (ov[:, :hd] / denom).astype(o_ref.dtype)
        o_ref[:, g * ATT_GROUP * hd:(g + 1) * ATT_GROUP * hd] = jnp.concatenate(
            [og[j * w:(j + 1) * w] for j in range(ATT_GROUP)], axis=1)


def _swa(q, kv, bias, sinks, bsz, seq):
    t = bsz * seq
    nb = seq // WINDOW
    kvw2 = 2 * ATT_KV_HEADS * ATT_HEAD_DIM
    return pl.pallas_call(
        _swa_kernel,
        grid=(bsz, nb),
        in_specs=[pl.BlockSpec(memory_space=pltpu.SMEM),
                  pl.BlockSpec((WINDOW, D_MODEL), lambda b, n: (b * nb + n, 0)),
                  pl.BlockSpec((WINDOW, kvw2), lambda b, n: (b * nb + jnp.maximum(n - 1, 0), 0)),
                  pl.BlockSpec((WINDOW, kvw2), lambda b, n: (b * nb + n, 0)),
                  pl.BlockSpec((ATT_Q_HEADS, WINDOW, 2 * WINDOW), lambda b, n: (0, 0, 0))],
        out_specs=pl.BlockSpec((WINDOW, D_MODEL), lambda b, n: (b * nb + n, 0)),
        out_shape=jax.ShapeDtypeStruct((t, D_MODEL), BF16),
        compiler_params=_params("arbitrary", "arbitrary"),
        name="swa",
    )(sinks, q, kv, kv, bias)


def _t5_bucket(dist):
    n = jnp.clip(dist, 0, REL_MAX_DISTANCE - 1)
    max_exact = N_BUCKETS // 2
    large = max_exact + (jnp.log(jnp.maximum(n, max_exact).astype(F32) / max_exact)
                         / math.log(REL_MAX_DISTANCE / max_exact)
                         * (N_BUCKETS - max_exact)).astype(I32)
    large = jnp.minimum(large, N_BUCKETS - 1)
    return jnp.where(n < max_exact, n, large)


def _band_bias(rel_bias):
    qi = jnp.arange(WINDOW)[:, None]
    kj = jnp.arange(2 * WINDOW)[None, :]
    return rel_bias.astype(F32)[_t5_bucket(qi + WINDOW - kj)].transpose(2, 0, 1)


def _moe_kernel(te_ref, nu_ref, src_ref, dst_ref,
                x_hbm, wg_ref, wu_ref, wd_ref, y_hbm,
                xbuf, ybuf, wgb_ref, wub_ref, wdb_ref, in_sem, out_sem):
    i = pl.program_id(0)
    n_used = nu_ref[0]
    rows = MOE_TILE

    def start_gather(tile, slot):
        for r in range(rows):
            pltpu.make_async_copy(x_hbm.at[pl.ds(src_ref[tile * rows + r], 1)],
                                  xbuf.at[slot, pl.ds(r, 1)], in_sem.at[slot]).start()

    def start_scatter(tile):
        for r in range(rows):
            pltpu.make_async_copy(ybuf.at[pl.ds(r, 1)],
                                  y_hbm.at[pl.ds(dst_ref[tile * rows + r], 1)], out_sem.at[0]).start()

    def wait_gather(slot):
        pltpu.make_async_copy(xbuf.at[slot], xbuf.at[slot], in_sem.at[slot]).wait()

    def wait_scatter():
        pltpu.make_async_copy(ybuf, ybuf, out_sem.at[0]).wait()

    @pl.when(i < n_used)
    def _():
        slot = i % 2

        @pl.when(i == 0)
        def _():
            start_gather(0, 0)
            ybuf[...] = jnp.zeros_like(ybuf)
            dump = pltpu.make_async_copy(ybuf, y_hbm.at[pl.ds(y_hbm.shape[0] - rows, rows)], out_sem.at[0])
            dump.start()
            dump.wait()

        @pl.when(i + 1 < n_used)
        def _():
            @pl.when(slot == 0)
            def _():
                start_gather(i + 1, 1)

            @pl.when(slot == 1)
            def _():
                start_gather(i + 1, 0)

        @pl.when((i == 0) | (te_ref[i] != te_ref[jnp.maximum(i - 1, 0)]))
        def _():
            wgb_ref[...] = wg_ref[...].astype(BF16)
            wub_ref[...] = wu_ref[...].astype(BF16)
            wdb_ref[...] = wd_ref[...].astype(BF16)

        wait_gather(slot)
        x = _unpack_bf16_pairs(xbuf[slot]).astype(BF16)
        hg = jnp.dot(x, wgb_ref[...], preferred_element_type=F32)
        hu = jnp.dot(x, wub_ref[...], preferred_element_type=F32)
        hidden = (hg * jax.nn.sigmoid(hg) * hu).astype(BF16)
        y = jnp.dot(hidden, wdb_ref[...], preferred_element_type=F32)

        @pl.when(i > 0)
        def _():
            wait_scatter()

        ybuf[...] = _pack_bf16_pairs(y)
        start_scatter(i)

        @pl.when(i == n_used - 1)
        def _():
            wait_scatter()


def _moe_experts(xp, tile_expert, n_used, slot_src, slot_dst, w_gate, w_up, w_down, layer):
    t = xp.shape[0]
    d = D_MODEL
    n_tiles = tile_expert.shape[0]
    wspec_in = pl.BlockSpec((None, None, d, D_EXPERT), lambda i, te, nu, s, dd: (layer, te[i], 0, 0))
    wspec_out = pl.BlockSpec((None, None, D_EXPERT, d), lambda i, te, nu, s, dd: (layer, te[i], 0, 0))
    return pl.pallas_call(
        _moe_kernel,
        grid_spec=pltpu.PrefetchScalarGridSpec(
            num_scalar_prefetch=4,
            grid=(n_tiles,),
            in_specs=[pl.BlockSpec(memory_space=pl.ANY), wspec_in, wspec_in, wspec_out],
            out_specs=pl.BlockSpec(memory_space=pl.ANY),
            scratch_shapes=[pltpu.VMEM((2, MOE_TILE, D_PACK), U32), pltpu.VMEM((MOE_TILE, D_PACK), U32),
                            pltpu.VMEM((d, D_EXPERT), BF16), pltpu.VMEM((d, D_EXPERT), BF16),
                            pltpu.VMEM((D_EXPERT, d), BF16),
                            pltpu.SemaphoreType.DMA((2,)), pltpu.SemaphoreType.DMA((1,))]),
        out_shape=jax.ShapeDtypeStruct((TOP_K * t + MOE_TILE, D_PACK), U32),
        compiler_params=_params("arbitrary"),
        name="moe_experts",
    )(tile_expert, n_used, slot_src, slot_dst, xp, w_gate, w_up, w_down)


def _hier_moe(xp, route, counts, w_gate, w_up, w_down, layer):
    n_tok = xp.shape[0]
    n_asg = n_tok * TOP_K
    e_flat = route[:, :TOP_K].astype(I32).reshape(-1)
    order = jnp.argsort(e_flat).astype(I32)
    counts = counts[0, :N_EXPERTS].astype(I32)
    starts = jnp.cumsum(counts) - counts
    padded = (counts + MOE_TILE - 1) // MOE_TILE * MOE_TILE
    pad_ends = jnp.cumsum(padded)
    pad_starts = pad_ends - padded
    n_tiles = -(-n_asg // MOE_TILE) + N_EXPERTS
    n_used = pad_ends[-1] // MOE_TILE
    tile_start = jnp.arange(n_tiles, dtype=I32) * MOE_TILE
    tile_expert = jnp.minimum(jnp.sum((pad_ends[None, :] <= tile_start[:, None]).astype(I32), 1), N_EXPERTS - 1)
    tile_expert = jnp.where(tile_start < pad_ends[-1], tile_expert, tile_expert[jnp.maximum(n_used - 1, 0)])

    slot = jnp.arange(n_tiles * MOE_TILE, dtype=I32)
    slot_e = jnp.repeat(tile_expert, MOE_TILE)
    rank = slot - pad_starts[slot_e]
    valid = (rank < counts[slot_e]) & (slot < pad_ends[-1])
    asg = order[jnp.clip(starts[slot_e] + rank, 0, n_asg - 1)]
    slot_src = jnp.where(valid, asg // TOP_K, 0)
    slot_dst = jnp.where(valid, (asg % TOP_K) * n_tok + asg // TOP_K, TOP_K * n_tok + slot % MOE_TILE)
    return _moe_experts(xp, tile_expert, n_used.reshape(1).astype(I32), slot_src, slot_dst,
                        w_gate, w_up, w_down, layer)


def kernel(x, a_w_in, a_lower_bound, a_norm_g, a_w_out, b_w_kv, b_w_q, b_sinks, b_w_out, rel_bias,
           moe_w_rg, moe_b_rg, moe_w_re, moe_b_re, moe_w_gate, moe_w_up, moe_w_down, ln_g, ln_b):
    bsz, seq, d = x.shape
    t = bsz * seq
    lb_sm = jax.nn.softmax(a_lower_bound.astype(F32), axis=0)
    lower_bounds = jnp.cumsum(lb_sm, axis=0) - lb_sm[0]
    att_bias = _band_bias(rel_bias)
    pad = ROUTER_COLS - N_GROUPS - N_EXPERTS
    w_router = jnp.concatenate([moe_w_rg, moe_w_re, jnp.zeros((DEPTH, d, pad), F32)], axis=-1)
    b_router = jnp.concatenate([moe_b_rg, moe_b_re, jnp.zeros((DEPTH, pad), F32)], axis=-1)

    xf = x.reshape(t, d).astype(F32)
    xb = xf.astype(BF16)
    kv = None
    for layer in range(DEPTH):
        if layer < N_A_LAYERS:
            proj = _matmul(xb, a_w_in, layer, F32)
            o = _hgrn(proj, lower_bounds[layer], a_norm_g[layer], bsz, seq)
            h = _matmul(o, a_w_out, layer, F32)
        else:
            j = layer - N_A_LAYERS
            if kv is None:
                kv = _matmul(xb, b_w_kv[None], 0, BF16)
            q = _matmul(xb, b_w_q, j, BF16)
            o = _swa(q, kv, att_bias, b_sinks[j].astype(F32), bsz, seq)
            h = _matmul(o, b_w_out, j, F32)
        xf, xp, route, counts = _ln_router(xf, h, ln_g[2 * layer], ln_b[2 * layer],
                                           w_router[layer], b_router[layer].reshape(1, ROUTER_COLS))
        y_planes = _hier_moe(xp, route, counts, moe_w_gate, moe_w_up, moe_w_down, layer)
        xf, xb = _ln_moe(xf, y_planes, route, ln_g[2 * layer + 1], ln_b[2 * layer + 1])
    return xf.reshape(bsz, seq, d).astype(x.dtype)
```

```python
import functools
import math

import numpy as np
import jax
import jax.numpy as jnp
from jax import lax
from jax.experimental import pallas as pl
from jax.experimental.pallas import tpu as pltpu

F32 = jnp.float32
BF16 = jnp.bfloat16
U32 = jnp.uint32
I32 = jnp.int32

D_MODEL = 2048
DEPTH = 4
N_A_LAYERS = DEPTH // 2
HG_HEADS = 16
HG_DK = 128
HG_DV = 128
ATT_HEAD_DIM = 64
ATT_Q_HEADS = 32
ATT_KV_HEADS = 4
ATT_GROUP = ATT_Q_HEADS // ATT_KV_HEADS
WINDOW = 128
N_BUCKETS = 32
REL_MAX_DISTANCE = 128
N_GROUPS = 4
EXPERTS_PER_GROUP = 8
N_EXPERTS = N_GROUPS * EXPERTS_PER_GROUP
TOP_K = 2
D_EXPERT = D_MODEL // 4
DEEPNORM_ALPHA = (2 * DEPTH) ** 0.25
LN_EPS = 1e-5
RMS_EPS = 1e-6
NEG_BIG = -1e30
MIN_FORGET = 1e-30

LANES = 128
HG_CHUNK = 128
HG_LEVELS = 7
HG_HEADS_PER_STEP = 4
MOE_TILE = 256
ROUTER_COLS = 128
D_PACK = D_MODEL // 2
VMEM_LIMIT = 56 * 1024 * 1024
NT_DIMS = (((1,), (1,)), ((), ()))


def _params(*sem):
    return pltpu.CompilerParams(dimension_semantics=sem, vmem_limit_bytes=VMEM_LIMIT)


def _pack_bf16_pairs(y):
    half = y.shape[1] // 2
    lo = lax.bitcast_convert_type(y[:, :half].astype(BF16).astype(F32), U32)
    hi = lax.bitcast_convert_type(y[:, half:].astype(BF16).astype(F32), U32)
    return (hi & jnp.uint32(0xFFFF0000)) | (lo >> 16)


def _unpack_bf16_pairs(w):
    lo = lax.bitcast_convert_type(w << 16, F32)
    hi = lax.bitcast_convert_type(w & jnp.uint32(0xFFFF0000), F32)
    return jnp.concatenate([lo, hi], axis=1)


def _mm_kernel(x_ref, w_ref, o_ref, wb_ref):
    @pl.when(pl.program_id(1) == 0)
    def _():
        wb_ref[...] = w_ref[...].astype(BF16)

    o_ref[...] = jnp.dot(x_ref[...], wb_ref[...], preferred_element_type=F32).astype(o_ref.dtype)


def _matmul(x, w, layer, out_dtype, tm=1024, tn=1024):
    m, k = x.shape
    n = w.shape[2]
    tm, tn = min(tm, m), min(tn, n)
    return pl.pallas_call(
        _mm_kernel,
        grid=(n // tn, m // tm),
        in_specs=[pl.BlockSpec((tm, k), lambda j, i: (i, 0)),
                  pl.BlockSpec((None, k, tn), lambda j, i: (layer, 0, j))],
        out_specs=pl.BlockSpec((tm, tn), lambda j, i: (i, j)),
        out_shape=jax.ShapeDtypeStruct((m, n), out_dtype),
        scratch_shapes=[pltpu.VMEM((k, tn), BF16)],
        compiler_params=_params("arbitrary", "arbitrary"),
        name="matmul",
    )(x, w)


def _layernorm(v, g, b):
    mu = jnp.mean(v, -1, keepdims=True)
    d = v - mu
    var = jnp.mean(d * d, -1, keepdims=True)
    return d * lax.rsqrt(var + LN_EPS) * g + b


def _route(logits):
    lane = lax.broadcasted_iota(I32, logits.shape, 1)
    big = jnp.int32(ROUTER_COLS)
    is_grp = lane < N_GROUPS
    gl = jnp.where(is_grp, logits, -jnp.inf)
    g_max = jnp.max(gl, -1, keepdims=True)
    grp = jnp.min(jnp.where(is_grp & (gl == g_max), lane, big), -1, keepdims=True)
    p_grp = 1.0 / jnp.sum(jnp.exp(gl - g_max), -1, keepdims=True)
    in_grp = (lane >= N_GROUPS) & (((lane - N_GROUPS) >> 3) == grp)
    el = jnp.where(in_grp, logits, -jnp.inf)
    v1 = jnp.max(el, -1, keepdims=True)
    i1 = jnp.min(jnp.where(in_grp & (el == v1), lane, big), -1, keepdims=True)
    el2 = jnp.where(lane == i1, -jnp.inf, el)
    v2 = jnp.max(el2, -1, keepdims=True)
    i2 = jnp.min(jnp.where(in_grp & (lane != i1) & (el2 == v2), lane, big), -1, keepdims=True)
    ex = jnp.exp(v2 - v1)
    w1 = 1.0 / (1.0 + ex)
    return i1 - N_GROUPS, i2 - N_GROUPS, p_grp * w1, p_grp * (ex * w1)


def _ln_router_kernel(x_ref, h_ref, g_ref, b_ref, wr_ref, br_ref, xo_ref, xp_ref, rt_ref, cnt_ref):
    y = _layernorm(DEEPNORM_ALPHA * x_ref[...] + h_ref[...], g_ref[...], b_ref[...])
    xo_ref[...] = y
    xp_ref[...] = _pack_bf16_pairs(y)
    y_hi = y.astype(BF16)
    y_lo = (y - y_hi.astype(F32)).astype(BF16)
    t_hi = jnp.dot(y_hi, wr_ref[...], preferred_element_type=F32)
    t_lo = jnp.dot(y_lo, wr_ref[:, :ROUTER_COLS], preferred_element_type=F32)
    logits = t_hi[:, :ROUTER_COLS] + t_hi[:, ROUTER_COLS:] + t_lo + br_ref[...]
    e1, e2, g1, g2 = _route(logits)
    lane = lax.broadcasted_iota(I32, logits.shape, 1)
    rt_ref[...] = jnp.where(lane == 0, e1.astype(F32),
                            jnp.where(lane == 1, e2.astype(F32),
                                      jnp.where(lane == 2, g1, jnp.where(lane == 3, g2, 0.0))))

    @pl.when(pl.program_id(0) == 0)
    def _():
        cnt_ref[...] = jnp.zeros_like(cnt_ref)

    hits = (lane == e1).astype(F32) + (lane == e2).astype(F32)
    cnt_ref[...] += jnp.sum(hits, 0, keepdims=True)


def _ln_router(x, h, g, b, wr, br, tm=256):
    t, d = x.shape
    row = lambda i: (i, 0)
    const = lambda i: (0, 0)
    return pl.pallas_call(
        _ln_router_kernel,
        grid=(t // tm,),
        in_specs=[pl.BlockSpec((tm, d), row), pl.BlockSpec((tm, d), row),
                  pl.BlockSpec((1, d), const), pl.BlockSpec((1, d), const),
                  pl.BlockSpec((d, 2 * ROUTER_COLS), const), pl.BlockSpec((1, ROUTER_COLS), const)],
        out_specs=[pl.BlockSpec((tm, d), row), pl.BlockSpec((tm, D_PACK), row),
                   pl.BlockSpec((tm, ROUTER_COLS), row), pl.BlockSpec((1, ROUTER_COLS), const)],
        out_shape=[jax.ShapeDtypeStruct((t, d), F32), jax.ShapeDtypeStruct((t, D_PACK), U32),
                   jax.ShapeDtypeStruct((t, ROUTER_COLS), F32), jax.ShapeDtypeStruct((1, ROUTER_COLS), F32)],
        compiler_params=_params("arbitrary"),
        name="ln_router",
    )(x, h, g.reshape(1, d), b.reshape(1, d), wr, br)


def _ln_moe_kernel(x_ref, y0_ref, y1_ref, rt_ref, g_ref, b_ref, xo_ref, xb_ref):
    rt = rt_ref[...]
    f = rt[:, 2:3] * _unpack_bf16_pairs(y0_ref[...]) + rt[:, 3:4] * _unpack_bf16_pairs(y1_ref[...])
    y = _layernorm(DEEPNORM_ALPHA * x_ref[...] + f, g_ref[...], b_ref[...])
    xo_ref[...] = y
    xb_ref[...] = y.astype(BF16)


def _ln_moe(x, y_planes, route, g, b, tm=256):
    t, d = x.shape
    row = lambda i: (i, 0)
    const = lambda i: (0, 0)
    return pl.pallas_call(
        _ln_moe_kernel,
        grid=(t // tm,),
        in_specs=[pl.BlockSpec((tm, d), row), pl.BlockSpec((tm, D_PACK), row),
                  pl.BlockSpec((tm, D_PACK), lambda i: (i + t // tm, 0)),
                  pl.BlockSpec((tm, ROUTER_COLS), row),
                  pl.BlockSpec((1, d), const), pl.BlockSpec((1, d), const)],
        out_specs=[pl.BlockSpec((tm, d), row), pl.BlockSpec((tm, d), row)],
        out_shape=[jax.ShapeDtypeStruct((t, d), F32), jax.ShapeDtypeStruct((t, d), BF16)],
        compiler_params=_params("arbitrary"),
        name="ln_moe",
    )(x, y_planes, y_planes, route, g.reshape(1, d), b.reshape(1, d))


def _hgrn_constants():
    c = HG_CHUNK
    t = np.arange(c)[:, None]
    s = np.arange(c)[None, :]
    masks = [(s == t).astype(np.float32)]
    for lvl in range(HG_LEVELS):
        m = 1 << lvl
        masks.append(((t // (2 * m) == s // (2 * m)) & ((t // m) % 2 == 1) & ((s // m) % 2 == 0)).astype(np.float32))
    return (s <= t).astype(np.float32), np.stack(masks, axis=0)


def _rows_broadcast(b, first, period, reps):
    n = b.shape[0] // period
    return jnp.concatenate([jnp.broadcast_to(b[first + j * period:first + j * period + 1, :], (reps, b.shape[1]))
                            for j in range(n) for _ in range(period // reps)], axis=0)


def _level_decay(b, fclip, row, lvl):
    if lvl == 0:
        return jnp.where((row & 1) == 1, fclip, 1.0)
    m = 1 << lvl
    if 2 * m < 8:
        ref = jnp.where((row & m * 2) == 0, _rows_broadcast(b, m - 1, 8, 8), _rows_broadcast(b, 3 * m - 1, 8, 8))
    else:
        ref = _rows_broadcast(b, m - 1, 2 * m, 2 * m)
    return jnp.exp(-jnp.abs(b - ref))


def _hgrn_kernel(q_ref, f_ref, i_ref, g_ref, lb_ref, ng_ref, lm_ref, pm_ref, o_ref, st_ref):
    c = HG_CHUNK
    nh = HG_HEADS_PER_STEP

    @pl.when(pl.program_id(2) == 0)
    def _():
        st_ref[...] = jnp.zeros_like(st_ref)

    lb = lb_ref[...]
    sig = jax.nn.sigmoid(f_ref[...])
    fclip_all = jnp.maximum(lb + (1.0 - lb) * sig, MIN_FORGET)
    logf = jnp.log(fclip_all)
    k_all = (1.0 - lb) * (1.0 - sig)

    g_hi = logf.astype(BF16)
    g_lo = (logf - g_hi.astype(F32)).astype(BF16)
    b2 = jnp.dot(lm_ref[...], jnp.concatenate([g_hi, g_lo], axis=1), preferred_element_type=F32)
    b_all = b2[:, :nh * LANES] + b2[:, nh * LANES:]
    row = lax.broadcasted_iota(I32, (c, LANES), 0)

    for j in range(nh):
        cols = slice(j * LANES, (j + 1) * LANES)
        q = q_ref[:, cols]
        v = i_ref[:, cols]
        k = k_all[:, cols]
        fclip = fclip_all[:, cols]
        b = b_all[:, cols]
        b_last = b[c - 1:c, :]

        st = st_ref[j]
        o = lax.dot_general((q * jnp.exp(b)).astype(BF16), st.astype(BF16), NT_DIMS, preferred_element_type=F32)

        scores = pm_ref[0] * lax.dot_general(q.astype(BF16), k.astype(BF16), NT_DIMS, preferred_element_type=F32)
        for lvl in range(HG_LEVELS):
            e = _level_decay(b, fclip, row, lvl)
            s_l = lax.dot_general((q * e).astype(BF16), (k * e).astype(BF16), NT_DIMS, preferred_element_type=F32)
            scores = scores + pm_ref[lvl + 1] * s_l
        o = o + jnp.dot(scores.astype(BF16), v.astype(BF16), preferred_element_type=F32)

        ks = (k * jnp.exp(b_last - b)).astype(BF16)
        st_ref[j] = st * jnp.exp(b_last) + jnp.dot(v.T.astype(BF16), ks, preferred_element_type=F32)

        o = o * lax.rsqrt(jnp.mean(o * o, -1, keepdims=True) + RMS_EPS) * ng_ref[:, cols]
        gate = g_ref[:, cols]
        o_ref[:, cols] = (o * (gate * jax.nn.sigmoid(gate))).astype(o_ref.dtype)


def _hgrn(proj, lb, norm_g, bsz, seq):
    t = bsz * seq
    nc = seq // HG_CHUNK
    c = HG_CHUNK
    nh = HG_HEADS_PER_STEP
    w = nh * HG_DK
    hsteps = HG_HEADS // nh
    lm, pm = _hgrn_constants()

    def part(p):
        return pl.BlockSpec((c, w), lambda b, h, n: (b * nc + n, p * hsteps + h))

    head = pl.BlockSpec((1, w), lambda b, h, n: (0, h))
    return pl.pallas_call(
        _hgrn_kernel,
        grid=(bsz, hsteps, nc),
        in_specs=[part(0), part(1), part(2), part(3), head, head,
                  pl.BlockSpec((c, c), lambda b, h, n: (0, 0)),
                  pl.BlockSpec((HG_LEVELS + 1, c, c), lambda b, h, n: (0, 0, 0))],
        out_specs=pl.BlockSpec((c, w), lambda b, h, n: (b * nc + n, h)),
        out_shape=jax.ShapeDtypeStruct((t, D_MODEL), BF16),
        scratch_shapes=[pltpu.VMEM((nh, HG_DV, HG_DK), F32)],
        compiler_params=_params("arbitrary", "arbitrary", "arbitrary"),
        name="hgrn2",
    )(proj, proj, proj, proj, lb.reshape(1, D_MODEL), norm_g.reshape(1, D_MODEL),
      jnp.asarray(lm, BF16), jnp.asarray(pm, F32))


def _swa_kernel(sink_ref, q_ref, kvp_ref, kvc_ref, bias_ref, o_ref):
    n = pl.program_id(1)
    w = WINDOW
    hd = ATT_HEAD_DIM
    cols = ATT_GROUP * w
    kj = lax.broadcasted_iota(I32, (2 * w, cols), 0)
    qi = lax.broadcasted_iota(I32, (2 * w, cols), 1) & (w - 1)
    dist = qi + w - kj
    mask = (dist >= 0) & (dist < w) & ((n > 0) | (kj >= w))
    kvw = ATT_KV_HEADS * hd
    ones = jnp.ones((2 * w, hd), F32)
    for g in range(ATT_KV_HEADS):
        kwin = jnp.concatenate([kvp_ref[:, g * hd:(g + 1) * hd], kvc_ref[:, g * hd:(g + 1) * hd]], axis=0)
        vwin = jnp.concatenate([kvp_ref[:, kvw + g * hd:kvw + (g + 1) * hd],
                                kvc_ref[:, kvw + g * hd:kvw + (g + 1) * hd]], axis=0)
        heads = range(g * ATT_GROUP, (g + 1) * ATT_GROUP)
        qg = jnp.concatenate([q_ref[:, h * hd:(h + 1) * hd] for h in heads], axis=0) * (hd ** -0.5)
        sink = jnp.concatenate([jnp.full((1, w), sink_ref[h], F32) for h in heads], axis=1)
        s = lax.dot_general(kwin, qg.astype(BF16), NT_DIMS, preferred_element_type=F32) + bias_ref[g]
        s = jnp.where(mask, s, NEG_BIG)
        m = jnp.maximum(jnp.max(s, 0, keepdims=True), sink)
        p = jnp.exp(s - m).astype(BF16)
        vext_t = jnp.concatenate([vwin.astype(F32), ones], axis=1).T.astype(BF16)
        ov = jnp.dot(vext_t, p, preferred_element_type=F32)
        ov = ov / (ov[hd:hd + 1, :] + jnp.exp(sink - m))
        og = ov.T.astype(o_ref.dtype)
        o_ref[:, g * ATT_GROUP * hd:(g + 1) * ATT_GROUP * hd] = jnp.concatenate(
            [og[j * w:(j + 1) * w, :hd] for j in range(ATT_GROUP)], axis=1)


def _swa(q, kv, bias, sinks, bsz, seq):
    t = bsz * seq
    nb = seq // WINDOW
    kvw2 = 2 * ATT_KV_HEADS * ATT_HEAD_DIM
    return pl.pallas_call(
        _swa_kernel,
        grid=(bsz, nb),
        in_specs=[pl.BlockSpec(memory_space=pltpu.SMEM),
                  pl.BlockSpec((WINDOW, D_MODEL), lambda b, n: (b * nb + n, 0)),
                  pl.BlockSpec((WINDOW, kvw2), lambda b, n: (b * nb + jnp.maximum(n - 1, 0), 0)),
                  pl.BlockSpec((WINDOW, kvw2), lambda b, n: (b * nb + n, 0)),
                  pl.BlockSpec((ATT_KV_HEADS, 2 * WINDOW, ATT_GROUP * WINDOW), lambda b, n: (0, 0, 0))],
        out_specs=pl.BlockSpec((WINDOW, D_MODEL), lambda b, n: (b * nb + n, 0)),
        out_shape=jax.ShapeDtypeStruct((t, D_MODEL), BF16),
        compiler_params=_params("arbitrary", "arbitrary"),
        name="swa",
    )(sinks, q, kv, kv, bias)


def _t5_bucket(dist):
    n = jnp.clip(dist, 0, REL_MAX_DISTANCE - 1)
    max_exact = N_BUCKETS // 2
    large = max_exact + (jnp.log(jnp.maximum(n, max_exact).astype(F32) / max_exact)
                         / math.log(REL_MAX_DISTANCE / max_exact)
                         * (N_BUCKETS - max_exact)).astype(I32)
    large = jnp.minimum(large, N_BUCKETS - 1)
    return jnp.where(n < max_exact, n, large)


def _band_bias_t(rel_bias):
    kj = jnp.arange(2 * WINDOW)[:, None]
    qi = jnp.arange(WINDOW)[None, :]
    bucket = _t5_bucket(qi + WINDOW - kj).reshape(-1)
    onehot = (bucket[:, None] == jnp.arange(N_BUCKETS)[None, :]).astype(F32)
    table = jnp.dot(onehot, rel_bias.astype(F32), precision=lax.Precision.HIGHEST)
    table = table.reshape(2 * WINDOW, WINDOW, ATT_KV_HEADS, ATT_GROUP)
    return table.transpose(2, 0, 3, 1).reshape(ATT_KV_HEADS, 2 * WINDOW, ATT_GROUP * WINDOW)


def _moe_kernel(layer, te_ref, nu_ref, nx_ref, src_ref, dst_ref,
                x_hbm, wg_hbm, wu_hbm, wd_hbm, y_hbm,
                xbuf, ybuf, wgs_ref, wus_ref, wds_ref, wgb_ref, wub_ref, wdb_ref, in_sem, out_sem, w_sem):
    i = pl.program_id(0)
    n_used = nu_ref[0]
    rows = MOE_TILE

    def start_gather(tile, slot):
        for r in range(rows):
            pltpu.make_async_copy(x_hbm.at[pl.ds(src_ref[tile * rows + r], 1)],
                                  xbuf.at[slot, pl.ds(r, 1)], in_sem.at[slot]).start()

    def start_scatter(tile):
        for r in range(rows):
            pltpu.make_async_copy(ybuf.at[pl.ds(r, 1)],
                                  y_hbm.at[pl.ds(dst_ref[tile * rows + r], 1)], out_sem.at[0]).start()

    def wait_gather(slot):
        pltpu.make_async_copy(xbuf.at[slot], xbuf.at[slot], in_sem.at[slot]).wait()

    def wait_scatter():
        pltpu.make_async_copy(ybuf, ybuf, out_sem.at[0]).wait()

    def weight_copies(e):
        return (pltpu.make_async_copy(wg_hbm.at[layer, e], wgs_ref, w_sem.at[0]),
                pltpu.make_async_copy(wu_hbm.at[layer, e], wus_ref, w_sem.at[1]),
                pltpu.make_async_copy(wd_hbm.at[layer, e], wds_ref, w_sem.at[2]))

    @pl.when(i < n_used)
    def _():
        slot = i % 2

        @pl.when(i == 0)
        def _():
            for cp in weight_copies(te_ref[0]):
                cp.start()
            start_gather(0, 0)
            ybuf[...] = jnp.zeros_like(ybuf)
            dump = pltpu.make_async_copy(ybuf, y_hbm.at[pl.ds(y_hbm.shape[0] - rows, rows)], out_sem.at[0])
            dump.start()
            dump.wait()

        @pl.when(i + 1 < n_used)
        def _():
            @pl.when(slot == 0)
            def _():
                start_gather(i + 1, 1)

            @pl.when(slot == 1)
            def _():
                start_gather(i + 1, 0)

        @pl.when((i == 0) | (te_ref[i] != te_ref[jnp.maximum(i - 1, 0)]))
        def _():
            for cp in weight_copies(te_ref[i]):
                cp.wait()
            wgb_ref[...] = wgs_ref[...].astype(BF16)
            wub_ref[...] = wus_ref[...].astype(BF16)
            wdb_ref[...] = wds_ref[...].astype(BF16)

            @pl.when(nx_ref[i] >= 0)
            def _():
                for cp in weight_copies(nx_ref[i]):
                    cp.start()

        wait_gather(slot)
        x = _unpack_bf16_pairs(xbuf[slot]).astype(BF16)
        hg = jnp.dot(x, wgb_ref[...], preferred_element_type=F32)
        hu = jnp.dot(x, wub_ref[...], preferred_element_type=F32)
        hidden = (hg * jax.nn.sigmoid(hg) * hu).astype(BF16)
        y = jnp.dot(hidden, wdb_ref[...], preferred_element_type=F32)

        @pl.when(i > 0)
        def _():
            wait_scatter()

        ybuf[...] = _pack_bf16_pairs(y)
        start_scatter(i)

        @pl.when(i == n_used - 1)
        def _():
            wait_scatter()


def _moe_experts(xp, tile_expert, n_used, next_expert, slot_src, slot_dst, w_gate, w_up, w_down, layer):
    t = xp.shape[0]
    d = D_MODEL
    n_tiles = tile_expert.shape[0]
    any_spec = pl.BlockSpec(memory_space=pl.ANY)
    return pl.pallas_call(
        functools.partial(_moe_kernel, layer),
        grid_spec=pltpu.PrefetchScalarGridSpec(
            num_scalar_prefetch=5,
            grid=(n_tiles,),
            in_specs=[any_spec, any_spec, any_spec, any_spec],
            out_specs=any_spec,
            scratch_shapes=[pltpu.VMEM((2, MOE_TILE, D_PACK), U32), pltpu.VMEM((MOE_TILE, D_PACK), U32),
                            pltpu.VMEM((d, D_EXPERT), F32), pltpu.VMEM((d, D_EXPERT), F32),
                            pltpu.VMEM((D_EXPERT, d), F32),
                            pltpu.VMEM((d, D_EXPERT), BF16), pltpu.VMEM((d, D_EXPERT), BF16),
                            pltpu.VMEM((D_EXPERT, d), BF16),
                            pltpu.SemaphoreType.DMA((2,)), pltpu.SemaphoreType.DMA((1,)),
                            pltpu.SemaphoreType.DMA((3,))]),
        out_shape=jax.ShapeDtypeStruct((TOP_K * t + MOE_TILE, D_PACK), U32),
        compiler_params=_params("arbitrary"),
        name="moe_experts",
    )(tile_expert, n_used, next_expert, slot_src, slot_dst, xp, w_gate, w_up, w_down)


def _hier_moe(xp, route, counts, w_gate, w_up, w_down, layer):
    n_tok = xp.shape[0]
    n_asg = n_tok * TOP_K
    e_flat = route[:, :TOP_K].astype(I32).reshape(-1)
    order = jnp.argsort(e_flat).astype(I32)
    counts = counts[0, :N_EXPERTS].astype(I32)
    starts = jnp.cumsum(counts) - counts
    padded = (counts + MOE_TILE - 1) // MOE_TILE * MOE_TILE
    pad_ends = jnp.cumsum(padded)
    pad_starts = pad_ends - padded
    n_tiles = -(-n_asg // MOE_TILE) + N_EXPERTS
    n_used = pad_ends[-1] // MOE_TILE

    tile_start = jnp.arange(n_tiles, dtype=I32) * MOE_TILE
    used = tile_start < pad_ends[-1]
    tile_expert = jnp.minimum(jnp.sum((pad_ends[None, :] <= tile_start[:, None]).astype(I32), 1), N_EXPERTS - 1)
    next_tile = pad_ends[tile_expert] // MOE_TILE
    next_expert = jnp.where(used & (next_tile < n_used), tile_expert[jnp.minimum(next_tile, n_tiles - 1)], -1)
    rank0 = tile_start - pad_starts[tile_expert]
    n_valid = jnp.where(used, jnp.clip(counts[tile_expert] - rank0, 0, MOE_TILE), 0)
    sorted0 = starts[tile_expert] + rank0

    r = jnp.arange(MOE_TILE, dtype=I32)[None, :]
    valid = r < n_valid[:, None]
    asg = order[jnp.clip(sorted0[:, None] + r, 0, n_asg - 1)]
    slot_src = jnp.where(valid, asg // TOP_K, 0).reshape(-1)
    slot_dst = jnp.where(valid, (asg % TOP_K) * n_tok + asg // TOP_K, TOP_K * n_tok + r).reshape(-1)
    return _moe_experts(xp, tile_expert, n_used.reshape(1).astype(I32), next_expert.astype(I32),
                        slot_src, slot_dst, w_gate, w_up, w_down, layer)


def kernel(x, a_w_in, a_lower_bound, a_norm_g, a_w_out, b_w_kv, b_w_q, b_sinks, b_w_out, rel_bias,
           moe_w_rg, moe_b_rg, moe_w_re, moe_b_re, moe_w_gate, moe_w_up, moe_w_down, ln_g, ln_b):
    bsz, seq, d = x.shape
    t = bsz * seq
    lb_sm = jax.nn.softmax(a_lower_bound.astype(F32), axis=0)
    lower_bounds = jnp.cumsum(lb_sm, axis=0) - lb_sm[0]
    att_bias = _band_bias_t(rel_bias)
    pad = ROUTER_COLS - N_GROUPS - N_EXPERTS
    w_router = jnp.concatenate([moe_w_rg, moe_w_re, jnp.zeros((DEPTH, d, pad), F32)], axis=-1)
    w_router_hi = w_router.astype(BF16)
    w_router_lo = (w_router - w_router_hi.astype(F32)).astype(BF16)
    w_router = jnp.concatenate([w_router_hi, w_router_lo], axis=-1)
    b_router = jnp.concatenate([moe_b_rg, moe_b_re, jnp.zeros((DEPTH, pad), F32)], axis=-1)

    xf = x.reshape(t, d).astype(F32)
    xb = xf.astype(BF16)
    kv = None
    for layer in range(DEPTH):
        if layer < N_A_LAYERS:
            proj = _matmul(xb, a_w_in, layer, F32)
            o = _hgrn(proj, lower_bounds[layer], a_norm_g[layer], bsz, seq)
            h = _matmul(o, a_w_out, layer, F32)
        else:
            j = layer - N_A_LAYERS
            if kv is None:
                kv = _matmul(xb, b_w_kv[None], 0, BF16)
            q = _matmul(xb, b_w_q, j, BF16)
            o = _swa(q, kv, att_bias, b_sinks[j].astype(F32), bsz, seq)
            h = _matmul(o, b_w_out, j, F32)
        xf, xp, route, counts = _ln_router(xf, h, ln_g[2 * layer], ln_b[2 * layer],
                                           w_router[layer], b_router[layer].reshape(1, ROUTER_COLS))
        y_planes = _hier_moe(xp, route, counts, moe_w_gate, moe_w_up, moe_w_down, layer)
        xf, xb = _ln_moe(xf, y_planes, route, ln_g[2 * layer + 1], ln_b[2 * layer + 1])
    return xf.reshape(bsz, seq, d).astype(x.dtype)
```

```python
import functools
import math

import numpy as np
import jax
import jax.numpy as jnp
from jax import lax
from jax.experimental import pallas as pl
from jax.experimental.pallas import tpu as pltpu

F32 = jnp.float32
BF16 = jnp.bfloat16
U32 = jnp.uint32
I32 = jnp.int32

D_MODEL = 2048
DEPTH = 4
N_A_LAYERS = DEPTH // 2
HG_HEADS = 16
HG_DK = 128
HG_DV = 128
ATT_HEAD_DIM = 64
ATT_Q_HEADS = 32
ATT_KV_HEADS = 4
ATT_GROUP = ATT_Q_HEADS // ATT_KV_HEADS
WINDOW = 128
N_BUCKETS = 32
REL_MAX_DISTANCE = 128
N_GROUPS = 4
EXPERTS_PER_GROUP = 8
N_EXPERTS = N_GROUPS * EXPERTS_PER_GROUP
TOP_K = 2
D_EXPERT = D_MODEL // 4
DEEPNORM_ALPHA = (2 * DEPTH) ** 0.25
LN_EPS = 1e-5
RMS_EPS = 1e-6
NEG_BIG = -1e30
MIN_FORGET = 1e-30

LANES = 128
HG_CHUNK = 128
HG_LEVELS = 7
HG_HEADS_PER_STEP = 4
MOE_TILE = 256
ROUTER_COLS = 128
SUBLANES = 8
SLAB = (SUBLANES, LANES)
assert D_MODEL // 2 == SUBLANES * LANES
VMEM_LIMIT = 56 * 1024 * 1024
NT_DIMS = (((1,), (1,)), ((), ()))


def _params(*sem):
    return pltpu.CompilerParams(dimension_semantics=sem, vmem_limit_bytes=VMEM_LIMIT)


def _pack_bf16_pairs(y):
    half = y.shape[1] // 2
    lo = lax.bitcast_convert_type(y[:, :half].astype(BF16).astype(F32), U32)
    hi = lax.bitcast_convert_type(y[:, half:].astype(BF16).astype(F32), U32)
    w = (hi & jnp.uint32(0xFFFF0000)) | (lo >> 16)
    return jnp.swapaxes(jnp.stack([w[:, s * LANES:(s + 1) * LANES] for s in range(SUBLANES)], axis=0), 0, 1)


def _unpack_bf16_pairs(slabs):
    cols = jnp.swapaxes(slabs, 0, 1)
    w = jnp.concatenate([cols[s] for s in range(SUBLANES)], axis=1)
    lo = lax.bitcast_convert_type(w << 16, F32)
    hi = lax.bitcast_convert_type(w & jnp.uint32(0xFFFF0000), F32)
    return jnp.concatenate([lo, hi], axis=1)


def _mm_kernel(x_ref, w_ref, o_ref, wb_ref):
    @pl.when(pl.program_id(1) == 0)
    def _():
        wb_ref[...] = w_ref[...].astype(BF16)

    o_ref[...] = jnp.dot(x_ref[...], wb_ref[...], preferred_element_type=F32).astype(o_ref.dtype)


def _matmul(x, w, layer, out_dtype, tm=1024, tn=1024):
    m, k = x.shape
    n = w.shape[2]
    tm, tn = min(tm, m), min(tn, n)
    return pl.pallas_call(
        _mm_kernel,
        grid=(n // tn, m // tm),
        in_specs=[pl.BlockSpec((tm, k), lambda j, i: (i, 0)),
                  pl.BlockSpec((None, k, tn), lambda j, i: (layer, 0, j))],
        out_specs=pl.BlockSpec((tm, tn), lambda j, i: (i, j)),
        out_shape=jax.ShapeDtypeStruct((m, n), out_dtype),
        scratch_shapes=[pltpu.VMEM((k, tn), BF16)],
        compiler_params=_params("arbitrary", "arbitrary"),
        name="matmul",
    )(x, w)


def _layernorm(v, g, b):
    mu = jnp.mean(v, -1, keepdims=True)
    d = v - mu
    var = jnp.mean(d * d, -1, keepdims=True)
    return d * lax.rsqrt(var + LN_EPS) * g + b


def _route(logits):
    lane = lax.broadcasted_iota(I32, logits.shape, 1)
    big = jnp.int32(ROUTER_COLS)
    is_grp = lane < N_GROUPS
    gl = jnp.where(is_grp, logits, -jnp.inf)
    g_max = jnp.max(gl, -1, keepdims=True)
    grp = jnp.min(jnp.where(is_grp & (gl == g_max), lane, big), -1, keepdims=True)
    p_grp = 1.0 / jnp.sum(jnp.exp(gl - g_max), -1, keepdims=True)
    in_grp = (lane >= N_GROUPS) & (((lane - N_GROUPS) >> 3) == grp)
    el = jnp.where(in_grp, logits, -jnp.inf)
    v1 = jnp.max(el, -1, keepdims=True)
    i1 = jnp.min(jnp.where(in_grp & (el == v1), lane, big), -1, keepdims=True)
    el2 = jnp.where(lane == i1, -jnp.inf, el)
    v2 = jnp.max(el2, -1, keepdims=True)
    i2 = jnp.min(jnp.where(in_grp & (lane != i1) & (el2 == v2), lane, big), -1, keepdims=True)
    ex = jnp.exp(v2 - v1)
    w1 = 1.0 / (1.0 + ex)
    return i1 - N_GROUPS, i2 - N_GROUPS, p_grp * w1, p_grp * (ex * w1)


def _ln_router_kernel(x_ref, h_ref, g_ref, b_ref, wr_ref, br_ref, xo_ref, xp_ref, rt_ref, cnt_ref):
    y = _layernorm(DEEPNORM_ALPHA * x_ref[...] + h_ref[...], g_ref[...], b_ref[...])
    xo_ref[...] = y
    xp_ref[...] = _pack_bf16_pairs(y)
    y_hi = y.astype(BF16)
    y_lo = (y - y_hi.astype(F32)).astype(BF16)
    t_hi = jnp.dot(y_hi, wr_ref[...], preferred_element_type=F32)
    t_lo = jnp.dot(y_lo, wr_ref[:, :ROUTER_COLS], preferred_element_type=F32)
    logits = t_hi[:, :ROUTER_COLS] + t_hi[:, ROUTER_COLS:] + t_lo + br_ref[...]
    e1, e2, g1, g2 = _route(logits)
    lane = lax.broadcasted_iota(I32, logits.shape, 1)
    rt_ref[...] = jnp.where(lane == 0, e1.astype(F32),
                            jnp.where(lane == 1, e2.astype(F32),
                                      jnp.where(lane == 2, g1, jnp.where(lane == 3, g2, 0.0))))

    @pl.when(pl.program_id(0) == 0)
    def _():
        cnt_ref[...] = jnp.zeros_like(cnt_ref)

    hits = (lane == e1).astype(F32) + (lane == e2).astype(F32)
    cnt_ref[...] += jnp.sum(hits, 0, keepdims=True)


def _ln_router(x, h, g, b, wr, br, tm=256):
    t, d = x.shape
    row = lambda i: (i, 0)
    const = lambda i: (0, 0)
    return pl.pallas_call(
        _ln_router_kernel,
        grid=(t // tm,),
        in_specs=[pl.BlockSpec((tm, d), row), pl.BlockSpec((tm, d), row),
                  pl.BlockSpec((1, d), const), pl.BlockSpec((1, d), const),
                  pl.BlockSpec((d, 2 * ROUTER_COLS), const), pl.BlockSpec((1, ROUTER_COLS), const)],
        out_specs=[pl.BlockSpec((tm, d), row), pl.BlockSpec((tm,) + SLAB, lambda i: (i, 0, 0)),
                   pl.BlockSpec((tm, ROUTER_COLS), row), pl.BlockSpec((1, ROUTER_COLS), const)],
        out_shape=[jax.ShapeDtypeStruct((t, d), F32), jax.ShapeDtypeStruct((t,) + SLAB, U32),
                   jax.ShapeDtypeStruct((t, ROUTER_COLS), F32), jax.ShapeDtypeStruct((1, ROUTER_COLS), F32)],
        compiler_params=_params("arbitrary"),
        name="ln_router",
    )(x, h, g.reshape(1, d), b.reshape(1, d), wr, br)


def _ln_moe_kernel(x_ref, y0_ref, y1_ref, rt_ref, g_ref, b_ref, xo_ref, xb_ref):
    rt = rt_ref[...]
    f = rt[:, 2:3] * _unpack_bf16_pairs(y0_ref[...]) + rt[:, 3:4] * _unpack_bf16_pairs(y1_ref[...])
    y = _layernorm(DEEPNORM_ALPHA * x_ref[...] + f, g_ref[...], b_ref[...])
    xo_ref[...] = y
    xb_ref[...] = y.astype(BF16)


def _ln_moe(x, y_planes, route, g, b, tm=256):
    t, d = x.shape
    row = lambda i: (i, 0)
    const = lambda i: (0, 0)
    return pl.pallas_call(
        _ln_moe_kernel,
        grid=(t // tm,),
        in_specs=[pl.BlockSpec((tm, d), row), pl.BlockSpec((tm,) + SLAB, lambda i: (i, 0, 0)),
                  pl.BlockSpec((tm,) + SLAB, lambda i: (i + t // tm, 0, 0)),
                  pl.BlockSpec((tm, ROUTER_COLS), row),
                  pl.BlockSpec((1, d), const), pl.BlockSpec((1, d), const)],
        out_specs=[pl.BlockSpec((tm, d), row), pl.BlockSpec((tm, d), row)],
        out_shape=[jax.ShapeDtypeStruct((t, d), F32), jax.ShapeDtypeStruct((t, d), BF16)],
        compiler_params=_params("arbitrary"),
        name="ln_moe",
    )(x, y_planes, y_planes, route, g.reshape(1, d), b.reshape(1, d))


def _hgrn_constants():
    c = HG_CHUNK
    t = np.arange(c)[:, None]
    s = np.arange(c)[None, :]
    masks = [(s == t).astype(np.float32)]
    for lvl in range(HG_LEVELS):
        m = 1 << lvl
        masks.append(((t // (2 * m) == s // (2 * m)) & ((t // m) % 2 == 1) & ((s // m) % 2 == 0)).astype(np.float32))
    return (s <= t).astype(np.float32), np.stack(masks, axis=0)


def _rows_broadcast(b, first, period, reps):
    n = b.shape[0] // period
    return jnp.concatenate([jnp.broadcast_to(b[first + j * period:first + j * period + 1, :], (reps, b.shape[1]))
                            for j in range(n) for _ in range(period // reps)], axis=0)


def _level_decay(b, fclip, row, lvl):
    if lvl == 0:
        return jnp.where((row & 1) == 1, fclip, 1.0)
    m = 1 << lvl
    if 2 * m < 8:
        ref = jnp.where((row & m * 2) == 0, _rows_broadcast(b, m - 1, 8, 8), _rows_broadcast(b, 3 * m - 1, 8, 8))
    else:
        ref = _rows_broadcast(b, m - 1, 2 * m, 2 * m)
    return jnp.exp(-jnp.abs(b - ref))


class _RowBlocks:
    def __init__(self, n):
        self.tiles = [None] * (n // SUBLANES)

    def add(self, first_row, val):
        for u in range(val.shape[0] // SUBLANES):
            piece = val[u * SUBLANES:(u + 1) * SUBLANES]
            i = first_row // SUBLANES + u
            self.tiles[i] = piece if self.tiles[i] is None else self.tiles[i] + piece

    def value(self):
        return jnp.concatenate(self.tiles, axis=0)


def _hgrn_kernel(q_ref, f_ref, i_ref, g_ref, lb_ref, ng_ref, lm_ref, pm_ref, o_ref, st_ref):
    c = HG_CHUNK
    nh = HG_HEADS_PER_STEP

    @pl.when(pl.program_id(2) == 0)
    def _():
        st_ref[...] = jnp.zeros_like(st_ref)

    lb = lb_ref[...]
    sig = jax.nn.sigmoid(f_ref[...])
    fclip_all = jnp.maximum(lb + (1.0 - lb) * sig, MIN_FORGET)
    logf = jnp.log(fclip_all)
    k_all = (1.0 - lb) * (1.0 - sig)

    g_hi = logf.astype(BF16)
    g_lo = (logf - g_hi.astype(F32)).astype(BF16)
    b2 = jnp.dot(lm_ref[...], jnp.concatenate([g_hi, g_lo], axis=1), preferred_element_type=F32)
    b_all = b2[:, :nh * LANES] + b2[:, nh * LANES:]
    row = lax.broadcasted_iota(I32, (c, LANES), 0)

    for j in range(nh):
        cols = slice(j * LANES, (j + 1) * LANES)
        q = q_ref[:, cols]
        v = i_ref[:, cols]
        k = k_all[:, cols]
        fclip = fclip_all[:, cols]
        b = b_all[:, cols]
        b_last = b[c - 1:c, :]

        st = st_ref[j]
        o = lax.dot_general((q * jnp.exp(b)).astype(BF16), st.astype(BF16), NT_DIMS, preferred_element_type=F32)

        scores = _RowBlocks(c)
        scores.add(0, pm_ref[0] * lax.dot_general(q.astype(BF16), k.astype(BF16), NT_DIMS, preferred_element_type=F32))
        for lvl in range(HG_LEVELS):
            e = _level_decay(b, fclip, row, lvl)
            m = 1 << lvl
            if m < SUBLANES:
                s_l = lax.dot_general((q * e).astype(BF16), (k * e).astype(BF16), NT_DIMS, preferred_element_type=F32)
                scores.add(0, pm_ref[lvl + 1] * s_l)
            else:
                up = [slice(u * 2 * m + m, (u + 1) * 2 * m) for u in range(c // (2 * m))]
                lo = [slice(u * 2 * m, u * 2 * m + m) for u in range(c // (2 * m))]
                qe = jnp.concatenate([q[r] * e[r] for r in up], axis=0).astype(BF16)
                ke = jnp.concatenate([piece for r in lo for piece in (k[r] * e[r], jnp.zeros((m, LANES), F32))],
                                     axis=0).astype(BF16)
                s_l = lax.dot_general(qe, ke, NT_DIMS, preferred_element_type=F32)
                for u, r in enumerate(up):
                    scores.add(r.start, pm_ref[lvl + 1, r, :] * s_l[u * m:(u + 1) * m])
        o = o + jnp.dot(scores.value().astype(BF16), v.astype(BF16), preferred_element_type=F32)

        ks = (k * jnp.exp(b_last - b)).astype(BF16)
        st_ref[j] = st * jnp.exp(b_last) + jnp.dot(v.T.astype(BF16), ks, preferred_element_type=F32)

        o = o * lax.rsqrt(jnp.mean(o * o, -1, keepdims=True) + RMS_EPS) * ng_ref[:, cols]
        gate = g_ref[:, cols]
        o_ref[:, cols] = (o * (gate * jax.nn.sigmoid(gate))).astype(o_ref.dtype)


def _hgrn(proj, lb, norm_g, bsz, seq):
    t = bsz * seq
    nc = seq // HG_CHUNK
    c = HG_CHUNK
    nh = HG_HEADS_PER_STEP
    w = nh * HG_DK
    hsteps = HG_HEADS // nh
    lm, pm = _hgrn_constants()

    def part(p):
        return pl.BlockSpec((c, w), lambda b, h, n: (b * nc + n, p * hsteps + h))

    head = pl.BlockSpec((1, w), lambda b, h, n: (0, h))
    return pl.pallas_call(
        _hgrn_kernel,
        grid=(bsz, hsteps, nc),
        in_specs=[part(0), part(1), part(2), part(3), head, head,
                  pl.BlockSpec((c, c), lambda b, h, n: (0, 0)),
                  pl.BlockSpec((HG_LEVELS + 1, c, c), lambda b, h, n: (0, 0, 0))],
        out_specs=pl.BlockSpec((c, w), lambda b, h, n: (b * nc + n, h)),
        out_shape=jax.ShapeDtypeStruct((t, D_MODEL), BF16),
        scratch_shapes=[pltpu.VMEM((nh, HG_DV, HG_DK), F32)],
        compiler_params=_params("arbitrary", "arbitrary", "arbitrary"),
        name="hgrn2",
    )(proj, proj, proj, proj, lb.reshape(1, D_MODEL), norm_g.reshape(1, D_MODEL),
      jnp.asarray(lm, BF16), jnp.asarray(pm, F32))


def _swa_kernel(sink_ref, q_ref, kvp_ref, kvc_ref, bias_ref, o_ref):
    n = pl.program_id(1)
    w = WINDOW
    hd = ATT_HEAD_DIM
    cols = ATT_GROUP * w
    kj = lax.broadcasted_iota(I32, (2 * w, cols), 0)
    qi = lax.broadcasted_iota(I32, (2 * w, cols), 1) & (w - 1)
    dist = qi + w - kj
    mask = (dist >= 0) & (dist < w) & ((n > 0) | (kj >= w))
    kvw = ATT_KV_HEADS * hd
    ones = jnp.ones((2 * w, hd), F32)
    for g in range(ATT_KV_HEADS):
        kwin = jnp.concatenate([kvp_ref[:, g * hd:(g + 1) * hd], kvc_ref[:, g * hd:(g + 1) * hd]], axis=0)
        vwin = jnp.concatenate([kvp_ref[:, kvw + g * hd:kvw + (g + 1) * hd],
                                kvc_ref[:, kvw + g * hd:kvw + (g + 1) * hd]], axis=0)
        heads = range(g * ATT_GROUP, (g + 1) * ATT_GROUP)
        qg = jnp.concatenate([q_ref[:, h * hd:(h + 1) * hd] for h in heads], axis=0) * (hd ** -0.5)
        sink = jnp.concatenate([jnp.full((1, w), sink_ref[h], F32) for h in heads], axis=1)
        s = lax.dot_general(kwin, qg.astype(BF16), NT_DIMS, preferred_element_type=F32) + bias_ref[g]
        s = jnp.where(mask, s, NEG_BIG)
        m = jnp.maximum(jnp.max(s, 0, keepdims=True), sink)
        p = jnp.exp(s - m).astype(BF16)
        vext_t = jnp.concatenate([vwin.astype(F32), ones], axis=1).T.astype(BF16)
        ov = jnp.dot(vext_t, p, preferred_element_type=F32)
        ov = ov / (ov[hd:hd + 1, :] + jnp.exp(sink - m))
        og = ov.T.astype(o_ref.dtype)
        o_ref[:, g * ATT_GROUP * hd:(g + 1) * ATT_GROUP * hd] = jnp.concatenate(
            [og[j * w:(j + 1) * w, :hd] for j in range(ATT_GROUP)], axis=1)


def _swa(q, kv, bias, sinks, bsz, seq):
    t = bsz * seq
    nb = seq // WINDOW
    kvw2 = 2 * ATT_KV_HEADS * ATT_HEAD_DIM
    return pl.pallas_call(
        _swa_kernel,
        grid=(bsz, nb),
        in_specs=[pl.BlockSpec(memory_space=pltpu.SMEM),
                  pl.BlockSpec((WINDOW, D_MODEL), lambda b, n: (b * nb + n, 0)),
                  pl.BlockSpec((WINDOW, kvw2), lambda b, n: (b * nb + jnp.maximum(n - 1, 0), 0)),
                  pl.BlockSpec((WINDOW, kvw2), lambda b, n: (b * nb + n, 0)),
                  pl.BlockSpec((ATT_KV_HEADS, 2 * WINDOW, ATT_GROUP * WINDOW), lambda b, n: (0, 0, 0))],
        out_specs=pl.BlockSpec((WINDOW, D_MODEL), lambda b, n: (b * nb + n, 0)),
        out_shape=jax.ShapeDtypeStruct((t, D_MODEL), BF16),
        compiler_params=_params("arbitrary", "arbitrary"),
        name="swa",
    )(sinks, q, kv, kv, bias)


def _t5_bucket(dist):
    n = jnp.clip(dist, 0, REL_MAX_DISTANCE - 1)
    max_exact = N_BUCKETS // 2
    large = max_exact + (jnp.log(jnp.maximum(n, max_exact).astype(F32) / max_exact)
                         / math.log(REL_MAX_DISTANCE / max_exact)
                         * (N_BUCKETS - max_exact)).astype(I32)
    large = jnp.minimum(large, N_BUCKETS - 1)
    return jnp.where(n < max_exact, n, large)


def _band_bias_t(rel_bias):
    kj = jnp.arange(2 * WINDOW)[:, None]
    qi = jnp.arange(WINDOW)[None, :]
    bucket = _t5_bucket(qi + WINDOW - kj).reshape(-1)
    onehot = (bucket[:, None] == jnp.arange(N_BUCKETS)[None, :]).astype(F32)
    table = jnp.dot(onehot, rel_bias.astype(F32), precision=lax.Precision.HIGHEST)
    table = table.reshape(2 * WINDOW, WINDOW, ATT_KV_HEADS, ATT_GROUP)
    return table.transpose(2, 0, 3, 1).reshape(ATT_KV_HEADS, 2 * WINDOW, ATT_GROUP * WINDOW)


def _moe_kernel(layer, te_ref, nu_ref, nx_ref, src_ref, dst_ref,
                x_hbm, wg_hbm, wu_hbm, wd_hbm, y_hbm,
                xbuf, ybuf, wgs_ref, wus_ref, wds_ref, wgb_ref, wub_ref, wdb_ref, in_sem, out_sem, w_sem):
    i = pl.program_id(0)
    n_used = nu_ref[0]
    rows = MOE_TILE

    def start_gather(tile, slot):
        for r in range(rows):
            pltpu.make_async_copy(x_hbm.at[src_ref[tile * rows + r]], xbuf.at[slot, r], in_sem.at[slot]).start()

    def start_scatter(tile):
        for r in range(rows):
            pltpu.make_async_copy(ybuf.at[r], y_hbm.at[dst_ref[tile * rows + r]], out_sem.at[0]).start()

    def wait_gather(slot):
        pltpu.make_async_copy(xbuf.at[slot], xbuf.at[slot], in_sem.at[slot]).wait()

    def wait_scatter():
        pltpu.make_async_copy(ybuf, ybuf, out_sem.at[0]).wait()

    def weight_copies(e):
        return (pltpu.make_async_copy(wg_hbm.at[layer, e], wgs_ref, w_sem.at[0]),
                pltpu.make_async_copy(wu_hbm.at[layer, e], wus_ref, w_sem.at[1]),
                pltpu.make_async_copy(wd_hbm.at[layer, e], wds_ref, w_sem.at[2]))

    @pl.when(i < n_used)
    def _():
        slot = i % 2

        @pl.when(i == 0)
        def _():
            for cp in weight_copies(te_ref[0]):
                cp.start()
            start_gather(0, 0)
            ybuf[...] = jnp.zeros_like(ybuf)
            dump = pltpu.make_async_copy(ybuf, y_hbm.at[pl.ds(y_hbm.shape[0] - rows, rows)], out_sem.at[0])
            dump.start()
            dump.wait()

        @pl.when(i + 1 < n_used)
        def _():
            @pl.when(slot == 0)
            def _():
                start_gather(i + 1, 1)

            @pl.when(slot == 1)
            def _():
                start_gather(i + 1, 0)

        @pl.when((i == 0) | (te_ref[i] != te_ref[jnp.maximum(i - 1, 0)]))
        def _():
            for cp in weight_copies(te_ref[i]):
                cp.wait()
            wgb_ref[...] = wgs_ref[...].astype(BF16)
            wub_ref[...] = wus_ref[...].astype(BF16)
            wdb_ref[...] = wds_ref[...].astype(BF16)

            @pl.when(nx_ref[i] >= 0)
            def _():
                for cp in weight_copies(nx_ref[i]):
                    cp.start()

        wait_gather(slot)
        x = _unpack_bf16_pairs(xbuf[slot]).astype(BF16)
        hg = jnp.dot(x, wgb_ref[...], preferred_element_type=F32)
        hu = jnp.dot(x, wub_ref[...], preferred_element_type=F32)
        hidden = (hg * jax.nn.sigmoid(hg) * hu).astype(BF16)
        y = jnp.dot(hidden, wdb_ref[...], preferred_element_type=F32)

        @pl.when(i > 0)
        def _():
            wait_scatter()

        ybuf[...] = _pack_bf16_pairs(y)
        start_scatter(i)

        @pl.when(i == n_used - 1)
        def _():
            wait_scatter()


def _moe_experts(xp, tile_expert, n_used, next_expert, slot_src, slot_dst, w_gate, w_up, w_down, layer):
    t = xp.shape[0]
    d = D_MODEL
    n_tiles = tile_expert.shape[0]
    any_spec = pl.BlockSpec(memory_space=pl.ANY)
    return pl.pallas_call(
        functools.partial(_moe_kernel, layer),
        grid_spec=pltpu.PrefetchScalarGridSpec(
            num_scalar_prefetch=5,
            grid=(n_tiles,),
            in_specs=[any_spec, any_spec, any_spec, any_spec],
            out_specs=any_spec,
            scratch_shapes=[pltpu.VMEM((2, MOE_TILE) + SLAB, U32), pltpu.VMEM((MOE_TILE,) + SLAB, U32),
                            pltpu.VMEM((d, D_EXPERT), F32), pltpu.VMEM((d, D_EXPERT), F32),
                            pltpu.VMEM((D_EXPERT, d), F32),
                            pltpu.VMEM((d, D_EXPERT), BF16), pltpu.VMEM((d, D_EXPERT), BF16),
                            pltpu.VMEM((D_EXPERT, d), BF16),
                            pltpu.SemaphoreType.DMA((2,)), pltpu.SemaphoreType.DMA((1,)),
                            pltpu.SemaphoreType.DMA((3,))]),
        out_shape=jax.ShapeDtypeStruct((TOP_K * t + MOE_TILE,) + SLAB, U32),
        compiler_params=_params("arbitrary"),
        name="moe_experts",
    )(tile_expert, n_used, next_expert, slot_src, slot_dst, xp, w_gate, w_up, w_down)


def _hier_moe(xp, route, counts, w_gate, w_up, w_down, layer):
    n_tok = xp.shape[0]
    n_asg = n_tok * TOP_K
    e_flat = route[:, :TOP_K].astype(I32).reshape(-1)
    order = jnp.argsort(e_flat).astype(I32)
    counts = counts[0, :N_EXPERTS].astype(I32)
    starts = jnp.cumsum(counts) - counts
    padded = (counts + MOE_TILE - 1) // MOE_TILE * MOE_TILE
    pad_ends = jnp.cumsum(padded)
    pad_starts = pad_ends - padded
    n_tiles = -(-n_asg // MOE_TILE) + N_EXPERTS
    n_used = pad_ends[-1] // MOE_TILE

    tile_start = jnp.arange(n_tiles, dtype=I32) * MOE_TILE
    used = tile_start < pad_ends[-1]
    tile_expert = jnp.minimum(jnp.sum((pad_ends[None, :] <= tile_start[:, None]).astype(I32), 1), N_EXPERTS - 1)
    next_tile = pad_ends[tile_expert] // MOE_TILE
    next_expert = jnp.where(used & (next_tile < n_used), tile_expert[jnp.minimum(next_tile, n_tiles - 1)], -1)
    rank0 = tile_start - pad_starts[tile_expert]
    n_valid = jnp.where(used, jnp.clip(counts[tile_expert] - rank0, 0, MOE_TILE), 0)
    sorted0 = starts[tile_expert] + rank0

    r = jnp.arange(MOE_TILE, dtype=I32)[None, :]
    valid = r < n_valid[:, None]
    asg = order[jnp.clip(sorted0[:, None] + r, 0, n_asg - 1)]
    slot_src = jnp.where(valid, asg // TOP_K, 0).reshape(-1)
    slot_dst = jnp.where(valid, (asg % TOP_K) * n_tok + asg // TOP_K, TOP_K * n_tok + r).reshape(-1)
    return _moe_experts(xp, tile_expert, n_used.reshape(1).astype(I32), next_expert.astype(I32),
                        slot_src, slot_dst, w_gate, w_up, w_down, layer)


def kernel(x, a_w_in, a_lower_bound, a_norm_g, a_w_out, b_w_kv, b_w_q, b_sinks, b_w_out, rel_bias,
           moe_w_rg, moe_b_rg, moe_w_re, moe_b_re, moe_w_gate, moe_w_up, moe_w_down, ln_g, ln_b):
    bsz, seq, d = x.shape
    t = bsz * seq
    lb_sm = jax.nn.softmax(a_lower_bound.astype(F32), axis=0)
    lower_bounds = jnp.cumsum(lb_sm, axis=0) - lb_sm[0]
    att_bias = _band_bias_t(rel_bias)
    pad = ROUTER_COLS - N_GROUPS - N_EXPERTS
    w_router = jnp.concatenate([moe_w_rg, moe_w_re, jnp.zeros((DEPTH, d, pad), F32)], axis=-1)
    w_router_hi = w_router.astype(BF16)
    w_router_lo = (w_router - w_router_hi.astype(F32)).astype(BF16)
    w_router = jnp.concatenate([w_router_hi, w_router_lo], axis=-1)
    b_router = jnp.concatenate([moe_b_rg, moe_b_re, jnp.zeros((DEPTH, pad), F32)], axis=-1)

    xf = x.reshape(t, d).astype(F32)
    xb = xf.astype(BF16)
    kv = None
    for layer in range(DEPTH):
        if layer < N_A_LAYERS:
            proj = _matmul(xb, a_w_in, layer, F32)
            o = _hgrn(proj, lower_bounds[layer], a_norm_g[layer], bsz, seq)
            h = _matmul(o, a_w_out, layer, F32)
        else:
            j = layer - N_A_LAYERS
            if kv is None:
                kv = _matmul(xb, b_w_kv[None], 0, BF16)
            q = _matmul(xb, b_w_q, j, BF16)
            o = _swa(q, kv, att_bias, b_sinks[j].astype(F32), bsz, seq)
            h = _matmul(o, b_w_out, j, F32)
        xf, xp, route, counts = _ln_router(xf, h, ln_g[2 * layer], ln_b[2 * layer],
                                           w_router[layer], b_router[layer].reshape(1, ROUTER_COLS))
        y_planes = _hier_moe(xp, route, counts, moe_w_gate, moe_w_up, moe_w_down, layer)
        xf, xb = _ln_moe(xf, y_planes, route, ln_g[2 * layer + 1], ln_b[2 * layer + 1])
    return xf.reshape(bsz, seq, d).astype(x.dtype)
```

```python
import functools
import math

import numpy as np
import jax
import jax.numpy as jnp
from jax import lax
from jax.experimental import pallas as pl
from jax.experimental.pallas import tpu as pltpu

F32 = jnp.float32
BF16 = jnp.bfloat16
U32 = jnp.uint32
I32 = jnp.int32

D_MODEL = 2048
DEPTH = 4
N_A_LAYERS = DEPTH // 2
HG_HEADS = 16
HG_DK = 128
HG_DV = 128
ATT_HEAD_DIM = 64
ATT_Q_HEADS = 32
ATT_KV_HEADS = 4
ATT_GROUP = ATT_Q_HEADS // ATT_KV_HEADS
WINDOW = 128
N_BUCKETS = 32
REL_MAX_DISTANCE = 128
N_GROUPS = 4
EXPERTS_PER_GROUP = 8
N_EXPERTS = N_GROUPS * EXPERTS_PER_GROUP
TOP_K = 2
D_EXPERT = D_MODEL // 4
DEEPNORM_ALPHA = (2 * DEPTH) ** 0.25
LN_EPS = 1e-5
RMS_EPS = 1e-6
NEG_BIG = -1e30
MIN_FORGET = 1e-30

LANES = 128
HG_CHUNK = 128
HG_LEVELS = 7
HG_HEADS_PER_STEP = 16
MOE_TILE = 256
ROUTER_COLS = 128
SUBLANES = 8
SLAB = (SUBLANES, LANES)
assert D_MODEL // 2 == SUBLANES * LANES
VMEM_LIMIT = 56 * 1024 * 1024
MOE_VMEM_LIMIT = 60 * 1024 * 1024
NT_DIMS = (((1,), (1,)), ((), ()))


def _params(*sem):
    return pltpu.CompilerParams(dimension_semantics=sem, vmem_limit_bytes=VMEM_LIMIT)


def _pack_bf16_pairs(y):
    half = y.shape[1] // 2
    lo = lax.bitcast_convert_type(y[:, :half].astype(BF16).astype(F32), U32)
    hi = lax.bitcast_convert_type(y[:, half:].astype(BF16).astype(F32), U32)
    w = (hi & jnp.uint32(0xFFFF0000)) | (lo >> 16)
    return jnp.swapaxes(jnp.stack([w[:, s * LANES:(s + 1) * LANES] for s in range(SUBLANES)], axis=0), 0, 1)


def _unpack_bf16_pairs(slabs):
    cols = jnp.swapaxes(slabs, 0, 1)
    w = jnp.concatenate([cols[s] for s in range(SUBLANES)], axis=1)
    lo = lax.bitcast_convert_type(w << 16, F32)
    hi = lax.bitcast_convert_type(w & jnp.uint32(0xFFFF0000), F32)
    return jnp.concatenate([lo, hi], axis=1)


def _mm_kernel(x_ref, w_ref, o_ref, wb_ref):
    @pl.when(pl.program_id(1) == 0)
    def _():
        wb_ref[...] = w_ref[...].astype(BF16)

    o_ref[...] = jnp.dot(x_ref[...], wb_ref[...], preferred_element_type=F32).astype(o_ref.dtype)


def _matmul(x, w, layer, out_dtype, tm=1024, tn=1024):
    m, k = x.shape
    n = w.shape[2]
    tm, tn = min(tm, m), min(tn, n)
    return pl.pallas_call(
        _mm_kernel,
        grid=(n // tn, m // tm),
        in_specs=[pl.BlockSpec((tm, k), lambda j, i: (i, 0)),
                  pl.BlockSpec((None, k, tn), lambda j, i: (layer, 0, j))],
        out_specs=pl.BlockSpec((tm, tn), lambda j, i: (i, j)),
        out_shape=jax.ShapeDtypeStruct((m, n), out_dtype),
        scratch_shapes=[pltpu.VMEM((k, tn), BF16)],
        compiler_params=_params("arbitrary", "arbitrary"),
        name="matmul",
    )(x, w)


def _layernorm(v, g, b):
    mu = jnp.mean(v, -1, keepdims=True)
    d = v - mu
    var = jnp.mean(d * d, -1, keepdims=True)
    return d * lax.rsqrt(var + LN_EPS) * g + b


def _route(logits):
    lane = lax.broadcasted_iota(I32, logits.shape, 1)
    big = jnp.int32(ROUTER_COLS)
    is_grp = lane < N_GROUPS
    gl = jnp.where(is_grp, logits, -jnp.inf)
    g_max = jnp.max(gl, -1, keepdims=True)
    grp = jnp.min(jnp.where(is_grp & (gl == g_max), lane, big), -1, keepdims=True)
    p_grp = 1.0 / jnp.sum(jnp.exp(gl - g_max), -1, keepdims=True)
    in_grp = (lane >= N_GROUPS) & (((lane - N_GROUPS) >> 3) == grp)
    el = jnp.where(in_grp, logits, -jnp.inf)
    v1 = jnp.max(el, -1, keepdims=True)
    i1 = jnp.min(jnp.where(in_grp & (el == v1), lane, big), -1, keepdims=True)
    el2 = jnp.where(lane == i1, -jnp.inf, el)
    v2 = jnp.max(el2, -1, keepdims=True)
    i2 = jnp.min(jnp.where(in_grp & (lane != i1) & (el2 == v2), lane, big), -1, keepdims=True)
    ex = jnp.exp(v2 - v1)
    w1 = 1.0 / (1.0 + ex)
    return i1 - N_GROUPS, i2 - N_GROUPS, p_grp * w1, p_grp * (ex * w1)


def _ln_router_kernel(x_ref, h_ref, g_ref, b_ref, wr_ref, br_ref, xo_ref, xp_ref, rt_ref, cnt_ref):
    y = _layernorm(DEEPNORM_ALPHA * x_ref[...] + h_ref[...], g_ref[...], b_ref[...])
    xo_ref[...] = y
    xp_ref[...] = _pack_bf16_pairs(y)
    y_hi = y.astype(BF16)
    y_lo = (y - y_hi.astype(F32)).astype(BF16)
    t_hi = jnp.dot(y_hi, wr_ref[...], preferred_element_type=F32)
    t_lo = jnp.dot(y_lo, wr_ref[:, :ROUTER_COLS], preferred_element_type=F32)
    logits = t_hi[:, :ROUTER_COLS] + t_hi[:, ROUTER_COLS:] + t_lo + br_ref[...]
    e1, e2, g1, g2 = _route(logits)
    lane = lax.broadcasted_iota(I32, logits.shape, 1)
    rt_ref[...] = jnp.where(lane == 0, e1.astype(F32),
                            jnp.where(lane == 1, e2.astype(F32),
                                      jnp.where(lane == 2, g1, jnp.where(lane == 3, g2, 0.0))))

    @pl.when(pl.program_id(0) == 0)
    def _():
        cnt_ref[...] = jnp.zeros_like(cnt_ref)

    hits = (lane == e1).astype(F32) + (lane == e2).astype(F32)
    cnt_ref[...] += jnp.sum(hits, 0, keepdims=True)


def _ln_router(x, h, g, b, wr, br, tm=256):
    t, d = x.shape
    row = lambda i: (i, 0)
    const = lambda i: (0, 0)
    return pl.pallas_call(
        _ln_router_kernel,
        grid=(t // tm,),
        in_specs=[pl.BlockSpec((tm, d), row), pl.BlockSpec((tm, d), row),
                  pl.BlockSpec((1, d), const), pl.BlockSpec((1, d), const),
                  pl.BlockSpec((d, 2 * ROUTER_COLS), const), pl.BlockSpec((1, ROUTER_COLS), const)],
        out_specs=[pl.BlockSpec((tm, d), row), pl.BlockSpec((tm,) + SLAB, lambda i: (i, 0, 0)),
                   pl.BlockSpec((tm, ROUTER_COLS), row), pl.BlockSpec((1, ROUTER_COLS), const)],
        out_shape=[jax.ShapeDtypeStruct((t, d), F32), jax.ShapeDtypeStruct((t,) + SLAB, U32),
                   jax.ShapeDtypeStruct((t, ROUTER_COLS), F32), jax.ShapeDtypeStruct((1, ROUTER_COLS), F32)],
        compiler_params=_params("arbitrary"),
        name="ln_router",
    )(x, h, g.reshape(1, d), b.reshape(1, d), wr, br)


def _ln_moe_kernel(x_ref, y0_ref, y1_ref, rt_ref, g_ref, b_ref, xo_ref, xb_ref):
    rt = rt_ref[...]
    f = rt[:, 2:3] * _unpack_bf16_pairs(y0_ref[...]) + rt[:, 3:4] * _unpack_bf16_pairs(y1_ref[...])
    y = _layernorm(DEEPNORM_ALPHA * x_ref[...] + f, g_ref[...], b_ref[...])
    xo_ref[...] = y
    xb_ref[...] = y.astype(BF16)


def _ln_moe(x, y_planes, route, g, b, tm=256):
    t, d = x.shape
    row = lambda i: (i, 0)
    const = lambda i: (0, 0)
    return pl.pallas_call(
        _ln_moe_kernel,
        grid=(t // tm,),
        in_specs=[pl.BlockSpec((tm, d), row), pl.BlockSpec((tm,) + SLAB, lambda i: (i, 0, 0)),
                  pl.BlockSpec((tm,) + SLAB, lambda i: (i + t // tm, 0, 0)),
                  pl.BlockSpec((tm, ROUTER_COLS), row),
                  pl.BlockSpec((1, d), const), pl.BlockSpec((1, d), const)],
        out_specs=[pl.BlockSpec((tm, d), row), pl.BlockSpec((tm, d), row)],
        out_shape=[jax.ShapeDtypeStruct((t, d), F32), jax.ShapeDtypeStruct((t, d), BF16)],
        compiler_params=_params("arbitrary"),
        name="ln_moe",
    )(x, y_planes, y_planes, route, g.reshape(1, d), b.reshape(1, d))


def _hgrn_constants():
    c = HG_CHUNK
    t = np.arange(c)[:, None]
    s = np.arange(c)[None, :]
    masks = [(s == t).astype(np.float32)]
    for lvl in range(HG_LEVELS):
        m = 1 << lvl
        masks.append(((t // (2 * m) == s // (2 * m)) & ((t // m) % 2 == 1) & ((s // m) % 2 == 0)).astype(np.float32))
    return (s <= t).astype(np.float32), np.stack(masks, axis=0)


def _rows_broadcast(b, first, period, reps):
    n = b.shape[0] // period
    return jnp.concatenate([jnp.broadcast_to(b[first + j * period:first + j * period + 1, :], (reps, b.shape[1]))
                            for j in range(n) for _ in range(period // reps)], axis=0)


def _level_decay(b, fclip, row, lvl):
    if lvl == 0:
        return jnp.where((row & 1) == 1, fclip, 1.0)
    m = 1 << lvl
    if 2 * m < 8:
        ref = jnp.where((row & m * 2) == 0, _rows_broadcast(b, m - 1, 8, 8), _rows_broadcast(b, 3 * m - 1, 8, 8))
    else:
        ref = _rows_broadcast(b, m - 1, 2 * m, 2 * m)
    return jnp.exp(-jnp.abs(b - ref))


class _RowBlocks:
    def __init__(self, n):
        self.tiles = [None] * (n // SUBLANES)

    def add(self, first_row, val):
        for u in range(val.shape[0] // SUBLANES):
            piece = val[u * SUBLANES:(u + 1) * SUBLANES]
            i = first_row // SUBLANES + u
            self.tiles[i] = piece if self.tiles[i] is None else self.tiles[i] + piece

    def value(self):
        return jnp.concatenate(self.tiles, axis=0)


def _hgrn_kernel(q_ref, f_ref, i_ref, g_ref, lb_ref, ng_ref, lm_ref, pm_ref, o_ref, st_ref):
    c = HG_CHUNK
    nh = HG_HEADS_PER_STEP

    @pl.when(pl.program_id(2) == 0)
    def _():
        st_ref[...] = jnp.zeros_like(st_ref)

    row = lax.broadcasted_iota(I32, (c, LANES), 0)

    for j in range(nh):
        cols = slice(j * LANES, (j + 1) * LANES)
        q = q_ref[:, cols]
        v = i_ref[:, cols]
        lb = lb_ref[:, cols]
        sig = jax.nn.sigmoid(f_ref[:, cols])
        fclip = jnp.maximum(lb + (1.0 - lb) * sig, MIN_FORGET)
        logf = jnp.log(fclip)
        k = (1.0 - lb) * (1.0 - sig)

        g_hi = logf.astype(BF16)
        g_lo = (logf - g_hi.astype(F32)).astype(BF16)
        b2 = jnp.dot(lm_ref[...], jnp.concatenate([g_hi, g_lo], axis=1), preferred_element_type=F32)
        b = b2[:, :LANES] + b2[:, LANES:]
        b_last = b[c - 1:c, :]

        st = st_ref[j]
        o = lax.dot_general((q * jnp.exp(b)).astype(BF16), st.astype(BF16), NT_DIMS, preferred_element_type=F32)

        scores = _RowBlocks(c)
        scores.add(0, pm_ref[0] * lax.dot_general(q.astype(BF16), k.astype(BF16), NT_DIMS, preferred_element_type=F32))
        for lvl in range(HG_LEVELS):
            e = _level_decay(b, fclip, row, lvl)
            m = 1 << lvl
            if m < SUBLANES:
                s_l = lax.dot_general((q * e).astype(BF16), (k * e).astype(BF16), NT_DIMS, preferred_element_type=F32)
                scores.add(0, pm_ref[lvl + 1] * s_l)
            else:
                up = [slice(u * 2 * m + m, (u + 1) * 2 * m) for u in range(c // (2 * m))]
                lo = [slice(u * 2 * m, u * 2 * m + m) for u in range(c // (2 * m))]
                qe = jnp.concatenate([q[r] * e[r] for r in up], axis=0).astype(BF16)
                ke = jnp.concatenate([piece for r in lo for piece in (k[r] * e[r], jnp.zeros((m, LANES), F32))],
                                     axis=0).astype(BF16)
                s_l = lax.dot_general(qe, ke, NT_DIMS, preferred_element_type=F32)
                for u, r in enumerate(up):
                    scores.add(r.start, pm_ref[lvl + 1, r, :] * s_l[u * m:(u + 1) * m])
        o = o + jnp.dot(scores.value().astype(BF16), v.astype(BF16), preferred_element_type=F32)

        ks = (k * jnp.exp(b_last - b)).astype(BF16)
        st_ref[j] = st * jnp.exp(b_last) + jnp.dot(v.T.astype(BF16), ks, preferred_element_type=F32)

        o = o * lax.rsqrt(jnp.mean(o * o, -1, keepdims=True) + RMS_EPS) * ng_ref[:, cols]
        gate = g_ref[:, cols]
        o_ref[:, cols] = (o * (gate * jax.nn.sigmoid(gate))).astype(o_ref.dtype)


def _hgrn(proj, lb, norm_g, bsz, seq):
    t = bsz * seq
    nc = seq // HG_CHUNK
    c = HG_CHUNK
    nh = HG_HEADS_PER_STEP
    w = nh * HG_DK
    hsteps = HG_HEADS // nh
    lm, pm = _hgrn_constants()

    def part(p):
        return pl.BlockSpec((c, w), lambda b, h, n: (b * nc + n, p * hsteps + h))

    head = pl.BlockSpec((1, w), lambda b, h, n: (0, h))
    return pl.pallas_call(
        _hgrn_kernel,
        grid=(bsz, hsteps, nc),
        in_specs=[part(0), part(1), part(2), part(3), head, head,
                  pl.BlockSpec((c, c), lambda b, h, n: (0, 0)),
                  pl.BlockSpec((HG_LEVELS + 1, c, c), lambda b, h, n: (0, 0, 0))],
        out_specs=pl.BlockSpec((c, w), lambda b, h, n: (b * nc + n, h)),
        out_shape=jax.ShapeDtypeStruct((t, D_MODEL), BF16),
        scratch_shapes=[pltpu.VMEM((nh, HG_DV, HG_DK), F32)],
        compiler_params=_params("arbitrary", "arbitrary", "arbitrary"),
        name="hgrn2",
    )(proj, proj, proj, proj, lb.reshape(1, D_MODEL), norm_g.reshape(1, D_MODEL),
      jnp.asarray(lm, BF16), jnp.asarray(pm, F32))


def _swa_kernel(sink_ref, q_ref, kvp_ref, kvc_ref, bias_ref, o_ref):
    n = pl.program_id(1)
    w = WINDOW
    hd = ATT_HEAD_DIM
    cols = ATT_GROUP * w
    kj = lax.broadcasted_iota(I32, (2 * w, cols), 0)
    qi = lax.broadcasted_iota(I32, (2 * w, cols), 1) & (w - 1)
    dist = qi + w - kj
    mask = (dist >= 0) & (dist < w) & ((n > 0) | (kj >= w))
    kvw = ATT_KV_HEADS * hd
    ones = jnp.ones((2 * w, hd), F32)
    for g in range(ATT_KV_HEADS):
        kwin = jnp.concatenate([kvp_ref[:, g * hd:(g + 1) * hd], kvc_ref[:, g * hd:(g + 1) * hd]], axis=0)
        vwin = jnp.concatenate([kvp_ref[:, kvw + g * hd:kvw + (g + 1) * hd],
                                kvc_ref[:, kvw + g * hd:kvw + (g + 1) * hd]], axis=0)
        heads = range(g * ATT_GROUP, (g + 1) * ATT_GROUP)
        qg = jnp.concatenate([q_ref[:, h * hd:(h + 1) * hd] for h in heads], axis=0) * (hd ** -0.5)
        sink = jnp.concatenate([jnp.full((1, w), sink_ref[h], F32) for h in heads], axis=1)
        s = lax.dot_general(kwin, qg.astype(BF16), NT_DIMS, preferred_element_type=F32) + bias_ref[g]
        s = jnp.where(mask, s, NEG_BIG)
        m = jnp.maximum(jnp.max(s, 0, keepdims=True), sink)
        p = jnp.exp(s - m).astype(BF16)
        vext_t = jnp.concatenate([vwin.astype(F32), ones], axis=1).T.astype(BF16)
        ov = jnp.dot(vext_t, p, preferred_element_type=F32)
        ov = ov / (ov[hd:hd + 1, :] + jnp.exp(sink - m))
        og = ov.T.astype(o_ref.dtype)
        o_ref[:, g * ATT_GROUP * hd:(g + 1) * ATT_GROUP * hd] = jnp.concatenate(
            [og[j * w:(j + 1) * w, :hd] for j in range(ATT_GROUP)], axis=1)


def _swa(q, kv, bias, sinks, bsz, seq):
    t = bsz * seq
    nb = seq // WINDOW
    kvw2 = 2 * ATT_KV_HEADS * ATT_HEAD_DIM
    return pl.pallas_call(
        _swa_kernel,
        grid=(bsz, nb),
        in_specs=[pl.BlockSpec(memory_space=pltpu.SMEM),
                  pl.BlockSpec((WINDOW, D_MODEL), lambda b, n: (b * nb + n, 0)),
                  pl.BlockSpec((WINDOW, kvw2), lambda b, n: (b * nb + jnp.maximum(n - 1, 0), 0)),
                  pl.BlockSpec((WINDOW, kvw2), lambda b, n: (b * nb + n, 0)),
                  pl.BlockSpec((ATT_KV_HEADS, 2 * WINDOW, ATT_GROUP * WINDOW), lambda b, n: (0, 0, 0))],
        out_specs=pl.BlockSpec((WINDOW, D_MODEL), lambda b, n: (b * nb + n, 0)),
        out_shape=jax.ShapeDtypeStruct((t, D_MODEL), BF16),
        compiler_params=_params("arbitrary", "arbitrary"),
        name="swa",
    )(sinks, q, kv, kv, bias)


def _t5_bucket(dist):
    n = jnp.clip(dist, 0, REL_MAX_DISTANCE - 1)
    max_exact = N_BUCKETS // 2
    large = max_exact + (jnp.log(jnp.maximum(n, max_exact).astype(F32) / max_exact)
                         / math.log(REL_MAX_DISTANCE / max_exact)
                         * (N_BUCKETS - max_exact)).astype(I32)
    large = jnp.minimum(large, N_BUCKETS - 1)
    return jnp.where(n < max_exact, n, large)


def _band_bias_t(rel_bias):
    kj = jnp.arange(2 * WINDOW)[:, None]
    qi = jnp.arange(WINDOW)[None, :]
    bucket = _t5_bucket(qi + WINDOW - kj).reshape(-1)
    onehot = (bucket[:, None] == jnp.arange(N_BUCKETS)[None, :]).astype(F32)
    table = jnp.dot(onehot, rel_bias.astype(F32), precision=lax.Precision.HIGHEST)
    table = table.reshape(2 * WINDOW, WINDOW, ATT_KV_HEADS, ATT_GROUP)
    return table.transpose(2, 0, 3, 1).reshape(ATT_KV_HEADS, 2 * WINDOW, ATT_GROUP * WINDOW)


def _moe_kernel(layer, te_ref, nu_ref, nx_ref, src_ref, dst_ref,
                x_hbm, wg_hbm, wu_hbm, wd_hbm, y_hbm,
                xres, xbuf, ybuf, wgs_ref, wus_ref, wds_ref, wgb_ref, wub_ref, wdb_ref, in_sem, out_sem, w_sem):
    i = pl.program_id(0)
    n_used = nu_ref[0]
    rows = MOE_TILE

    def gather_rows(tile):
        for r in range(rows):
            xbuf[r] = xres[src_ref[tile * rows + r]]

    def start_scatter(tile):
        for r in range(rows):
            pltpu.make_async_copy(ybuf.at[r], y_hbm.at[dst_ref[tile * rows + r]], out_sem.at[0]).start(priority=r % 2)

    def wait_scatter():
        pltpu.make_async_copy(ybuf, ybuf, out_sem.at[0]).wait()

    def weight_copies(e):
        return (pltpu.make_async_copy(wg_hbm.at[layer, e], wgs_ref, w_sem.at[0]),
                pltpu.make_async_copy(wu_hbm.at[layer, e], wus_ref, w_sem.at[1]),
                pltpu.make_async_copy(wd_hbm.at[layer, e], wds_ref, w_sem.at[2]))

    @pl.when(i < n_used)
    def _():
        @pl.when(i == 0)
        def _():
            for cp in weight_copies(te_ref[0]):
                cp.start()
            load_x = pltpu.make_async_copy(x_hbm, xres, in_sem.at[0])
            load_x.start()
            ybuf[...] = jnp.zeros_like(ybuf)
            dump = pltpu.make_async_copy(ybuf, y_hbm.at[pl.ds(y_hbm.shape[0] - rows, rows)], out_sem.at[0])
            dump.start()
            dump.wait()
            load_x.wait()

        @pl.when((i == 0) | (te_ref[i] != te_ref[jnp.maximum(i - 1, 0)]))
        def _():
            for cp in weight_copies(te_ref[i]):
                cp.wait()
            wgb_ref[...] = wgs_ref[...].astype(BF16)
            wub_ref[...] = wus_ref[...].astype(BF16)
            wdb_ref[...] = wds_ref[...].astype(BF16)

            @pl.when(nx_ref[i] >= 0)
            def _():
                for cp in weight_copies(nx_ref[i]):
                    cp.start()

        gather_rows(i)
        x = _unpack_bf16_pairs(xbuf[...]).astype(BF16)
        hg = jnp.dot(x, wgb_ref[...], preferred_element_type=F32)
        hu = jnp.dot(x, wub_ref[...], preferred_element_type=F32)
        hidden = (hg * jax.nn.sigmoid(hg) * hu).astype(BF16)
        y = jnp.dot(hidden, wdb_ref[...], preferred_element_type=F32)

        @pl.when(i > 0)
        def _():
            wait_scatter()

        ybuf[...] = _pack_bf16_pairs(y)
        start_scatter(i)

        @pl.when(i == n_used - 1)
        def _():
            wait_scatter()


def _moe_experts(xp, tile_expert, n_used, next_expert, slot_src, slot_dst, w_gate, w_up, w_down, layer):
    t = xp.shape[0]
    d = D_MODEL
    n_tiles = tile_expert.shape[0]
    any_spec = pl.BlockSpec(memory_space=pl.ANY)
    return pl.pallas_call(
        functools.partial(_moe_kernel, layer),
        grid_spec=pltpu.PrefetchScalarGridSpec(
            num_scalar_prefetch=5,
            grid=(n_tiles,),
            in_specs=[any_spec, any_spec, any_spec, any_spec],
            out_specs=any_spec,
            scratch_shapes=[pltpu.VMEM((t,) + SLAB, U32),
                            pltpu.VMEM((MOE_TILE,) + SLAB, U32), pltpu.VMEM((MOE_TILE,) + SLAB, U32),
                            pltpu.VMEM((d, D_EXPERT), F32), pltpu.VMEM((d, D_EXPERT), F32),
                            pltpu.VMEM((D_EXPERT, d), F32),
                            pltpu.VMEM((d, D_EXPERT), BF16), pltpu.VMEM((d, D_EXPERT), BF16),
                            pltpu.VMEM((D_EXPERT, d), BF16),
                            pltpu.SemaphoreType.DMA((1,)), pltpu.SemaphoreType.DMA((1,)),
                            pltpu.SemaphoreType.DMA((3,))]),
        out_shape=jax.ShapeDtypeStruct((TOP_K * t + MOE_TILE,) + SLAB, U32),
        compiler_params=pltpu.CompilerParams(dimension_semantics=("arbitrary",), vmem_limit_bytes=MOE_VMEM_LIMIT),
        name="moe_experts",
    )(tile_expert, n_used, next_expert, slot_src, slot_dst, xp, w_gate, w_up, w_down)


def _hier_moe(xp, route, counts, w_gate, w_up, w_down, layer):
    n_tok = xp.shape[0]
    n_asg = n_tok * TOP_K
    e_flat = route[:, :TOP_K].astype(I32).reshape(-1)
    order = jnp.argsort(e_flat).astype(I32)
    counts = counts[0, :N_EXPERTS].astype(I32)
    starts = jnp.cumsum(counts) - counts
    padded = (counts + MOE_TILE - 1) // MOE_TILE * MOE_TILE
    pad_ends = jnp.cumsum(padded)
    pad_starts = pad_ends - padded
    n_tiles = -(-n_asg // MOE_TILE) + N_EXPERTS
    n_used = pad_ends[-1] // MOE_TILE

    tile_start = jnp.arange(n_tiles, dtype=I32) * MOE_TILE
    used = tile_start < pad_ends[-1]
    tile_expert = jnp.minimum(jnp.sum((pad_ends[None, :] <= tile_start[:, None]).astype(I32), 1), N_EXPERTS - 1)
    next_tile = pad_ends[tile_expert] // MOE_TILE
    next_expert = jnp.where(used & (next_tile < n_used), tile_expert[jnp.minimum(next_tile, n_tiles - 1)], -1)
    rank0 = tile_start - pad_starts[tile_expert]
    n_valid = jnp.where(used, jnp.clip(counts[tile_expert] - rank0, 0, MOE_TILE), 0)
    sorted0 = starts[tile_expert] + rank0

    r = jnp.arange(MOE_TILE, dtype=I32)[None, :]
    valid = r < n_valid[:, None]
    asg = order[jnp.clip(sorted0[:, None] + r, 0, n_asg - 1)]
    slot_src = jnp.where(valid, asg // TOP_K, 0).reshape(-1)
    slot_dst = jnp.where(valid, (asg % TOP_K) * n_tok + asg // TOP_K, TOP_K * n_tok + r).reshape(-1)
    return _moe_experts(xp, tile_expert, n_used.reshape(1).astype(I32), next_expert.astype(I32),
                        slot_src, slot_dst, w_gate, w_up, w_down, layer)


def kernel(x, a_w_in, a_lower_bound, a_norm_g, a_w_out, b_w_kv, b_w_q, b_sinks, b_w_out, rel_bias,
           moe_w_rg, moe_b_rg, moe_w_re, moe_b_re, moe_w_gate, moe_w_up, moe_w_down, ln_g, ln_b):
    bsz, seq, d = x.shape
    t = bsz * seq
    lb_sm = jax.nn.softmax(a_lower_bound.astype(F32), axis=0)
    lower_bounds = jnp.cumsum(lb_sm, axis=0) - lb_sm[0]
    att_bias = _band_bias_t(rel_bias)
    pad = ROUTER_COLS - N_GROUPS - N_EXPERTS
    w_router = jnp.concatenate([moe_w_rg, moe_w_re, jnp.zeros((DEPTH, d, pad), F32)], axis=-1)
    w_router_hi = w_router.astype(BF16)
    w_router_lo = (w_router - w_router_hi.astype(F32)).astype(BF16)
    w_router = jnp.concatenate([w_router_hi, w_router_lo], axis=-1)
    b_router = jnp.concatenate([moe_b_rg, moe_b_re, jnp.zeros((DEPTH, pad), F32)], axis=-1)

    xf = x.reshape(t, d).astype(F32)
    xb = xf.astype(BF16)
    kv = None
    for layer in range(DEPTH):
        if layer < N_A_LAYERS:
            proj = _matmul(xb, a_w_in, layer, F32)
            o = _hgrn(proj, lower_bounds[layer], a_norm_g[layer], bsz, seq)
            h = _matmul(o, a_w_out, layer, F32)
        else:
            j = layer - N_A_LAYERS
            if kv is None:
                kv = _matmul(xb, b_w_kv[None], 0, BF16)
            q = _matmul(xb, b_w_q, j, BF16)
            o = _swa(q, kv, att_bias, b_sinks[j].astype(F32), bsz, seq)
            h = _matmul(o, b_w_out, j, F32)
        xf, xp, route, counts = _ln_router(xf, h, ln_g[2 * layer], ln_b[2 * layer],
                                           w_router[layer], b_router[layer].reshape(1, ROUTER_COLS))
        y_planes = _hier_moe(xp, route, counts, moe_w_gate, moe_w_up, moe_w_down, layer)
        xf, xb = _ln_moe(xf, y_planes, route, ln_g[2 * layer + 1], ln_b[2 * layer + 1])
    return xf.reshape(bsz, seq, d).astype(x.dtype)
```

```python
import functools
import math

import numpy as np
import jax
import jax.numpy as jnp
from jax import lax
from jax.experimental import pallas as pl
from jax.experimental.pallas import tpu as pltpu

F32 = jnp.float32
BF16 = jnp.bfloat16
U32 = jnp.uint32
I32 = jnp.int32

D_MODEL = 2048
DEPTH = 4
N_A_LAYERS = DEPTH // 2
HG_HEADS = 16
HG_DK = 128
HG_DV = 128
ATT_HEAD_DIM = 64
ATT_Q_HEADS = 32
ATT_KV_HEADS = 4
ATT_GROUP = ATT_Q_HEADS // ATT_KV_HEADS
WINDOW = 128
N_BUCKETS = 32
REL_MAX_DISTANCE = 128
N_GROUPS = 4
EXPERTS_PER_GROUP = 8
N_EXPERTS = N_GROUPS * EXPERTS_PER_GROUP
TOP_K = 2
D_EXPERT = D_MODEL // 4
DEEPNORM_ALPHA = (2 * DEPTH) ** 0.25
LN_EPS = 1e-5
RMS_EPS = 1e-6
NEG_BIG = -1e30
MIN_FORGET = 1e-30
LOG2_E = 1.0 / math.log(2.0)

LANES = 128
HG_CHUNK = 128
HG_LEVELS = 7
HG_HEADS_PER_STEP = 16
MOE_TILE = 256
OUT_LN_CHUNK = 512
ROUTER_COLS = 128
SUBLANES = 8
SLAB = (SUBLANES, LANES)
assert D_MODEL // 2 == SUBLANES * LANES
VMEM_LIMIT = 56 * 1024 * 1024
MOE_VMEM_LIMIT = 60 * 1024 * 1024
NT_DIMS = (((1,), (1,)), ((), ()))


def _params(*sem):
    return pltpu.CompilerParams(dimension_semantics=sem, vmem_limit_bytes=VMEM_LIMIT)


def _pack_bf16_pairs(y):
    half = y.shape[1] // 2
    lo = lax.bitcast_convert_type(y[:, :half].astype(BF16).astype(F32), U32)
    hi = lax.bitcast_convert_type(y[:, half:].astype(BF16).astype(F32), U32)
    w = (hi & jnp.uint32(0xFFFF0000)) | (lo >> 16)
    return jnp.swapaxes(jnp.stack([w[:, s * LANES:(s + 1) * LANES] for s in range(SUBLANES)], axis=0), 0, 1)


def _unpack_bf16_pairs(slabs):
    cols = jnp.swapaxes(slabs, 0, 1)
    w = jnp.concatenate([cols[s] for s in range(SUBLANES)], axis=1)
    lo = lax.bitcast_convert_type(w << 16, F32)
    hi = lax.bitcast_convert_type(w & jnp.uint32(0xFFFF0000), F32)
    return jnp.concatenate([lo, hi], axis=1)


def _mm_kernel(x_ref, w_ref, o_ref, wb_ref):
    @pl.when(pl.program_id(1) == 0)
    def _():
        wb_ref[...] = w_ref[...].astype(BF16)

    o_ref[...] = jnp.dot(x_ref[...], wb_ref[...], preferred_element_type=F32).astype(o_ref.dtype)


def _matmul(x, w, layer, out_dtype, tm=1024, tn=1024):
    m, k = x.shape
    n = w.shape[2]
    tm, tn = min(tm, m), min(tn, n)
    return pl.pallas_call(
        _mm_kernel,
        grid=(n // tn, m // tm),
        in_specs=[pl.BlockSpec((tm, k), lambda j, i: (i, 0)),
                  pl.BlockSpec((None, k, tn), lambda j, i: (layer, 0, j))],
        out_specs=pl.BlockSpec((tm, tn), lambda j, i: (i, j)),
        out_shape=jax.ShapeDtypeStruct((m, n), out_dtype),
        scratch_shapes=[pltpu.VMEM((k, tn), BF16)],
        compiler_params=_params("arbitrary", "arbitrary"),
        name="matmul",
    )(x, w)


def _layernorm(v, g, b):
    mu = jnp.mean(v, -1, keepdims=True)
    d = v - mu
    var = jnp.mean(d * d, -1, keepdims=True)
    return d * lax.rsqrt(var + LN_EPS) * g + b


def _route(logits):
    lane = lax.broadcasted_iota(I32, logits.shape, 1)
    big = jnp.int32(ROUTER_COLS)
    is_grp = lane < N_GROUPS
    gl = jnp.where(is_grp, logits, -jnp.inf)
    g_max = jnp.max(gl, -1, keepdims=True)
    grp = jnp.min(jnp.where(is_grp & (gl == g_max), lane, big), -1, keepdims=True)
    p_grp = 1.0 / jnp.sum(jnp.exp(gl - g_max), -1, keepdims=True)
    in_grp = (lane >= N_GROUPS) & (((lane - N_GROUPS) >> 3) == grp)
    el = jnp.where(in_grp, logits, -jnp.inf)
    v1 = jnp.max(el, -1, keepdims=True)
    i1 = jnp.min(jnp.where(in_grp & (el == v1), lane, big), -1, keepdims=True)
    el2 = jnp.where(lane == i1, -jnp.inf, el)
    v2 = jnp.max(el2, -1, keepdims=True)
    i2 = jnp.min(jnp.where(in_grp & (lane != i1) & (el2 == v2), lane, big), -1, keepdims=True)
    ex = jnp.exp(v2 - v1)
    w1 = 1.0 / (1.0 + ex)
    return i1 - N_GROUPS, i2 - N_GROUPS, p_grp * w1, p_grp * (ex * w1)


def _out_ln_router_kernel(layer, o_ref, w_hbm, x_ref, g_ref, b_ref, wr_ref, br_ref,
                          xo_ref, xp_ref, rt_ref, cnt_ref, ws_ref, wb_ref, w_sem):
    @pl.when(pl.program_id(0) == 0)
    def _():
        load_w = pltpu.make_async_copy(w_hbm.at[layer], ws_ref, w_sem.at[0])
        load_w.start()
        cnt_ref[...] = jnp.zeros_like(cnt_ref)
        load_w.wait()
        wb_ref[...] = ws_ref[...].astype(BF16)

    tm = o_ref.shape[0]
    chunk = min(OUT_LN_CHUNK, tm)
    for r0 in range(0, tm, chunk):
        rows = slice(r0, r0 + chunk)
        h = jnp.dot(o_ref[rows, :], wb_ref[...], preferred_element_type=F32)
        y = _layernorm(DEEPNORM_ALPHA * x_ref[rows, :] + h, g_ref[...], b_ref[...])
        xo_ref[rows, :] = y
        xp_ref[rows] = _pack_bf16_pairs(y)
        y_hi = y.astype(BF16)
        y_lo = (y - y_hi.astype(F32)).astype(BF16)
        t_hi = jnp.dot(y_hi, wr_ref[...], preferred_element_type=F32)
        t_lo = jnp.dot(y_lo, wr_ref[:, :ROUTER_COLS], preferred_element_type=F32)
        logits = t_hi[:, :ROUTER_COLS] + t_hi[:, ROUTER_COLS:] + t_lo + br_ref[...]
        e1, e2, g1, g2 = _route(logits)
        lane = lax.broadcasted_iota(I32, logits.shape, 1)
        rt_ref[rows, :] = jnp.where(lane == 0, e1.astype(F32),
                                    jnp.where(lane == 1, e2.astype(F32),
                                              jnp.where(lane == 2, g1, jnp.where(lane == 3, g2, 0.0))))
        hits = (lane == e1).astype(F32) + (lane == e2).astype(F32)
        cnt_ref[...] += jnp.sum(hits, 0, keepdims=True)


def _out_ln_router(o, w, layer, x, g, b, wr, br, tm=512):
    t, d = x.shape
    tm = min(tm, t)
    row = lambda i: (i, 0)
    const = lambda i: (0, 0)
    return pl.pallas_call(
        functools.partial(_out_ln_router_kernel, layer),
        grid=(t // tm,),
        in_specs=[pl.BlockSpec((tm, d), row), pl.BlockSpec(memory_space=pl.ANY), pl.BlockSpec((tm, d), row),
                  pl.BlockSpec((1, d), const), pl.BlockSpec((1, d), const),
                  pl.BlockSpec((d, 2 * ROUTER_COLS), const), pl.BlockSpec((1, ROUTER_COLS), const)],
        out_specs=[pl.BlockSpec((tm, d), row), pl.BlockSpec((tm,) + SLAB, lambda i: (i, 0, 0)),
                   pl.BlockSpec((tm, ROUTER_COLS), row), pl.BlockSpec((1, ROUTER_COLS), const)],
        out_shape=[jax.ShapeDtypeStruct((t, d), F32), jax.ShapeDtypeStruct((t,) + SLAB, U32),
                   jax.ShapeDtypeStruct((t, ROUTER_COLS), F32), jax.ShapeDtypeStruct((1, ROUTER_COLS), F32)],
        scratch_shapes=[pltpu.VMEM((d, d), F32), pltpu.VMEM((d, d), BF16), pltpu.SemaphoreType.DMA((1,))],
        compiler_params=_params("arbitrary"),
        name="out_ln_router",
    )(o, w, x, g.reshape(1, d), b.reshape(1, d), wr, br)


def _ln_moe_kernel(x_ref, y0_ref, y1_ref, rt_ref, g_ref, b_ref, xo_ref, xb_ref):
    rt = rt_ref[...]
    f = rt[:, 2:3] * _unpack_bf16_pairs(y0_ref[...]) + rt[:, 3:4] * _unpack_bf16_pairs(y1_ref[...])
    y = _layernorm(DEEPNORM_ALPHA * x_ref[...] + f, g_ref[...], b_ref[...])
    xo_ref[...] = y
    xb_ref[...] = y.astype(BF16)


def _ln_moe(x, y_planes, route, g, b, tm=256):
    t, d = x.shape
    row = lambda i: (i, 0)
    const = lambda i: (0, 0)
    return pl.pallas_call(
        _ln_moe_kernel,
        grid=(t // tm,),
        in_specs=[pl.BlockSpec((tm, d), row), pl.BlockSpec((tm,) + SLAB, lambda i: (i, 0, 0)),
                  pl.BlockSpec((tm,) + SLAB, lambda i: (i + t // tm, 0, 0)),
                  pl.BlockSpec((tm, ROUTER_COLS), row),
                  pl.BlockSpec((1, d), const), pl.BlockSpec((1, d), const)],
        out_specs=[pl.BlockSpec((tm, d), row), pl.BlockSpec((tm, d), row)],
        out_shape=[jax.ShapeDtypeStruct((t, d), F32), jax.ShapeDtypeStruct((t, d), BF16)],
        compiler_params=_params("arbitrary"),
        name="ln_moe",
    )(x, y_planes, y_planes, route, g.reshape(1, d), b.reshape(1, d))


def _hgrn_constants():
    c = HG_CHUNK
    t = np.arange(c)[:, None]
    s = np.arange(c)[None, :]
    masks = [(s == t).astype(np.float32)]
    for lvl in range(HG_LEVELS):
        m = 1 << lvl
        masks.append(((t // (2 * m) == s // (2 * m)) & ((t // m) % 2 == 1) & ((s // m) % 2 == 0)).astype(np.float32))
    return (s <= t).astype(np.float32), np.stack(masks, axis=0)


def _rows_broadcast(b, first, period, reps):
    n = b.shape[0] // period
    return jnp.concatenate([jnp.broadcast_to(b[first + j * period:first + j * period + 1, :], (reps, b.shape[1]))
                            for j in range(n) for _ in range(period // reps)], axis=0)


def _level_decay(b, fclip, row, lvl):
    if lvl == 0:
        return jnp.where((row & 1) == 1, fclip, 1.0)
    m = 1 << lvl
    if 2 * m < 8:
        ref = jnp.where((row & m * 2) == 0, _rows_broadcast(b, m - 1, 8, 8), _rows_broadcast(b, 3 * m - 1, 8, 8))
    else:
        ref = _rows_broadcast(b, m - 1, 2 * m, 2 * m)
    return jnp.exp2(-jnp.abs(b - ref))


class _RowBlocks:
    def __init__(self, n):
        self.tiles = [None] * (n // SUBLANES)

    def add(self, first_row, val):
        for u in range(val.shape[0] // SUBLANES):
            piece = val[u * SUBLANES:(u + 1) * SUBLANES]
            i = first_row // SUBLANES + u
            self.tiles[i] = piece if self.tiles[i] is None else self.tiles[i] + piece

    def value(self):
        return jnp.concatenate(self.tiles, axis=0)


def _hgrn_kernel(q_ref, f_ref, i_ref, g_ref, lb_ref, ng_ref, lm_ref, pm_ref, o_ref, st_ref):
    c = HG_CHUNK
    nh = HG_HEADS_PER_STEP

    @pl.when(pl.program_id(2) == 0)
    def _():
        st_ref[...] = jnp.zeros_like(st_ref)

    row = lax.broadcasted_iota(I32, (c, LANES), 0)

    for j in range(nh):
        cols = slice(j * LANES, (j + 1) * LANES)
        q = q_ref[:, cols]
        v = i_ref[:, cols]
        lb = lb_ref[:, cols]
        sig = jax.nn.sigmoid(f_ref[:, cols])
        fclip = jnp.maximum(lb + (1.0 - lb) * sig, MIN_FORGET)
        logf = jnp.log(fclip) * LOG2_E
        k = (1.0 - lb) * (1.0 - sig)

        g_hi = logf.astype(BF16)
        g_lo = (logf - g_hi.astype(F32)).astype(BF16)
        b2 = jnp.dot(lm_ref[...], jnp.concatenate([g_hi, g_lo], axis=1), preferred_element_type=F32)
        b = b2[:, :LANES] + b2[:, LANES:]
        b_last = b[c - 1:c, :]

        st = st_ref[j]
        o = lax.dot_general((q * jnp.exp2(b)).astype(BF16), st.astype(BF16), NT_DIMS, preferred_element_type=F32)

        scores = _RowBlocks(c)
        scores.add(0, pm_ref[0] * lax.dot_general(q.astype(BF16), k.astype(BF16), NT_DIMS, preferred_element_type=F32))
        for lvl in range(HG_LEVELS):
            e = _level_decay(b, fclip, row, lvl)
            m = 1 << lvl
            if m < SUBLANES:
                s_l = lax.dot_general((q * e).astype(BF16), (k * e).astype(BF16), NT_DIMS, preferred_element_type=F32)
                scores.add(0, pm_ref[lvl + 1] * s_l)
            else:
                up = [slice(u * 2 * m + m, (u + 1) * 2 * m) for u in range(c // (2 * m))]
                lo = [slice(u * 2 * m, u * 2 * m + m) for u in range(c // (2 * m))]
                qe = jnp.concatenate([q[r] * e[r] for r in up], axis=0).astype(BF16)
                ke = jnp.concatenate([piece for r in lo for piece in (k[r] * e[r], jnp.zeros((m, LANES), F32))],
                                     axis=0).astype(BF16)
                s_l = lax.dot_general(qe, ke, NT_DIMS, preferred_element_type=F32)
                for u, r in enumerate(up):
                    part = s_l[u * m:(u + 1) * m]
                    scores.add(r.start, part if len(up) == 1 else pm_ref[lvl + 1, r, :] * part)
        o = o + jnp.dot(scores.value().astype(BF16), v.astype(BF16), preferred_element_type=F32)

        ks = (k * jnp.exp2(b_last - b)).astype(BF16)
        st_ref[j] = st * jnp.exp2(b_last) + jnp.dot(v.T.astype(BF16), ks, preferred_element_type=F32)

        o = o * lax.rsqrt(jnp.mean(o * o, -1, keepdims=True) + RMS_EPS) * ng_ref[:, cols]
        gate = g_ref[:, cols]
        o_ref[:, cols] = (o * (gate * jax.nn.sigmoid(gate))).astype(o_ref.dtype)


def _hgrn(proj, lb, norm_g, bsz, seq):
    t = bsz * seq
    nc = seq // HG_CHUNK
    c = HG_CHUNK
    nh = HG_HEADS_PER_STEP
    w = nh * HG_DK
    hsteps = HG_HEADS // nh
    lm, pm = _hgrn_constants()

    def part(p):
        return pl.BlockSpec((c, w), lambda b, h, n: (b * nc + n, p * hsteps + h))

    head = pl.BlockSpec((1, w), lambda b, h, n: (0, h))
    return pl.pallas_call(
        _hgrn_kernel,
        grid=(bsz, hsteps, nc),
        in_specs=[part(0), part(1), part(2), part(3), head, head,
                  pl.BlockSpec((c, c), lambda b, h, n: (0, 0)),
                  pl.BlockSpec((HG_LEVELS + 1, c, c), lambda b, h, n: (0, 0, 0))],
        out_specs=pl.BlockSpec((c, w), lambda b, h, n: (b * nc + n, h)),
        out_shape=jax.ShapeDtypeStruct((t, D_MODEL), BF16),
        scratch_shapes=[pltpu.VMEM((nh, HG_DV, HG_DK), F32)],
        compiler_params=_params("arbitrary", "arbitrary", "arbitrary"),
        name="hgrn2",
    )(proj, proj, proj, proj, lb.reshape(1, D_MODEL), norm_g.reshape(1, D_MODEL),
      jnp.asarray(lm, BF16), jnp.asarray(pm, F32))


def _swa_kernel(sink_ref, q_ref, kvp_ref, kvc_ref, bias_ref, o_ref):
    n = pl.program_id(1)
    w = WINDOW
    hd = ATT_HEAD_DIM
    cols = ATT_GROUP * w
    kj = lax.broadcasted_iota(I32, (2 * w, cols), 0)
    qi = lax.broadcasted_iota(I32, (2 * w, cols), 1) & (w - 1)
    dist = qi + w - kj
    mask = (dist >= 0) & (dist < w) & ((n > 0) | (kj >= w))
    kvw = ATT_KV_HEADS * hd
    ones = jnp.ones((2 * w, hd), F32)
    for g in range(ATT_KV_HEADS):
        kwin = jnp.concatenate([kvp_ref[:, g * hd:(g + 1) * hd], kvc_ref[:, g * hd:(g + 1) * hd]], axis=0)
        vwin = jnp.concatenate([kvp_ref[:, kvw + g * hd:kvw + (g + 1) * hd],
                                kvc_ref[:, kvw + g * hd:kvw + (g + 1) * hd]], axis=0)
        heads = range(g * ATT_GROUP, (g + 1) * ATT_GROUP)
        qg = jnp.concatenate([q_ref[:, h * hd:(h + 1) * hd] for h in heads], axis=0) * (hd ** -0.5)
        sink = jnp.concatenate([jnp.full((1, w), sink_ref[h], F32) for h in heads], axis=1)
        s = lax.dot_general(kwin, qg.astype(BF16), NT_DIMS, preferred_element_type=F32) + bias_ref[g]
        s = jnp.where(mask, s, NEG_BIG)
        m = jnp.maximum(jnp.max(s, 0, keepdims=True), sink)
        p = jnp.exp(s - m).astype(BF16)
        vext_t = jnp.concatenate([vwin.astype(F32), ones], axis=1).T.astype(BF16)
        ov = jnp.dot(vext_t, p, preferred_element_type=F32)
        ov = ov / (ov[hd:hd + 1, :] + jnp.exp(sink - m))
        og = ov.T.astype(o_ref.dtype)
        o_ref[:, g * ATT_GROUP * hd:(g + 1) * ATT_GROUP * hd] = jnp.concatenate(
            [og[j * w:(j + 1) * w, :hd] for j in range(ATT_GROUP)], axis=1)


def _swa(q, kv, bias, sinks, bsz, seq):
    t = bsz * seq
    nb = seq // WINDOW
    kvw2 = 2 * ATT_KV_HEADS * ATT_HEAD_DIM
    return pl.pallas_call(
        _swa_kernel,
        grid=(bsz, nb),
        in_specs=[pl.BlockSpec(memory_space=pltpu.SMEM),
                  pl.BlockSpec((WINDOW, D_MODEL), lambda b, n: (b * nb + n, 0)),
                  pl.BlockSpec((WINDOW, kvw2), lambda b, n: (b * nb + jnp.maximum(n - 1, 0), 0)),
                  pl.BlockSpec((WINDOW, kvw2), lambda b, n: (b * nb + n, 0)),
                  pl.BlockSpec((ATT_KV_HEADS, 2 * WINDOW, ATT_GROUP * WINDOW), lambda b, n: (0, 0, 0))],
        out_specs=pl.BlockSpec((WINDOW, D_MODEL), lambda b, n: (b * nb + n, 0)),
        out_shape=jax.ShapeDtypeStruct((t, D_MODEL), BF16),
        compiler_params=_params("arbitrary", "arbitrary"),
        name="swa",
    )(sinks, q, kv, kv, bias)


def _t5_bucket(dist):
    n = jnp.clip(dist, 0, REL_MAX_DISTANCE - 1)
    max_exact = N_BUCKETS // 2
    large = max_exact + (jnp.log(jnp.maximum(n, max_exact).astype(F32) / max_exact)
                         / math.log(REL_MAX_DISTANCE / max_exact)
                         * (N_BUCKETS - max_exact)).astype(I32)
    large = jnp.minimum(large, N_BUCKETS - 1)
    return jnp.where(n < max_exact, n, large)


def _band_bias_t(rel_bias):
    kj = jnp.arange(2 * WINDOW)[:, None]
    qi = jnp.arange(WINDOW)[None, :]
    bucket = _t5_bucket(qi + WINDOW - kj).reshape(-1)
    onehot = (bucket[:, None] == jnp.arange(N_BUCKETS)[None, :]).astype(F32)
    table = jnp.dot(onehot, rel_bias.astype(F32), precision=lax.Precision.HIGHEST)
    table = table.reshape(2 * WINDOW, WINDOW, ATT_KV_HEADS, ATT_GROUP)
    return table.transpose(2, 0, 3, 1).reshape(ATT_KV_HEADS, 2 * WINDOW, ATT_GROUP * WINDOW)


def _moe_kernel(layer, te_ref, nu_ref, nx_ref, src_ref, dst_ref,
                x_hbm, wg_hbm, wu_hbm, wd_hbm, y_hbm,
                xres, xbuf, ybuf, wgs_ref, wus_ref, wds_ref, wgb_ref, wub_ref, wdb_ref, in_sem, out_sem, w_sem):
    i = pl.program_id(0)
    n_used = nu_ref[0]
    rows = MOE_TILE

    def gather_rows(tile):
        for r in range(rows):
            xbuf[r] = xres[src_ref[tile * rows + r]]

    def start_scatter(tile):
        for r in range(rows):
            pltpu.make_async_copy(ybuf.at[r], y_hbm.at[dst_ref[tile * rows + r]], out_sem.at[0]).start(priority=r % 2)

    def wait_scatter():
        pltpu.make_async_copy(ybuf, ybuf, out_sem.at[0]).wait()

    def weight_copies(e):
        return (pltpu.make_async_copy(wg_hbm.at[layer, e], wgs_ref, w_sem.at[0]),
                pltpu.make_async_copy(wu_hbm.at[layer, e], wus_ref, w_sem.at[1]),
                pltpu.make_async_copy(wd_hbm.at[layer, e], wds_ref, w_sem.at[2]))

    @pl.when(i < n_used)
    def _():
        @pl.when(i == 0)
        def _():
            for cp in weight_copies(te_ref[0]):
                cp.start()
            load_x = pltpu.make_async_copy(x_hbm, xres, in_sem.at[0])
            load_x.start()
            ybuf[...] = jnp.zeros_like(ybuf)
            dump = pltpu.make_async_copy(ybuf, y_hbm.at[pl.ds(y_hbm.shape[0] - rows, rows)], out_sem.at[0])
            dump.start()
            dump.wait()
            load_x.wait()

        @pl.when((i == 0) | (te_ref[i] != te_ref[jnp.maximum(i - 1, 0)]))
        def _():
            for cp in weight_copies(te_ref[i]):
                cp.wait()
            wgb_ref[...] = wgs_ref[...].astype(BF16)
            wub_ref[...] = wus_ref[...].astype(BF16)
            wdb_ref[...] = wds_ref[...].astype(BF16)

            @pl.when(nx_ref[i] >= 0)
            def _():
                for cp in weight_copies(nx_ref[i]):
                    cp.start()

        gather_rows(i)
        x = _unpack_bf16_pairs(xbuf[...]).astype(BF16)
        hg = jnp.dot(x, wgb_ref[...], preferred_element_type=F32)
        hu = jnp.dot(x, wub_ref[...], preferred_element_type=F32)
        hidden = (hg * jax.nn.sigmoid(hg) * hu).astype(BF16)
        y = jnp.dot(hidden, wdb_ref[...], preferred_element_type=F32)

        @pl.when(i > 0)
        def _():
            wait_scatter()

        ybuf[...] = _pack_bf16_pairs(y)
        start_scatter(i)

        @pl.when(i == n_used - 1)
        def _():
            wait_scatter()


def _moe_experts(xp, tile_expert, n_used, next_expert, slot_src, slot_dst, w_gate, w_up, w_down, layer):
    t = xp.shape[0]
    d = D_MODEL
    n_tiles = tile_expert.shape[0]
    any_spec = pl.BlockSpec(memory_space=pl.ANY)
    return pl.pallas_call(
        functools.partial(_moe_kernel, layer),
        grid_spec=pltpu.PrefetchScalarGridSpec(
            num_scalar_prefetch=5,
            grid=(n_tiles,),
            in_specs=[any_spec, any_spec, any_spec, any_spec],
            out_specs=any_spec,
            scratch_shapes=[pltpu.VMEM((t,) + SLAB, U32),
                            pltpu.VMEM((MOE_TILE,) + SLAB, U32), pltpu.VMEM((MOE_TILE,) + SLAB, U32),
                            pltpu.VMEM((d, D_EXPERT), F32), pltpu.VMEM((d, D_EXPERT), F32),
                            pltpu.VMEM((D_EXPERT, d), F32),
                            pltpu.VMEM((d, D_EXPERT), BF16), pltpu.VMEM((d, D_EXPERT), BF16),
                            pltpu.VMEM((D_EXPERT, d), BF16),
                            pltpu.SemaphoreType.DMA((1,)), pltpu.SemaphoreType.DMA((1,)),
                            pltpu.SemaphoreType.DMA((3,))]),
        out_shape=jax.ShapeDtypeStruct((TOP_K * t + MOE_TILE,) + SLAB, U32),
        compiler_params=pltpu.CompilerParams(dimension_semantics=("arbitrary",), vmem_limit_bytes=MOE_VMEM_LIMIT),
        name="moe_experts",
    )(tile_expert, n_used, next_expert, slot_src, slot_dst, xp, w_gate, w_up, w_down)


def _hier_moe(xp, route, counts, w_gate, w_up, w_down, layer):
    n_tok = xp.shape[0]
    n_asg = n_tok * TOP_K
    e_flat = route[:, :TOP_K].astype(I32).reshape(-1)
    order = jnp.argsort(e_flat).astype(I32)
    counts = counts[0, :N_EXPERTS].astype(I32)
    starts = jnp.cumsum(counts) - counts
    padded = (counts + MOE_TILE - 1) // MOE_TILE * MOE_TILE
    pad_ends = jnp.cumsum(padded)
    pad_starts = pad_ends - padded
    n_tiles = -(-n_asg // MOE_TILE) + N_EXPERTS
    n_used = pad_ends[-1] // MOE_TILE

    tile_start = jnp.arange(n_tiles, dtype=I32) * MOE_TILE
    used = tile_start < pad_ends[-1]
    tile_expert = jnp.minimum(jnp.sum((pad_ends[None, :] <= tile_start[:, None]).astype(I32), 1), N_EXPERTS - 1)
    next_tile = pad_ends[tile_expert] // MOE_TILE
    next_expert = jnp.where(used & (next_tile < n_used), tile_expert[jnp.minimum(next_tile, n_tiles - 1)], -1)
    rank0 = tile_start - pad_starts[tile_expert]
    n_valid = jnp.where(used, jnp.clip(counts[tile_expert] - rank0, 0, MOE_TILE), 0)
    sorted0 = starts[tile_expert] + rank0

    r = jnp.arange(MOE_TILE, dtype=I32)[None, :]
    valid = r < n_valid[:, None]
    asg = order[jnp.clip(sorted0[:, None] + r, 0, n_asg - 1)]
    slot_src = jnp.where(valid, asg // TOP_K, 0).reshape(-1)
    slot_dst = jnp.where(valid, (asg % TOP_K) * n_tok + asg // TOP_K, TOP_K * n_tok + r).reshape(-1)
    return _moe_experts(xp, tile_expert, n_used.reshape(1).astype(I32), next_expert.astype(I32),
                        slot_src, slot_dst, w_gate, w_up, w_down, layer)


def kernel(x, a_w_in, a_lower_bound, a_norm_g, a_w_out, b_w_kv, b_w_q, b_sinks, b_w_out, rel_bias,
           moe_w_rg, moe_b_rg, moe_w_re, moe_b_re, moe_w_gate, moe_w_up, moe_w_down, ln_g, ln_b):
    bsz, seq, d = x.shape
    t = bsz * seq
    lb_sm = jax.nn.softmax(a_lower_bound.astype(F32), axis=0)
    lower_bounds = jnp.cumsum(lb_sm, axis=0) - lb_sm[0]
    att_bias = _band_bias_t(rel_bias)
    pad = ROUTER_COLS - N_GROUPS - N_EXPERTS
    w_router = jnp.concatenate([moe_w_rg, moe_w_re, jnp.zeros((DEPTH, d, pad), F32)], axis=-1)
    w_router_hi = w_router.astype(BF16)
    w_router_lo = (w_router - w_router_hi.astype(F32)).astype(BF16)
    w_router = jnp.concatenate([w_router_hi, w_router_lo], axis=-1)
    b_router = jnp.concatenate([moe_b_rg, moe_b_re, jnp.zeros((DEPTH, pad), F32)], axis=-1)

    xf = x.reshape(t, d).astype(F32)
    xb = xf.astype(BF16)
    kv = None
    for layer in range(DEPTH):
        if layer < N_A_LAYERS:
            proj = _matmul(xb, a_w_in, layer, F32)
            o = _hgrn(proj, lower_bounds[layer], a_norm_g[layer], bsz, seq)
            w_out, w_layer = a_w_out, layer
        else:
            j = layer - N_A_LAYERS
            if kv is None:
                kv = _matmul(xb, b_w_kv[None], 0, BF16)
            q = _matmul(xb, b_w_q, j, BF16)
            o = _swa(q, kv, att_bias, b_sinks[j].astype(F32), bsz, seq)
            w_out, w_layer = b_w_out, j
        xf, xp, route, counts = _out_ln_router(o, w_out, w_layer, xf, ln_g[2 * layer], ln_b[2 * layer],
                                               w_router[layer], b_router[layer].reshape(1, ROUTER_COLS))
        y_planes = _hier_moe(xp, route, counts, moe_w_gate, moe_w_up, moe_w_down, layer)
        xf, xb = _ln_moe(xf, y_planes, route, ln_g[2 * layer + 1], ln_b[2 * layer + 1])
    return xf.reshape(bsz, seq, d).astype(x.dtype)
```

```python
import functools
import math

import numpy as np
import jax
import jax.numpy as jnp
from jax import lax
from jax.experimental import pallas as pl
from jax.experimental.pallas import tpu as pltpu

F32 = jnp.float32
BF16 = jnp.bfloat16
U32 = jnp.uint32
I32 = jnp.int32

D_MODEL = 2048
DEPTH = 4
N_A_LAYERS = DEPTH // 2
HG_HEADS = 16
HG_DK = 128
HG_DV = 128
ATT_HEAD_DIM = 64
ATT_Q_HEADS = 32
ATT_KV_HEADS = 4
ATT_GROUP = ATT_Q_HEADS // ATT_KV_HEADS
WINDOW = 128
N_BUCKETS = 32
REL_MAX_DISTANCE = 128
N_GROUPS = 4
EXPERTS_PER_GROUP = 8
N_EXPERTS = N_GROUPS * EXPERTS_PER_GROUP
TOP_K = 2
D_EXPERT = D_MODEL // 4
DEEPNORM_ALPHA = (2 * DEPTH) ** 0.25
LN_EPS = 1e-5
RMS_EPS = 1e-6
NEG_BIG = -1e30
MIN_FORGET = 1e-30
LOG2_E = 1.0 / math.log(2.0)

LANES = 128
HG_CHUNK = 128
HG_LEVELS = 7
HG_W_PIECE = 512
HG_HEAD_GROUP = 4
HG_GROUP_COLS = HG_HEAD_GROUP * 128
MOE_TILE = 256
OUT_LN_CHUNK = 512
ROUTER_COLS = 128
SUBLANES = 8
SLAB = (SUBLANES, LANES)
assert D_MODEL // 2 == SUBLANES * LANES
VMEM_LIMIT = 56 * 1024 * 1024
BIG_VMEM_LIMIT = 60 * 1024 * 1024
NT_DIMS = (((1,), (1,)), ((), ()))


def _params(*sem):
    return pltpu.CompilerParams(dimension_semantics=sem, vmem_limit_bytes=VMEM_LIMIT)


def _pack_bf16_pairs(y):
    half = y.shape[1] // 2
    lo = lax.bitcast_convert_type(y[:, :half].astype(BF16).astype(F32), U32)
    hi = lax.bitcast_convert_type(y[:, half:].astype(BF16).astype(F32), U32)
    w = (hi & jnp.uint32(0xFFFF0000)) | (lo >> 16)
    return jnp.swapaxes(jnp.stack([w[:, s * LANES:(s + 1) * LANES] for s in range(SUBLANES)], axis=0), 0, 1)


def _unpack_bf16_pairs(slabs):
    cols = jnp.swapaxes(slabs, 0, 1)
    w = jnp.concatenate([cols[s] for s in range(SUBLANES)], axis=1)
    lo = lax.bitcast_convert_type(w << 16, F32)
    hi = lax.bitcast_convert_type(w & jnp.uint32(0xFFFF0000), F32)
    return jnp.concatenate([lo, hi], axis=1)


def _mm_kernel(x_ref, w_ref, o_ref, wb_ref):
    @pl.when(pl.program_id(1) == 0)
    def _():
        wb_ref[...] = w_ref[...].astype(BF16)

    o_ref[...] = jnp.dot(x_ref[...], wb_ref[...], preferred_element_type=F32).astype(o_ref.dtype)


def _matmul(x, w, layer, out_dtype, tm=1024, tn=1024):
    m, k = x.shape
    n = w.shape[2]
    tm, tn = min(tm, m), min(tn, n)
    return pl.pallas_call(
        _mm_kernel,
        grid=(n // tn, m // tm),
        in_specs=[pl.BlockSpec((tm, k), lambda j, i: (i, 0)),
                  pl.BlockSpec((None, k, tn), lambda j, i: (layer, 0, j))],
        out_specs=pl.BlockSpec((tm, tn), lambda j, i: (i, j)),
        out_shape=jax.ShapeDtypeStruct((m, n), out_dtype),
        scratch_shapes=[pltpu.VMEM((k, tn), BF16)],
        compiler_params=_params("arbitrary", "arbitrary"),
        name="matmul",
    )(x, w)


def _layernorm(v, g, b):
    mu = jnp.mean(v, -1, keepdims=True)
    d = v - mu
    var = jnp.mean(d * d, -1, keepdims=True)
    return d * lax.rsqrt(var + LN_EPS) * g + b


def _route(logits):
    lane = lax.broadcasted_iota(I32, logits.shape, 1)
    big = jnp.int32(ROUTER_COLS)
    is_grp = lane < N_GROUPS
    gl = jnp.where(is_grp, logits, -jnp.inf)
    g_max = jnp.max(gl, -1, keepdims=True)
    grp = jnp.min(jnp.where(is_grp & (gl == g_max), lane, big), -1, keepdims=True)
    p_grp = 1.0 / jnp.sum(jnp.exp(gl - g_max), -1, keepdims=True)
    in_grp = (lane >= N_GROUPS) & (((lane - N_GROUPS) >> 3) == grp)
    el = jnp.where(in_grp, logits, -jnp.inf)
    v1 = jnp.max(el, -1, keepdims=True)
    i1 = jnp.min(jnp.where(in_grp & (el == v1), lane, big), -1, keepdims=True)
    el2 = jnp.where(lane == i1, -jnp.inf, el)
    v2 = jnp.max(el2, -1, keepdims=True)
    i2 = jnp.min(jnp.where(in_grp & (lane != i1) & (el2 == v2), lane, big), -1, keepdims=True)
    ex = jnp.exp(v2 - v1)
    w1 = 1.0 / (1.0 + ex)
    return i1 - N_GROUPS, i2 - N_GROUPS, p_grp * w1, p_grp * (ex * w1)


def _out_ln_router_kernel(layer, o_ref, w_hbm, x_ref, g_ref, b_ref, wr_ref, br_ref,
                          xo_ref, xp_ref, rt_ref, cnt_ref, ws_ref, wb_ref, w_sem):
    @pl.when(pl.program_id(0) == 0)
    def _():
        load_w = pltpu.make_async_copy(w_hbm.at[layer], ws_ref, w_sem.at[0])
        load_w.start()
        cnt_ref[...] = jnp.zeros_like(cnt_ref)
        load_w.wait()
        wb_ref[...] = ws_ref[...].astype(BF16)

    tm = o_ref.shape[0]
    chunk = min(OUT_LN_CHUNK, tm)
    for r0 in range(0, tm, chunk):
        rows = slice(r0, r0 + chunk)
        h = jnp.dot(o_ref[rows, :], wb_ref[...], preferred_element_type=F32)
        y = _layernorm(DEEPNORM_ALPHA * x_ref[rows, :] + h, g_ref[...], b_ref[...])
        xo_ref[rows, :] = y
        xp_ref[rows] = _pack_bf16_pairs(y)
        y_hi = y.astype(BF16)
        y_lo = (y - y_hi.astype(F32)).astype(BF16)
        t_hi = jnp.dot(y_hi, wr_ref[...], preferred_element_type=F32)
        t_lo = jnp.dot(y_lo, wr_ref[:, :ROUTER_COLS], preferred_element_type=F32)
        logits = t_hi[:, :ROUTER_COLS] + t_hi[:, ROUTER_COLS:] + t_lo + br_ref[...]
        e1, e2, g1, g2 = _route(logits)
        lane = lax.broadcasted_iota(I32, logits.shape, 1)
        rt_ref[rows, :] = jnp.where(lane == 0, e1.astype(F32),
                                    jnp.where(lane == 1, e2.astype(F32),
                                              jnp.where(lane == 2, g1, jnp.where(lane == 3, g2, 0.0))))
        hits = (lane == e1).astype(F32) + (lane == e2).astype(F32)
        cnt_ref[...] += jnp.sum(hits, 0, keepdims=True)


def _out_ln_router(o, w, layer, x, g, b, wr, br, tm=512):
    t, d = x.shape
    tm = min(tm, t)
    row = lambda i: (i, 0)
    const = lambda i: (0, 0)
    return pl.pallas_call(
        functools.partial(_out_ln_router_kernel, layer),
        grid=(t // tm,),
        in_specs=[pl.BlockSpec((tm, d), row), pl.BlockSpec(memory_space=pl.ANY), pl.BlockSpec((tm, d), row),
                  pl.BlockSpec((1, d), const), pl.BlockSpec((1, d), const),
                  pl.BlockSpec((d, 2 * ROUTER_COLS), const), pl.BlockSpec((1, ROUTER_COLS), const)],
        out_specs=[pl.BlockSpec((tm, d), row), pl.BlockSpec((tm,) + SLAB, lambda i: (i, 0, 0)),
                   pl.BlockSpec((tm, ROUTER_COLS), row), pl.BlockSpec((1, ROUTER_COLS), const)],
        out_shape=[jax.ShapeDtypeStruct((t, d), F32), jax.ShapeDtypeStruct((t,) + SLAB, U32),
                   jax.ShapeDtypeStruct((t, ROUTER_COLS), F32), jax.ShapeDtypeStruct((1, ROUTER_COLS), F32)],
        scratch_shapes=[pltpu.VMEM((d, d), F32), pltpu.VMEM((d, d), BF16), pltpu.SemaphoreType.DMA((1,))],
        compiler_params=_params("arbitrary"),
        name="out_ln_router",
    )(o, w, x, g.reshape(1, d), b.reshape(1, d), wr, br)


def _ln_moe_kernel(x_ref, y0_ref, y1_ref, rt_ref, g_ref, b_ref, xo_ref, xb_ref):
    rt = rt_ref[...]
    f = rt[:, 2:3] * _unpack_bf16_pairs(y0_ref[...]) + rt[:, 3:4] * _unpack_bf16_pairs(y1_ref[...])
    y = _layernorm(DEEPNORM_ALPHA * x_ref[...] + f, g_ref[...], b_ref[...])
    xo_ref[...] = y
    xb_ref[...] = y.astype(BF16)


def _ln_moe(x, y_planes, route, g, b, tm=256):
    t, d = x.shape
    row = lambda i: (i, 0)
    const = lambda i: (0, 0)
    return pl.pallas_call(
        _ln_moe_kernel,
        grid=(t // tm,),
        in_specs=[pl.BlockSpec((tm, d), row), pl.BlockSpec((tm,) + SLAB, lambda i: (i, 0, 0)),
                  pl.BlockSpec((tm,) + SLAB, lambda i: (i + t // tm, 0, 0)),
                  pl.BlockSpec((tm, ROUTER_COLS), row),
                  pl.BlockSpec((1, d), const), pl.BlockSpec((1, d), const)],
        out_specs=[pl.BlockSpec((tm, d), row), pl.BlockSpec((tm, d), row)],
        out_shape=[jax.ShapeDtypeStruct((t, d), F32), jax.ShapeDtypeStruct((t, d), BF16)],
        compiler_params=_params("arbitrary"),
        name="ln_moe",
    )(x, y_planes, y_planes, route, g.reshape(1, d), b.reshape(1, d))


def _hgrn_constants():
    c = HG_CHUNK
    t = np.arange(c)[:, None]
    s = np.arange(c)[None, :]
    masks = [(s == t).astype(np.float32)]
    for lvl in range(HG_LEVELS):
        m = 1 << lvl
        masks.append(((t // (2 * m) == s // (2 * m)) & ((t // m) % 2 == 1) & ((s // m) % 2 == 0)).astype(np.float32))
    return (s <= t).astype(np.float32), np.stack(masks, axis=0)


def _rows_broadcast(b, first, period, reps):
    n = b.shape[0] // period
    return jnp.concatenate([jnp.broadcast_to(b[first + j * period:first + j * period + 1, :], (reps, b.shape[1]))
                            for j in range(n) for _ in range(period // reps)], axis=0)


def _level_decay(b, fclip, row, lvl):
    if lvl == 0:
        return jnp.where((row & 1) == 1, fclip, 1.0)
    m = 1 << lvl
    if 2 * m < 8:
        ref = jnp.where((row & m * 2) == 0, _rows_broadcast(b, m - 1, 8, 8), _rows_broadcast(b, 3 * m - 1, 8, 8))
    else:
        ref = _rows_broadcast(b, m - 1, 2 * m, 2 * m)
    return jnp.exp2(-jnp.abs(b - ref))


class _RowBlocks:
    def __init__(self, n):
        self.tiles = [None] * (n // SUBLANES)

    def add(self, first_row, val):
        for u in range(val.shape[0] // SUBLANES):
            piece = val[u * SUBLANES:(u + 1) * SUBLANES]
            i = first_row // SUBLANES + u
            self.tiles[i] = piece if self.tiles[i] is None else self.tiles[i] + piece

    def value(self):
        return jnp.concatenate(self.tiles, axis=0)


def _hgrn_kernel(layer, x_ref, xn_ref, w_hbm, lb_ref, ng_ref, lm_ref, pm_ref, o_ref,
                 wb_ref, ws_ref, st_ref, w_sem, *proj_refs):
    c = HG_CHUNK
    d = D_MODEL
    first = (pl.program_id(0) == 0) & (pl.program_id(1) == 0)

    n_grp = HG_HEADS // HG_HEAD_GROUP

    def project(src_ref, grp, part):
        cols = slice(part * d + grp * HG_GROUP_COLS, part * d + (grp + 1) * HG_GROUP_COLS)
        proj_refs[grp][:, part * HG_GROUP_COLS:(part + 1) * HG_GROUP_COLS] = jnp.dot(
            src_ref[...], wb_ref[:, cols], preferred_element_type=F32)

    def head_part(j, part):
        first_col = part * HG_GROUP_COLS + (j % HG_HEAD_GROUP) * LANES
        return proj_refs[j // HG_HEAD_GROUP][:, first_col:first_col + LANES]

    @pl.when(first)
    def _():
        def piece(p):
            return pltpu.make_async_copy(w_hbm.at[layer, :, pl.ds(p * HG_W_PIECE, HG_W_PIECE)],
                                         ws_ref.at[p % 2], w_sem.at[p % 2])
        n_piece = 4 * d // HG_W_PIECE
        piece(0).start()
        for p in range(n_piece):
            if p + 1 < n_piece:
                piece(p + 1).start()
            piece(p).wait()
            wb_ref[:, p * HG_W_PIECE:(p + 1) * HG_W_PIECE] = ws_ref[p % 2].astype(BF16)
        for grp in range(n_grp - 1):
            for part in range(4):
                project(x_ref, grp, part)

    @pl.when(pl.program_id(1) == 0)
    def _():
        st_ref[...] = jnp.zeros_like(st_ref)

    row = lax.broadcasted_iota(I32, (c, LANES), 0)

    for j in range(HG_HEADS):
        cols = slice(j * LANES, (j + 1) * LANES)
        q = head_part(j, 0)
        v = head_part(j, 2)
        lb = lb_ref[:, cols]
        sig = jax.nn.sigmoid(head_part(j, 1))
        fclip = jnp.maximum(lb + (1.0 - lb) * sig, MIN_FORGET)
        logf = jnp.log(fclip) * LOG2_E
        k = (1.0 - lb) * (1.0 - sig)

        g_hi = logf.astype(BF16)
        g_lo = (logf - g_hi.astype(F32)).astype(BF16)
        b2 = jnp.dot(lm_ref[...], jnp.concatenate([g_hi, g_lo], axis=1), preferred_element_type=F32)
        b = b2[:, :LANES] + b2[:, LANES:]
        b_last = b[c - 1:c, :]

        st = st_ref[j]
        o = lax.dot_general((q * jnp.exp2(b)).astype(BF16), st.astype(BF16), NT_DIMS, preferred_element_type=F32)

        scores = _RowBlocks(c)
        scores.add(0, pm_ref[0] * lax.dot_general(q.astype(BF16), k.astype(BF16), NT_DIMS, preferred_element_type=F32))
        for lvl in range(HG_LEVELS):
            e = _level_decay(b, fclip, row, lvl)
            m = 1 << lvl
            if m < SUBLANES:
                s_l = lax.dot_general((q * e).astype(BF16), (k * e).astype(BF16), NT_DIMS, preferred_element_type=F32)
                scores.add(0, pm_ref[lvl + 1] * s_l)
            else:
                up = [slice(u * 2 * m + m, (u + 1) * 2 * m) for u in range(c // (2 * m))]
                lo = [slice(u * 2 * m, u * 2 * m + m) for u in range(c // (2 * m))]
                qe = jnp.concatenate([q[r] * e[r] for r in up], axis=0).astype(BF16)
                ke = jnp.concatenate([piece for r in lo for piece in (k[r] * e[r], jnp.zeros((m, LANES), F32))],
                                     axis=0).astype(BF16)
                s_l = lax.dot_general(qe, ke, NT_DIMS, preferred_element_type=F32)
                for u, r in enumerate(up):
                    part = s_l[u * m:(u + 1) * m]
                    scores.add(r.start, part if len(up) == 1 else pm_ref[lvl + 1, r, :] * part)
        o = o + jnp.dot(scores.value().astype(BF16), v.astype(BF16), preferred_element_type=F32)

        ks = (k * jnp.exp2(b_last - b)).astype(BF16)
        st_ref[j] = st * jnp.exp2(b_last) + jnp.dot(v.T.astype(BF16), ks, preferred_element_type=F32)

        o = o * lax.rsqrt(jnp.mean(o * o, -1, keepdims=True) + RMS_EPS) * ng_ref[:, cols]
        gate = head_part(j, 3)
        o_ref[:, cols] = (o * (gate * jax.nn.sigmoid(gate))).astype(o_ref.dtype)

        grp, part = divmod(j, HG_HEAD_GROUP)
        if grp == 0:
            project(x_ref, n_grp - 1, part)
        else:
            project(xn_ref, grp - 1, part)


def _hgrn(xb, w_in, layer, lb, norm_g, bsz, seq):
    t, d = xb.shape
    nc = seq // HG_CHUNK
    c = HG_CHUNK
    lm, pm = _hgrn_constants()
    chunk = pl.BlockSpec((c, d), lambda b, n: (b * nc + n, 0))
    next_chunk = pl.BlockSpec((c, d), lambda b, n: (jnp.minimum(b * nc + n + 1, bsz * nc - 1), 0))
    const2 = lambda b, n: (0, 0)
    return pl.pallas_call(
        functools.partial(_hgrn_kernel, layer),
        grid=(bsz, nc),
        in_specs=[chunk, next_chunk, pl.BlockSpec(memory_space=pl.ANY),
                  pl.BlockSpec((1, d), const2), pl.BlockSpec((1, d), const2),
                  pl.BlockSpec((c, c), const2),
                  pl.BlockSpec((HG_LEVELS + 1, c, c), lambda b, n: (0, 0, 0))],
        out_specs=chunk,
        out_shape=jax.ShapeDtypeStruct((t, d), BF16),
        scratch_shapes=[pltpu.VMEM((d, 4 * d), BF16), pltpu.VMEM((2, d, HG_W_PIECE), F32),
                        pltpu.VMEM((HG_HEADS, HG_DV, HG_DK), F32), pltpu.SemaphoreType.DMA((2,))]
                       + [pltpu.VMEM((c, 4 * HG_GROUP_COLS), F32)] * (HG_HEADS // HG_HEAD_GROUP),
        compiler_params=pltpu.CompilerParams(dimension_semantics=("arbitrary", "arbitrary"),
                                             vmem_limit_bytes=BIG_VMEM_LIMIT),
        name="hgrn2",
    )(xb, xb, w_in, lb.reshape(1, d), norm_g.reshape(1, d), jnp.asarray(lm, BF16), jnp.asarray(pm, F32))


def _swa_kernel(sink_ref, q_ref, kvp_ref, kvc_ref, bias_ref, o_ref):
    n = pl.program_id(1)
    w = WINDOW
    hd = ATT_HEAD_DIM
    cols = ATT_GROUP * w
    kj = lax.broadcasted_iota(I32, (2 * w, cols), 0)
    qi = lax.broadcasted_iota(I32, (2 * w, cols), 1) & (w - 1)
    dist = qi + w - kj
    mask = (dist >= 0) & (dist < w) & ((n > 0) | (kj >= w))
    kvw = ATT_KV_HEADS * hd
    ones = jnp.ones((2 * w, hd), F32)
    for g in range(ATT_KV_HEADS):
        kwin = jnp.concatenate([kvp_ref[:, g * hd:(g + 1) * hd], kvc_ref[:, g * hd:(g + 1) * hd]], axis=0)
        vwin = jnp.concatenate([kvp_ref[:, kvw + g * hd:kvw + (g + 1) * hd],
                                kvc_ref[:, kvw + g * hd:kvw + (g + 1) * hd]], axis=0)
        heads = range(g * ATT_GROUP, (g + 1) * ATT_GROUP)
        qg = jnp.concatenate([q_ref[:, h * hd:(h + 1) * hd] for h in heads], axis=0) * (hd ** -0.5)
        sink = jnp.concatenate([jnp.full((1, w), sink_ref[h], F32) for h in heads], axis=1)
        s = lax.dot_general(kwin, qg.astype(BF16), NT_DIMS, preferred_element_type=F32) + bias_ref[g]
        s = jnp.where(mask, s, NEG_BIG)
        m = jnp.maximum(jnp.max(s, 0, keepdims=True), sink)
        p = jnp.exp(s - m).astype(BF16)
        vext_t = jnp.concatenate([vwin.astype(F32), ones], axis=1).T.astype(BF16)
        ov = jnp.dot(vext_t, p, preferred_element_type=F32)
        ov = ov / (ov[hd:hd + 1, :] + jnp.exp(sink - m))
        og = ov.T.astype(o_ref.dtype)
        o_ref[:, g * ATT_GROUP * hd:(g + 1) * ATT_GROUP * hd] = jnp.concatenate(
            [og[j * w:(j + 1) * w, :hd] for j in range(ATT_GROUP)], axis=1)


def _swa(q, kv, bias, sinks, bsz, seq):
    t = bsz * seq
    nb = seq // WINDOW
    kvw2 = 2 * ATT_KV_HEADS * ATT_HEAD_DIM
    return pl.pallas_call(
        _swa_kernel,
        grid=(bsz, nb),
        in_specs=[pl.BlockSpec(memory_space=pltpu.SMEM),
                  pl.BlockSpec((WINDOW, D_MODEL), lambda b, n: (b * nb + n, 0)),
                  pl.BlockSpec((WINDOW, kvw2), lambda b, n: (b * nb + jnp.maximum(n - 1, 0), 0)),
                  pl.BlockSpec((WINDOW, kvw2), lambda b, n: (b * nb + n, 0)),
                  pl.BlockSpec((ATT_KV_HEADS, 2 * WINDOW, ATT_GROUP * WINDOW), lambda b, n: (0, 0, 0))],
        out_specs=pl.BlockSpec((WINDOW, D_MODEL), lambda b, n: (b * nb + n, 0)),
        out_shape=jax.ShapeDtypeStruct((t, D_MODEL), BF16),
        compiler_params=_params("arbitrary", "arbitrary"),
        name="swa",
    )(sinks, q, kv, kv, bias)


def _t5_bucket(dist):
    n = jnp.clip(dist, 0, REL_MAX_DISTANCE - 1)
    max_exact = N_BUCKETS // 2
    large = max_exact + (jnp.log(jnp.maximum(n, max_exact).astype(F32) / max_exact)
                         / math.log(REL_MAX_DISTANCE / max_exact)
                         * (N_BUCKETS - max_exact)).astype(I32)
    large = jnp.minimum(large, N_BUCKETS - 1)
    return jnp.where(n < max_exact, n, large)


def _band_bias_t(rel_bias):
    kj = jnp.arange(2 * WINDOW)[:, None]
    qi = jnp.arange(WINDOW)[None, :]
    bucket = _t5_bucket(qi + WINDOW - kj).reshape(-1)
    onehot = (bucket[:, None] == jnp.arange(N_BUCKETS)[None, :]).astype(F32)
    table = jnp.dot(onehot, rel_bias.astype(F32), precision=lax.Precision.HIGHEST)
    table = table.reshape(2 * WINDOW, WINDOW, ATT_KV_HEADS, ATT_GROUP)
    return table.transpose(2, 0, 3, 1).reshape(ATT_KV_HEADS, 2 * WINDOW, ATT_GROUP * WINDOW)


def _moe_kernel(layer, te_ref, nu_ref, nx_ref, src_ref, dst_ref,
                x_hbm, wg_hbm, wu_hbm, wd_hbm, y_hbm,
                xres, xbuf, ybuf, wgs_ref, wus_ref, wds_ref, wgb_ref, wub_ref, wdb_ref, in_sem, out_sem, w_sem):
    i = pl.program_id(0)
    n_used = nu_ref[0]
    rows = MOE_TILE

    def gather_rows(tile):
        for r in range(rows):
            xbuf[r] = xres[src_ref[tile * rows + r]]

    def start_scatter(tile):
        for r in range(rows):
            pltpu.make_async_copy(ybuf.at[r], y_hbm.at[dst_ref[tile * rows + r]], out_sem.at[0]).start(priority=r % 2)

    def wait_scatter():
        pltpu.make_async_copy(ybuf, ybuf, out_sem.at[0]).wait()

    def weight_copies(e):
        return (pltpu.make_async_copy(wg_hbm.at[layer, e], wgs_ref, w_sem.at[0]),
                pltpu.make_async_copy(wu_hbm.at[layer, e], wus_ref, w_sem.at[1]),
                pltpu.make_async_copy(wd_hbm.at[layer, e], wds_ref, w_sem.at[2]))

    @pl.when(i < n_used)
    def _():
        @pl.when(i == 0)
        def _():
            for cp in weight_copies(te_ref[0]):
                cp.start()
            load_x = pltpu.make_async_copy(x_hbm, xres, in_sem.at[0])
            load_x.start()
            ybuf[...] = jnp.zeros_like(ybuf)
            dump = pltpu.make_async_copy(ybuf, y_hbm.at[pl.ds(y_hbm.shape[0] - rows, rows)], out_sem.at[0])
            dump.start()
            dump.wait()
            load_x.wait()

        @pl.when((i == 0) | (te_ref[i] != te_ref[jnp.maximum(i - 1, 0)]))
        def _():
            for cp in weight_copies(te_ref[i]):
                cp.wait()
            wgb_ref[...] = wgs_ref[...].astype(BF16)
            wub_ref[...] = wus_ref[...].astype(BF16)
            wdb_ref[...] = wds_ref[...].astype(BF16)

            @pl.when(nx_ref[i] >= 0)
            def _():
                for cp in weight_copies(nx_ref[i]):
                    cp.start()

        gather_rows(i)
        x = _unpack_bf16_pairs(xbuf[...]).astype(BF16)
        hg = jnp.dot(x, wgb_ref[...], preferred_element_type=F32)
        hu = jnp.dot(x, wub_ref[...], preferred_element_type=F32)
        hidden = (hg * jax.nn.sigmoid(hg) * hu).astype(BF16)
        y = jnp.dot(hidden, wdb_ref[...], preferred_element_type=F32)

        @pl.when(i > 0)
        def _():
            wait_scatter()

        ybuf[...] = _pack_bf16_pairs(y)
        start_scatter(i)

        @pl.when(i == n_used - 1)
        def _():
            wait_scatter()


def _moe_experts(xp, tile_expert, n_used, next_expert, slot_src, slot_dst, w_gate, w_up, w_down, layer):
    t = xp.shape[0]
    d = D_MODEL
    n_tiles = tile_expert.shape[0]
    any_spec = pl.BlockSpec(memory_space=pl.ANY)
    return pl.pallas_call(
        functools.partial(_moe_kernel, layer),
        grid_spec=pltpu.PrefetchScalarGridSpec(
            num_scalar_prefetch=5,
            grid=(n_tiles,),
            in_specs=[any_spec, any_spec, any_spec, any_spec],
            out_specs=any_spec,
            scratch_shapes=[pltpu.VMEM((t,) + SLAB, U32),
                            pltpu.VMEM((MOE_TILE,) + SLAB, U32), pltpu.VMEM((MOE_TILE,) + SLAB, U32),
                            pltpu.VMEM((d, D_EXPERT), F32), pltpu.VMEM((d, D_EXPERT), F32),
                            pltpu.VMEM((D_EXPERT, d), F32),
                            pltpu.VMEM((d, D_EXPERT), BF16), pltpu.VMEM((d, D_EXPERT), BF16),
                            pltpu.VMEM((D_EXPERT, d), BF16),
                            pltpu.SemaphoreType.DMA((1,)), pltpu.SemaphoreType.DMA((1,)),
                            pltpu.SemaphoreType.DMA((3,))]),
        out_shape=jax.ShapeDtypeStruct((TOP_K * t + MOE_TILE,) + SLAB, U32),
        compiler_params=pltpu.CompilerParams(dimension_semantics=("arbitrary",), vmem_limit_bytes=BIG_VMEM_LIMIT),
        name="moe_experts",
    )(tile_expert, n_used, next_expert, slot_src, slot_dst, xp, w_gate, w_up, w_down)


def _hier_moe(xp, route, counts, w_gate, w_up, w_down, layer):
    n_tok = xp.shape[0]
    n_asg = n_tok * TOP_K
    e_flat = route[:, :TOP_K].astype(I32).reshape(-1)
    order = jnp.argsort(e_flat).astype(I32)
    counts = counts[0, :N_EXPERTS].astype(I32)
    starts = jnp.cumsum(counts) - counts
    padded = (counts + MOE_TILE - 1) // MOE_TILE * MOE_TILE
    pad_ends = jnp.cumsum(padded)
    pad_starts = pad_ends - padded
    n_tiles = -(-n_asg // MOE_TILE) + N_EXPERTS
    n_used = pad_ends[-1] // MOE_TILE

    tile_start = jnp.arange(n_tiles, dtype=I32) * MOE_TILE
    used = tile_start < pad_ends[-1]
    tile_expert = jnp.minimum(jnp.sum((pad_ends[None, :] <= tile_start[:, None]).astype(I32), 1), N_EXPERTS - 1)
    next_tile = pad_ends[tile_expert] // MOE_TILE
    next_expert = jnp.where(used & (next_tile < n_used), tile_expert[jnp.minimum(next_tile, n_tiles - 1)], -1)
    rank0 = tile_start - pad_starts[tile_expert]
    n_valid = jnp.where(used, jnp.clip(counts[tile_expert] - rank0, 0, MOE_TILE), 0)
    sorted0 = starts[tile_expert] + rank0

    r = jnp.arange(MOE_TILE, dtype=I32)[None, :]
    valid = r < n_valid[:, None]
    asg = order[jnp.clip(sorted0[:, None] + r, 0, n_asg - 1)]
    slot_src = jnp.where(valid, asg // TOP_K, 0).reshape(-1)
    slot_dst = jnp.where(valid, (asg % TOP_K) * n_tok + asg // TOP_K, TOP_K * n_tok + r).reshape(-1)
    return _moe_experts(xp, tile_expert, n_used.reshape(1).astype(I32), next_expert.astype(I32),
                        slot_src, slot_dst, w_gate, w_up, w_down, layer)


def kernel(x, a_w_in, a_lower_bound, a_norm_g, a_w_out, b_w_kv, b_w_q, b_sinks, b_w_out, rel_bias,
           moe_w_rg, moe_b_rg, moe_w_re, moe_b_re, moe_w_gate, moe_w_up, moe_w_down, ln_g, ln_b):
    bsz, seq, d = x.shape
    t = bsz * seq
    lb_sm = jax.nn.softmax(a_lower_bound.astype(F32), axis=0)
    lower_bounds = jnp.cumsum(lb_sm, axis=0) - lb_sm[0]
    att_bias = _band_bias_t(rel_bias)
    pad = ROUTER_COLS - N_GROUPS - N_EXPERTS
    w_router = jnp.concatenate([moe_w_rg, moe_w_re, jnp.zeros((DEPTH, d, pad), F32)], axis=-1)
    w_router_hi = w_router.astype(BF16)
    w_router_lo = (w_router - w_router_hi.astype(F32)).astype(BF16)
    w_router = jnp.concatenate([w_router_hi, w_router_lo], axis=-1)
    b_router = jnp.concatenate([moe_b_rg, moe_b_re, jnp.zeros((DEPTH, pad), F32)], axis=-1)

    xf = x.reshape(t, d).astype(F32)
    xb = xf.astype(BF16)
    kv = None
    for layer in range(DEPTH):
        if layer < N_A_LAYERS:
            o = _hgrn(xb, a_w_in, layer, lower_bounds[layer], a_norm_g[layer], bsz, seq)
            w_out, w_layer = a_w_out, layer
        else:
            j = layer - N_A_LAYERS
            if kv is None:
                kv = _matmul(xb, b_w_kv[None], 0, BF16)
            q = _matmul(xb, b_w_q, j, BF16)
            o = _swa(q, kv, att_bias, b_sinks[j].astype(F32), bsz, seq)
            w_out, w_layer = b_w_out, j
        xf, xp, route, counts = _out_ln_router(o, w_out, w_layer, xf, ln_g[2 * layer], ln_b[2 * layer],
                                               w_router[layer], b_router[layer].reshape(1, ROUTER_COLS))
        y_planes = _hier_moe(xp, route, counts, moe_w_gate, moe_w_up, moe_w_down, layer)
        xf, xb = _ln_moe(xf, y_planes, route, ln_g[2 * layer + 1], ln_b[2 * layer + 1])
    return xf.reshape(bsz, seq, d).astype(x.dtype)
```

```python
import functools
import math

import numpy as np
import jax
import jax.numpy as jnp
from jax import lax
from jax.experimental import pallas as pl
from jax.experimental.pallas import tpu as pltpu

F32 = jnp.float32
BF16 = jnp.bfloat16
U32 = jnp.uint32
I32 = jnp.int32

D_MODEL = 2048
DEPTH = 4
N_A_LAYERS = DEPTH // 2
HG_HEADS = 16
HG_DK = 128
HG_DV = 128
ATT_HEAD_DIM = 64
ATT_Q_HEADS = 32
ATT_KV_HEADS = 4
ATT_GROUP = ATT_Q_HEADS // ATT_KV_HEADS
WINDOW = 128
N_BUCKETS = 32
REL_MAX_DISTANCE = 128
N_GROUPS = 4
EXPERTS_PER_GROUP = 8
N_EXPERTS = N_GROUPS * EXPERTS_PER_GROUP
TOP_K = 2
D_EXPERT = D_MODEL // 4
DEEPNORM_ALPHA = (2 * DEPTH) ** 0.25
LN_EPS = 1e-5
RMS_EPS = 1e-6
NEG_BIG = -1e30
MIN_FORGET = 1e-30
LOG2_E = 1.0 / math.log(2.0)

LANES = 128
HG_CHUNK = 128
HG_LEVELS = 7
HG_HEADS_PER_STEP = 16
HG_SUB = 32
HG_SAFE_LOG2_DECAY = 100.0
MOE_TILE = 256
OUT_LN_CHUNK = 512
ROUTER_COLS = 128
SUBLANES = 8
SLAB = (SUBLANES, LANES)
assert D_MODEL // 2 == SUBLANES * LANES
VMEM_LIMIT = 56 * 1024 * 1024
MOE_VMEM_LIMIT = 60 * 1024 * 1024
NT_DIMS = (((1,), (1,)), ((), ()))


def _params(*sem):
    return pltpu.CompilerParams(dimension_semantics=sem, vmem_limit_bytes=VMEM_LIMIT)


def _pack_bf16_pairs(y):
    half = y.shape[1] // 2
    lo = lax.bitcast_convert_type(y[:, :half].astype(BF16).astype(F32), U32)
    hi = lax.bitcast_convert_type(y[:, half:].astype(BF16).astype(F32), U32)
    w = (hi & jnp.uint32(0xFFFF0000)) | (lo >> 16)
    return jnp.swapaxes(jnp.stack([w[:, s * LANES:(s + 1) * LANES] for s in range(SUBLANES)], axis=0), 0, 1)


def _unpack_bf16_pairs(slabs):
    cols = jnp.swapaxes(slabs, 0, 1)
    w = jnp.concatenate([cols[s] for s in range(SUBLANES)], axis=1)
    lo = lax.bitcast_convert_type(w << 16, F32)
    hi = lax.bitcast_convert_type(w & jnp.uint32(0xFFFF0000), F32)
    return jnp.concatenate([lo, hi], axis=1)


def _mm_kernel(x_ref, w_ref, o_ref, wb_ref):
    @pl.when(pl.program_id(1) == 0)
    def _():
        wb_ref[...] = w_ref[...].astype(BF16)

    o_ref[...] = jnp.dot(x_ref[...], wb_ref[...], preferred_element_type=F32).astype(o_ref.dtype)


def _matmul(x, w, layer, out_dtype, tm=1024, tn=1024):
    m, k = x.shape
    n = w.shape[2]
    tm, tn = min(tm, m), min(tn, n)
    return pl.pallas_call(
        _mm_kernel,
        grid=(n // tn, m // tm),
        in_specs=[pl.BlockSpec((tm, k), lambda j, i: (i, 0)),
                  pl.BlockSpec((None, k, tn), lambda j, i: (layer, 0, j))],
        out_specs=pl.BlockSpec((tm, tn), lambda j, i: (i, j)),
        out_shape=jax.ShapeDtypeStruct((m, n), out_dtype),
        scratch_shapes=[pltpu.VMEM((k, tn), BF16)],
        compiler_params=_params("arbitrary", "arbitrary"),
        name="matmul",
    )(x, w)


def _layernorm(v, g, b):
    mu = jnp.mean(v, -1, keepdims=True)
    d = v - mu
    var = jnp.mean(d * d, -1, keepdims=True)
    return d * lax.rsqrt(var + LN_EPS) * g + b


def _route(logits):
    lane = lax.broadcasted_iota(I32, logits.shape, 1)
    big = jnp.int32(ROUTER_COLS)
    is_grp = lane < N_GROUPS
    gl = jnp.where(is_grp, logits, -jnp.inf)
    g_max = jnp.max(gl, -1, keepdims=True)
    grp = jnp.min(jnp.where(is_grp & (gl == g_max), lane, big), -1, keepdims=True)
    p_grp = 1.0 / jnp.sum(jnp.exp(gl - g_max), -1, keepdims=True)
    in_grp = (lane >= N_GROUPS) & (((lane - N_GROUPS) >> 3) == grp)
    el = jnp.where(in_grp, logits, -jnp.inf)
    v1 = jnp.max(el, -1, keepdims=True)
    i1 = jnp.min(jnp.where(in_grp & (el == v1), lane, big), -1, keepdims=True)
    el2 = jnp.where(lane == i1, -jnp.inf, el)
    v2 = jnp.max(el2, -1, keepdims=True)
    i2 = jnp.min(jnp.where(in_grp & (lane != i1) & (el2 == v2), lane, big), -1, keepdims=True)
    ex = jnp.exp(v2 - v1)
    w1 = 1.0 / (1.0 + ex)
    return i1 - N_GROUPS, i2 - N_GROUPS, p_grp * w1, p_grp * (ex * w1)


def _out_ln_router_kernel(layer, o_ref, w_hbm, x_ref, g_ref, b_ref, wr_ref, br_ref,
                          xo_ref, xp_ref, rt_ref, cnt_ref, ws_ref, wb_ref, w_sem):
    @pl.when(pl.program_id(0) == 0)
    def _():
        load_w = pltpu.make_async_copy(w_hbm.at[layer], ws_ref, w_sem.at[0])
        load_w.start()
        cnt_ref[...] = jnp.zeros_like(cnt_ref)
        load_w.wait()
        wb_ref[...] = ws_ref[...].astype(BF16)

    tm = o_ref.shape[0]
    chunk = min(OUT_LN_CHUNK, tm)
    for r0 in range(0, tm, chunk):
        rows = slice(r0, r0 + chunk)
        h = jnp.dot(o_ref[rows, :], wb_ref[...], preferred_element_type=F32)
        y = _layernorm(DEEPNORM_ALPHA * x_ref[rows, :] + h, g_ref[...], b_ref[...])
        xo_ref[rows, :] = y
        xp_ref[rows] = _pack_bf16_pairs(y)
        y_hi = y.astype(BF16)
        y_lo = (y - y_hi.astype(F32)).astype(BF16)
        t_hi = jnp.dot(y_hi, wr_ref[...], preferred_element_type=F32)
        t_lo = jnp.dot(y_lo, wr_ref[:, :ROUTER_COLS], preferred_element_type=F32)
        logits = t_hi[:, :ROUTER_COLS] + t_hi[:, ROUTER_COLS:] + t_lo + br_ref[...]
        e1, e2, g1, g2 = _route(logits)
        lane = lax.broadcasted_iota(I32, logits.shape, 1)
        rt_ref[rows, :] = jnp.where(lane == 0, e1.astype(F32),
                                    jnp.where(lane == 1, e2.astype(F32),
                                              jnp.where(lane == 2, g1, jnp.where(lane == 3, g2, 0.0))))
        hits = (lane == e1).astype(F32) + (lane == e2).astype(F32)
        cnt_ref[...] += jnp.sum(hits, 0, keepdims=True)


def _out_ln_router(o, w, layer, x, g, b, wr, br, tm=512):
    t, d = x.shape
    tm = min(tm, t)
    row = lambda i: (i, 0)
    const = lambda i: (0, 0)
    return pl.pallas_call(
        functools.partial(_out_ln_router_kernel, layer),
        grid=(t // tm,),
        in_specs=[pl.BlockSpec((tm, d), row), pl.BlockSpec(memory_space=pl.ANY), pl.BlockSpec((tm, d), row),
                  pl.BlockSpec((1, d), const), pl.BlockSpec((1, d), const),
                  pl.BlockSpec((d, 2 * ROUTER_COLS), const), pl.BlockSpec((1, ROUTER_COLS), const)],
        out_specs=[pl.BlockSpec((tm, d), row), pl.BlockSpec((tm,) + SLAB, lambda i: (i, 0, 0)),
                   pl.BlockSpec((tm, ROUTER_COLS), row), pl.BlockSpec((1, ROUTER_COLS), const)],
        out_shape=[jax.ShapeDtypeStruct((t, d), F32), jax.ShapeDtypeStruct((t,) + SLAB, U32),
                   jax.ShapeDtypeStruct((t, ROUTER_COLS), F32), jax.ShapeDtypeStruct((1, ROUTER_COLS), F32)],
        scratch_shapes=[pltpu.VMEM((d, d), F32), pltpu.VMEM((d, d), BF16), pltpu.SemaphoreType.DMA((1,))],
        compiler_params=_params("arbitrary"),
        name="out_ln_router",
    )(o, w, x, g.reshape(1, d), b.reshape(1, d), wr, br)


def _ln_moe_kernel(x_ref, y0_ref, y1_ref, rt_ref, g_ref, b_ref, xo_ref, xb_ref):
    rt = rt_ref[...]
    f = rt[:, 2:3] * _unpack_bf16_pairs(y0_ref[...]) + rt[:, 3:4] * _unpack_bf16_pairs(y1_ref[...])
    y = _layernorm(DEEPNORM_ALPHA * x_ref[...] + f, g_ref[...], b_ref[...])
    xo_ref[...] = y
    xb_ref[...] = y.astype(BF16)


def _ln_moe(x, y_planes, route, g, b, tm=512):
    t, d = x.shape
    tm = min(tm, t)
    row = lambda i: (i, 0)
    const = lambda i: (0, 0)
    return pl.pallas_call(
        _ln_moe_kernel,
        grid=(t // tm,),
        in_specs=[pl.BlockSpec((tm, d), row), pl.BlockSpec((tm,) + SLAB, lambda i: (i, 0, 0)),
                  pl.BlockSpec((tm,) + SLAB, lambda i: (i + t // tm, 0, 0)),
                  pl.BlockSpec((tm, ROUTER_COLS), row),
                  pl.BlockSpec((1, d), const), pl.BlockSpec((1, d), const)],
        out_specs=[pl.BlockSpec((tm, d), row), pl.BlockSpec((tm, d), row)],
        out_shape=[jax.ShapeDtypeStruct((t, d), F32), jax.ShapeDtypeStruct((t, d), BF16)],
        compiler_params=_params("arbitrary"),
        name="ln_moe",
    )(x, y_planes, y_planes, route, g.reshape(1, d), b.reshape(1, d))


def _hgrn_constants():
    c = HG_CHUNK
    t = np.arange(c)[:, None]
    s = np.arange(c)[None, :]
    masks = [(s == t).astype(np.float32)]
    for lvl in range(HG_LEVELS):
        m = 1 << lvl
        masks.append(((t // (2 * m) == s // (2 * m)) & ((t // m) % 2 == 1) & ((s // m) % 2 == 0)).astype(np.float32))
    sub = ((t // HG_SUB == s // HG_SUB) & (s <= t)).astype(np.float32)
    return (s <= t).astype(np.float32), np.stack(masks + [sub], axis=0)


def _rows_broadcast(b, first, period, reps):
    n = b.shape[0] // period
    return jnp.concatenate([jnp.broadcast_to(b[first + j * period:first + j * period + 1, :], (reps, b.shape[1]))
                            for j in range(n) for _ in range(period // reps)], axis=0)


def _level_decay(b, fclip, row, lvl):
    if lvl == 0:
        return jnp.where((row & 1) == 1, fclip, 1.0)
    m = 1 << lvl
    if 2 * m < 8:
        ref = jnp.where((row & m * 2) == 0, _rows_broadcast(b, m - 1, 8, 8), _rows_broadcast(b, 3 * m - 1, 8, 8))
    else:
        ref = _rows_broadcast(b, m - 1, 2 * m, 2 * m)
    return jnp.exp2(-jnp.abs(b - ref))


class _RowBlocks:
    def __init__(self, n):
        self.tiles = [None] * (n // SUBLANES)

    def add(self, first_row, val):
        for u in range(val.shape[0] // SUBLANES):
            piece = val[u * SUBLANES:(u + 1) * SUBLANES]
            i = first_row // SUBLANES + u
            self.tiles[i] = piece if self.tiles[i] is None else self.tiles[i] + piece

    def value(self):
        return jnp.concatenate(self.tiles, axis=0)


def _hgrn_scores(q, k, b, fclip, row, pm_ref, first_level):
    c = HG_CHUNK
    scores = _RowBlocks(c)
    if first_level == 0:
        scores.add(0, pm_ref[0] * lax.dot_general(q.astype(BF16), k.astype(BF16), NT_DIMS, preferred_element_type=F32))
    for lvl in range(first_level, HG_LEVELS):
        e = _level_decay(b, fclip, row, lvl)
        m = 1 << lvl
        if m < SUBLANES:
            s_l = lax.dot_general((q * e).astype(BF16), (k * e).astype(BF16), NT_DIMS, preferred_element_type=F32)
            scores.add(0, pm_ref[lvl + 1] * s_l)
        else:
            up = [slice(u * 2 * m + m, (u + 1) * 2 * m) for u in range(c // (2 * m))]
            lo = [slice(u * 2 * m, u * 2 * m + m) for u in range(c // (2 * m))]
            qe = jnp.concatenate([q[r] * e[r] for r in up], axis=0).astype(BF16)
            ke = jnp.concatenate([piece for r in lo for piece in (k[r] * e[r], jnp.zeros((m, LANES), F32))],
                                 axis=0).astype(BF16)
            s_l = lax.dot_general(qe, ke, NT_DIMS, preferred_element_type=F32)
            for u, r in enumerate(up):
                part = s_l[u * m:(u + 1) * m]
                scores.add(r.start, part if len(up) == 1 else pm_ref[lvl + 1, r, :] * part)
    return scores


def _sub_block_start(b):
    n = b.shape[0] // HG_SUB
    pieces = [jnp.zeros((HG_SUB, b.shape[1]), F32)]
    pieces += [jnp.broadcast_to(b[u * HG_SUB - 1:u * HG_SUB, :], (HG_SUB, b.shape[1])) for u in range(1, n)]
    return jnp.concatenate(pieces, axis=0)


def _hgrn_kernel(q_ref, f_ref, i_ref, g_ref, lb_ref, ng_ref, lm_ref, pm_ref, o_ref, st_ref, k_ref, b_ref, fc_ref):
    c = HG_CHUNK
    nh = HG_HEADS_PER_STEP

    @pl.when(pl.program_id(2) == 0)
    def _():
        st_ref[...] = jnp.zeros_like(st_ref)

    worst = jnp.zeros((1, LANES), F32)
    for j in range(nh):
        cols = slice(j * LANES, (j + 1) * LANES)
        lb = lb_ref[:, cols]
        sig = jax.nn.sigmoid(f_ref[:, cols])
        fclip = jnp.maximum(lb + (1.0 - lb) * sig, MIN_FORGET)
        logf = jnp.log(fclip) * LOG2_E
        g_hi = logf.astype(BF16)
        g_lo = (logf - g_hi.astype(F32)).astype(BF16)
        b2 = jnp.dot(lm_ref[...], jnp.concatenate([g_hi, g_lo], axis=1), preferred_element_type=F32)
        b = b2[:, :LANES] + b2[:, LANES:]
        k_ref[:, cols] = (1.0 - lb) * (1.0 - sig)
        b_ref[:, cols] = b
        fc_ref[:, cols] = fclip
        for u in range(c // HG_SUB):
            start = b[u * HG_SUB - 1:u * HG_SUB, :] if u else jnp.zeros((1, LANES), F32)
            worst = jnp.maximum(worst, start - b[(u + 1) * HG_SUB - 1:(u + 1) * HG_SUB, :])
    safe = jnp.max(worst) < HG_SAFE_LOG2_DECAY

    def heads(direct):
        row = lax.broadcasted_iota(I32, (c, LANES), 0)
        for j in range(nh):
            cols = slice(j * LANES, (j + 1) * LANES)
            q = q_ref[:, cols]
            v = i_ref[:, cols]
            k = k_ref[:, cols]
            b = b_ref[:, cols]
            b_last = b[c - 1:c, :]

            st = st_ref[j]
            o = lax.dot_general((q * jnp.exp2(b)).astype(BF16), st.astype(BF16), NT_DIMS, preferred_element_type=F32)

            if direct:
                rel = b - _sub_block_start(b)
                s_d = lax.dot_general((q * jnp.exp2(rel)).astype(BF16), (k * jnp.exp2(-rel)).astype(BF16),
                                      NT_DIMS, preferred_element_type=F32)
                scores = _hgrn_scores(q, k, b, None, row, pm_ref, HG_SUB.bit_length() - 1)
                scores.add(0, jnp.where(pm_ref[HG_LEVELS + 1] > 0.5, s_d, 0.0))
            else:
                scores = _hgrn_scores(q, k, b, fc_ref[:, cols], row, pm_ref, 0)
            o = o + jnp.dot(scores.value().astype(BF16), v.astype(BF16), preferred_element_type=F32)

            ks = (k * jnp.exp2(b_last - b)).astype(BF16)
            st_ref[j] = st * jnp.exp2(b_last) + jnp.dot(v.T.astype(BF16), ks, preferred_element_type=F32)

            o = o * lax.rsqrt(jnp.mean(o * o, -1, keepdims=True) + RMS_EPS) * ng_ref[:, cols]
            gate = g_ref[:, cols]
            o_ref[:, cols] = (o * (gate * jax.nn.sigmoid(gate))).astype(o_ref.dtype)

    @pl.when(safe)
    def _():
        heads(True)

    @pl.when(jnp.logical_not(safe))
    def _():
        heads(False)


def _hgrn(proj, lb, norm_g, bsz, seq):
    t = bsz * seq
    nc = seq // HG_CHUNK
    c = HG_CHUNK
    nh = HG_HEADS_PER_STEP
    w = nh * HG_DK
    hsteps = HG_HEADS // nh
    lm, pm = _hgrn_constants()

    def part(p):
        return pl.BlockSpec((c, w), lambda b, h, n: (b * nc + n, p * hsteps + h))

    head = pl.BlockSpec((1, w), lambda b, h, n: (0, h))
    return pl.pallas_call(
        _hgrn_kernel,
        grid=(bsz, hsteps, nc),
        in_specs=[part(0), part(1), part(2), part(3), head, head,
                  pl.BlockSpec((c, c), lambda b, h, n: (0, 0)),
                  pl.BlockSpec((HG_LEVELS + 2, c, c), lambda b, h, n: (0, 0, 0))],
        out_specs=pl.BlockSpec((c, w), lambda b, h, n: (b * nc + n, h)),
        out_shape=jax.ShapeDtypeStruct((t, D_MODEL), BF16),
        scratch_shapes=[pltpu.VMEM((nh, HG_DV, HG_DK), F32)] + [pltpu.VMEM((c, w), F32)] * 3,
        compiler_params=_params("arbitrary", "arbitrary", "arbitrary"),
        name="hgrn2",
    )(proj, proj, proj, proj, lb.reshape(1, D_MODEL), norm_g.reshape(1, D_MODEL),
      jnp.asarray(lm, BF16), jnp.asarray(pm, F32))


def _swa_kernel(sink_ref, q_ref, kvp_ref, kvc_ref, bias_ref, o_ref):
    n = pl.program_id(1)
    w = WINDOW
    hd = ATT_HEAD_DIM
    cols = ATT_GROUP * w
    kj = lax.broadcasted_iota(I32, (2 * w, cols), 0)
    qi = lax.broadcasted_iota(I32, (2 * w, cols), 1) & (w - 1)
    dist = qi + w - kj
    mask = (dist >= 0) & (dist < w) & ((n > 0) | (kj >= w))
    kvw = ATT_KV_HEADS * hd
    ones = jnp.ones((2 * w, hd), F32)
    for g in range(ATT_KV_HEADS):
        kwin = jnp.concatenate([kvp_ref[:, g * hd:(g + 1) * hd], kvc_ref[:, g * hd:(g + 1) * hd]], axis=0)
        vwin = jnp.concatenate([kvp_ref[:, kvw + g * hd:kvw + (g + 1) * hd],
                                kvc_ref[:, kvw + g * hd:kvw + (g + 1) * hd]], axis=0)
        heads = range(g * ATT_GROUP, (g + 1) * ATT_GROUP)
        qg = jnp.concatenate([q_ref[:, h * hd:(h + 1) * hd] for h in heads], axis=0) * (hd ** -0.5)
        sink = jnp.concatenate([jnp.full((1, w), sink_ref[h], F32) for h in heads], axis=1)
        s = lax.dot_general(kwin, qg.astype(BF16), NT_DIMS, preferred_element_type=F32) + bias_ref[g]
        s = jnp.where(mask, s, NEG_BIG)
        m = jnp.maximum(jnp.max(s, 0, keepdims=True), sink)
        p = jnp.exp(s - m).astype(BF16)
        vext_t = jnp.concatenate([vwin.astype(F32), ones], axis=1).T.astype(BF16)
        ov = jnp.dot(vext_t, p, preferred_element_type=F32)
        ov = ov / (ov[hd:hd + 1, :] + jnp.exp(sink - m))
        og = ov.T.astype(o_ref.dtype)
        o_ref[:, g * ATT_GROUP * hd:(g + 1) * ATT_GROUP * hd] = jnp.concatenate(
            [og[j * w:(j + 1) * w, :hd] for j in range(ATT_GROUP)], axis=1)


def _swa(q, kv, bias, sinks, bsz, seq):
    t = bsz * seq
    nb = seq // WINDOW
    kvw2 = 2 * ATT_KV_HEADS * ATT_HEAD_DIM
    return pl.pallas_call(
        _swa_kernel,
        grid=(bsz, nb),
        in_specs=[pl.BlockSpec(memory_space=pltpu.SMEM),
                  pl.BlockSpec((WINDOW, D_MODEL), lambda b, n: (b * nb + n, 0)),
                  pl.BlockSpec((WINDOW, kvw2), lambda b, n: (b * nb + jnp.maximum(n - 1, 0), 0)),
                  pl.BlockSpec((WINDOW, kvw2), lambda b, n: (b * nb + n, 0)),
                  pl.BlockSpec((ATT_KV_HEADS, 2 * WINDOW, ATT_GROUP * WINDOW), lambda b, n: (0, 0, 0))],
        out_specs=pl.BlockSpec((WINDOW, D_MODEL), lambda b, n: (b * nb + n, 0)),
        out_shape=jax.ShapeDtypeStruct((t, D_MODEL), BF16),
        compiler_params=_params("arbitrary", "arbitrary"),
        name="swa",
    )(sinks, q, kv, kv, bias)


def _t5_bucket(dist):
    n = jnp.clip(dist, 0, REL_MAX_DISTANCE - 1)
    max_exact = N_BUCKETS // 2
    large = max_exact + (jnp.log(jnp.maximum(n, max_exact).astype(F32) / max_exact)
                         / math.log(REL_MAX_DISTANCE / max_exact)
                         * (N_BUCKETS - max_exact)).astype(I32)
    large = jnp.minimum(large, N_BUCKETS - 1)
    return jnp.where(n < max_exact, n, large)


def _band_bias_t(rel_bias):
    kj = jnp.arange(2 * WINDOW)[:, None]
    qi = jnp.arange(WINDOW)[None, :]
    bucket = _t5_bucket(qi + WINDOW - kj).reshape(-1)
    onehot = (bucket[:, None] == jnp.arange(N_BUCKETS)[None, :]).astype(F32)
    table = jnp.dot(onehot, rel_bias.astype(F32), precision=lax.Precision.HIGHEST)
    table = table.reshape(2 * WINDOW, WINDOW, ATT_KV_HEADS, ATT_GROUP)
    return table.transpose(2, 0, 3, 1).reshape(ATT_KV_HEADS, 2 * WINDOW, ATT_GROUP * WINDOW)


def _moe_kernel(layer, te_ref, nu_ref, nx_ref, src_ref, dst_ref,
                x_hbm, wg_hbm, wu_hbm, wd_hbm, y_hbm,
                xres, xbuf, ybuf, wgs_ref, wus_ref, wds_ref, wgb_ref, wub_ref, wdb_ref, in_sem, out_sem, w_sem):
    i = pl.program_id(0)
    n_used = nu_ref[0]
    rows = MOE_TILE

    def gather_rows(tile):
        for r in range(rows):
            xbuf[r] = xres[src_ref[tile * rows + r]]

    def start_scatter(tile):
        for r in range(rows):
            pltpu.make_async_copy(ybuf.at[r], y_hbm.at[dst_ref[tile * rows + r]], out_sem.at[0]).start(priority=r % 2)

    def wait_scatter():
        pltpu.make_async_copy(ybuf, ybuf, out_sem.at[0]).wait()

    def weight_copies(e):
        return (pltpu.make_async_copy(wg_hbm.at[layer, e], wgs_ref, w_sem.at[0]),
                pltpu.make_async_copy(wu_hbm.at[layer, e], wus_ref, w_sem.at[1]),
                pltpu.make_async_copy(wd_hbm.at[layer, e], wds_ref, w_sem.at[2]))

    @pl.when(i < n_used)
    def _():
        @pl.when(i == 0)
        def _():
            for cp in weight_copies(te_ref[0]):
                cp.start()
            load_x = pltpu.make_async_copy(x_hbm, xres, in_sem.at[0])
            load_x.start()
            ybuf[...] = jnp.zeros_like(ybuf)
            dump = pltpu.make_async_copy(ybuf, y_hbm.at[pl.ds(y_hbm.shape[0] - rows, rows)], out_sem.at[0])
            dump.start()
            dump.wait()
            load_x.wait()

        @pl.when((i == 0) | (te_ref[i] != te_ref[jnp.maximum(i - 1, 0)]))
        def _():
            for cp in weight_copies(te_ref[i]):
                cp.wait()
            wgb_ref[...] = wgs_ref[...].astype(BF16)
            wub_ref[...] = wus_ref[...].astype(BF16)
            wdb_ref[...] = wds_ref[...].astype(BF16)

            @pl.when(nx_ref[i] >= 0)
            def _():
                for cp in weight_copies(nx_ref[i]):
                    cp.start()

        gather_rows(i)
        x = _unpack_bf16_pairs(xbuf[...]).astype(BF16)
        hg = jnp.dot(x, wgb_ref[...], preferred_element_type=F32)
        hu = jnp.dot(x, wub_ref[...], preferred_element_type=F32)
        hidden = (hg * jax.nn.sigmoid(hg) * hu).astype(BF16)
        y = jnp.dot(hidden, wdb_ref[...], preferred_element_type=F32)

        @pl.when(i > 0)
        def _():
            wait_scatter()

        ybuf[...] = _pack_bf16_pairs(y)
        start_scatter(i)

        @pl.when(i == n_used - 1)
        def _():
            wait_scatter()


def _moe_experts(xp, tile_expert, n_used, next_expert, slot_src, slot_dst, w_gate, w_up, w_down, layer):
    t = xp.shape[0]
    d = D_MODEL
    n_tiles = tile_expert.shape[0]
    any_spec = pl.BlockSpec(memory_space=pl.ANY)
    return pl.pallas_call(
        functools.partial(_moe_kernel, layer),
        grid_spec=pltpu.PrefetchScalarGridSpec(
            num_scalar_prefetch=5,
            grid=(n_tiles,),
            in_specs=[any_spec, any_spec, any_spec, any_spec],
            out_specs=any_spec,
            scratch_shapes=[pltpu.VMEM((t,) + SLAB, U32),
                            pltpu.VMEM((MOE_TILE,) + SLAB, U32), pltpu.VMEM((MOE_TILE,) + SLAB, U32),
                            pltpu.VMEM((d, D_EXPERT), F32), pltpu.VMEM((d, D_EXPERT), F32),
                            pltpu.VMEM((D_EXPERT, d), F32),
                            pltpu.VMEM((d, D_EXPERT), BF16), pltpu.VMEM((d, D_EXPERT), BF16),
                            pltpu.VMEM((D_EXPERT, d), BF16),
                            pltpu.SemaphoreType.DMA((1,)), pltpu.SemaphoreType.DMA((1,)),
                            pltpu.SemaphoreType.DMA((3,))]),
        out_shape=jax.ShapeDtypeStruct((TOP_K * t + MOE_TILE,) + SLAB, U32),
        compiler_params=pltpu.CompilerParams(dimension_semantics=("arbitrary",), vmem_limit_bytes=MOE_VMEM_LIMIT),
        name="moe_experts",
    )(tile_expert, n_used, next_expert, slot_src, slot_dst, xp, w_gate, w_up, w_down)


def _hier_moe(xp, route, counts, w_gate, w_up, w_down, layer):
    n_tok = xp.shape[0]
    n_asg = n_tok * TOP_K
    e_flat = route[:, :TOP_K].astype(I32).reshape(-1)
    order = jnp.argsort(e_flat).astype(I32)
    counts = counts[0, :N_EXPERTS].astype(I32)
    starts = jnp.cumsum(counts) - counts
    padded = (counts + MOE_TILE - 1) // MOE_TILE * MOE_TILE
    pad_ends = jnp.cumsum(padded)
    pad_starts = pad_ends - padded
    n_tiles = -(-n_asg // MOE_TILE) + N_EXPERTS
    n_used = pad_ends[-1] // MOE_TILE

    tile_start = jnp.arange(n_tiles, dtype=I32) * MOE_TILE
    used = tile_start < pad_ends[-1]
    tile_expert = jnp.minimum(jnp.sum((pad_ends[None, :] <= tile_start[:, None]).astype(I32), 1), N_EXPERTS - 1)
    next_tile = pad_ends[tile_expert] // MOE_TILE
    next_expert = jnp.where(used & (next_tile < n_used), tile_expert[jnp.minimum(next_tile, n_tiles - 1)], -1)
    rank0 = tile_start - pad_starts[tile_expert]
    n_valid = jnp.where(used, jnp.clip(counts[tile_expert] - rank0, 0, MOE_TILE), 0)
    sorted0 = starts[tile_expert] + rank0

    r = jnp.arange(MOE_TILE, dtype=I32)[None, :]
    valid = r < n_valid[:, None]
    asg = order[jnp.clip(sorted0[:, None] + r, 0, n_asg - 1)]
    slot_src = jnp.where(valid, asg // TOP_K, 0).reshape(-1)
    slot_dst = jnp.where(valid, (asg % TOP_K) * n_tok + asg // TOP_K, TOP_K * n_tok + r).reshape(-1)
    return _moe_experts(xp, tile_expert, n_used.reshape(1).astype(I32), next_expert.astype(I32),
                        slot_src, slot_dst, w_gate, w_up, w_down, layer)


def kernel(x, a_w_in, a_lower_bound, a_norm_g, a_w_out, b_w_kv, b_w_q, b_sinks, b_w_out, rel_bias,
           moe_w_rg, moe_b_rg, moe_w_re, moe_b_re, moe_w_gate, moe_w_up, moe_w_down, ln_g, ln_b):
    bsz, seq, d = x.shape
    t = bsz * seq
    lb_sm = jax.nn.softmax(a_lower_bound.astype(F32), axis=0)
    lower_bounds = jnp.cumsum(lb_sm, axis=0) - lb_sm[0]
    att_bias = _band_bias_t(rel_bias)
    pad = ROUTER_COLS - N_GROUPS - N_EXPERTS
    w_router = jnp.concatenate([moe_w_rg, moe_w_re, jnp.zeros((DEPTH, d, pad), F32)], axis=-1)
    w_router_hi = w_router.astype(BF16)
    w_router_lo = (w_router - w_router_hi.astype(F32)).astype(BF16)
    w_router = jnp.concatenate([w_router_hi, w_router_lo], axis=-1)
    b_router = jnp.concatenate([moe_b_rg, moe_b_re, jnp.zeros((DEPTH, pad), F32)], axis=-1)

    xf = x.reshape(t, d).astype(F32)
    xb = xf.astype(BF16)
    kv = None
    for layer in range(DEPTH):
        if layer < N_A_LAYERS:
            proj = _matmul(xb, a_w_in, layer, F32)
            o = _hgrn(proj, lower_bounds[layer], a_norm_g[layer], bsz, seq)
            w_out, w_layer = a_w_out, layer
        else:
            j = layer - N_A_LAYERS
            if kv is None:
                kv = _matmul(xb, b_w_kv[None], 0, BF16)
            q = _matmul(xb, b_w_q, j, BF16)
            o = _swa(q, kv, att_bias, b_sinks[j].astype(F32), bsz, seq)
            w_out, w_layer = b_w_out, j
        xf, xp, route, counts = _out_ln_router(o, w_out, w_layer, xf, ln_g[2 * layer], ln_b[2 * layer],
                                               w_router[layer], b_router[layer].reshape(1, ROUTER_COLS))
        y_planes = _hier_moe(xp, route, counts, moe_w_gate, moe_w_up, moe_w_down, layer)
        xf, xb = _ln_moe(xf, y_planes, route, ln_g[2 * layer + 1], ln_b[2 * layer + 1])
    return xf.reshape(bsz, seq, d).astype(x.dtype)
```

```python
import functools
import math

import numpy as np
import jax
import jax.numpy as jnp
from jax import lax
from jax.experimental import pallas as pl
from jax.experimental.pallas import tpu as pltpu

F32 = jnp.float32
BF16 = jnp.bfloat16
U32 = jnp.uint32
I32 = jnp.int32

D_MODEL = 2048
DEPTH = 4
N_A_LAYERS = DEPTH // 2
HG_HEADS = 16
HG_DK = 128
HG_DV = 128
ATT_HEAD_DIM = 64
ATT_Q_HEADS = 32
ATT_KV_HEADS = 4
ATT_GROUP = ATT_Q_HEADS // ATT_KV_HEADS
WINDOW = 128
N_BUCKETS = 32
REL_MAX_DISTANCE = 128
N_GROUPS = 4
EXPERTS_PER_GROUP = 8
N_EXPERTS = N_GROUPS * EXPERTS_PER_GROUP
TOP_K = 2
D_EXPERT = D_MODEL // 4
DEEPNORM_ALPHA = (2 * DEPTH) ** 0.25
LN_EPS = 1e-5
RMS_EPS = 1e-6
NEG_BIG = -1e30
MIN_FORGET = 1e-30
LOG2_E = 1.0 / math.log(2.0)

LANES = 128
HG_CHUNK = 128
HG_LEVELS = 7
HG_HEADS_PER_STEP = 16
HG_SUB = 32
HG_SAFE_LOG2_DECAY = 100.0
MOE_TILE = 256
OUT_LN_CHUNK = 512
ROUTER_COLS = 128
SUBLANES = 8
SLAB = (SUBLANES, LANES)
assert D_MODEL // 2 == SUBLANES * LANES
VMEM_LIMIT = 56 * 1024 * 1024
MOE_VMEM_LIMIT = 60 * 1024 * 1024
NT_DIMS = (((1,), (1,)), ((), ()))


def _params(*sem):
    return pltpu.CompilerParams(dimension_semantics=sem, vmem_limit_bytes=VMEM_LIMIT)


def _pack_bf16_pairs(y):
    half = y.shape[1] // 2
    lo = lax.bitcast_convert_type(y[:, :half].astype(BF16).astype(F32), U32)
    hi = lax.bitcast_convert_type(y[:, half:].astype(BF16).astype(F32), U32)
    w = (hi & jnp.uint32(0xFFFF0000)) | (lo >> 16)
    return jnp.swapaxes(jnp.stack([w[:, s * LANES:(s + 1) * LANES] for s in range(SUBLANES)], axis=0), 0, 1)


def _unpack_bf16_pairs(slabs):
    cols = jnp.swapaxes(slabs, 0, 1)
    w = jnp.concatenate([cols[s] for s in range(SUBLANES)], axis=1)
    lo = lax.bitcast_convert_type(w << 16, F32)
    hi = lax.bitcast_convert_type(w & jnp.uint32(0xFFFF0000), F32)
    return jnp.concatenate([lo, hi], axis=1)


def _mm_kernel(x_ref, w_ref, o_ref, wb_ref):
    @pl.when(pl.program_id(1) == 0)
    def _():
        wb_ref[...] = w_ref[...].astype(BF16)

    o_ref[...] = jnp.dot(x_ref[...], wb_ref[...], preferred_element_type=F32).astype(o_ref.dtype)


def _matmul(x, w, layer, out_dtype, tm=1024, tn=1024):
    m, k = x.shape
    n = w.shape[2]
    tm, tn = min(tm, m), min(tn, n)
    return pl.pallas_call(
        _mm_kernel,
        grid=(n // tn, m // tm),
        in_specs=[pl.BlockSpec((tm, k), lambda j, i: (i, 0)),
                  pl.BlockSpec((None, k, tn), lambda j, i: (layer, 0, j))],
        out_specs=pl.BlockSpec((tm, tn), lambda j, i: (i, j)),
        out_shape=jax.ShapeDtypeStruct((m, n), out_dtype),
        scratch_shapes=[pltpu.VMEM((k, tn), BF16)],
        compiler_params=_params("arbitrary", "arbitrary"),
        name="matmul",
    )(x, w)


def _layernorm(v, g, b):
    mu = jnp.mean(v, -1, keepdims=True)
    d = v - mu
    var = jnp.mean(d * d, -1, keepdims=True)
    return d * lax.rsqrt(var + LN_EPS) * g + b


def _route(logits):
    lane = lax.broadcasted_iota(I32, logits.shape, 1)
    big = jnp.int32(ROUTER_COLS)
    is_grp = lane < N_GROUPS
    gl = jnp.where(is_grp, logits, -jnp.inf)
    g_max = jnp.max(gl, -1, keepdims=True)
    grp = jnp.min(jnp.where(is_grp & (gl == g_max), lane, big), -1, keepdims=True)
    p_grp = 1.0 / jnp.sum(jnp.exp(gl - g_max), -1, keepdims=True)
    in_grp = (lane >= N_GROUPS) & (((lane - N_GROUPS) >> 3) == grp)
    el = jnp.where(in_grp, logits, -jnp.inf)
    v1 = jnp.max(el, -1, keepdims=True)
    i1 = jnp.min(jnp.where(in_grp & (el == v1), lane, big), -1, keepdims=True)
    el2 = jnp.where(lane == i1, -jnp.inf, el)
    v2 = jnp.max(el2, -1, keepdims=True)
    i2 = jnp.min(jnp.where(in_grp & (lane != i1) & (el2 == v2), lane, big), -1, keepdims=True)
    ex = jnp.exp(v2 - v1)
    w1 = 1.0 / (1.0 + ex)
    return i1 - N_GROUPS, i2 - N_GROUPS, p_grp * w1, p_grp * (ex * w1)


def _out_ln_router_kernel(layer, o_ref, w_hbm, x_ref, g_ref, b_ref, wr_ref, br_ref,
                          xo_ref, xp_ref, rt_ref, cnt_ref, ws_ref, wb_ref, w_sem):
    @pl.when(pl.program_id(0) == 0)
    def _():
        load_w = pltpu.make_async_copy(w_hbm.at[layer], ws_ref, w_sem.at[0])
        load_w.start()
        cnt_ref[...] = jnp.zeros_like(cnt_ref)
        load_w.wait()
        wb_ref[...] = ws_ref[...].astype(BF16)

    tm = o_ref.shape[0]
    chunk = min(OUT_LN_CHUNK, tm)
    for r0 in range(0, tm, chunk):
        rows = slice(r0, r0 + chunk)
        h = jnp.dot(o_ref[rows, :], wb_ref[...], preferred_element_type=F32)
        y = _layernorm(DEEPNORM_ALPHA * x_ref[rows, :] + h, g_ref[...], b_ref[...])
        xo_ref[rows, :] = y
        xp_ref[rows] = _pack_bf16_pairs(y)
        y_hi = y.astype(BF16)
        y_lo = (y - y_hi.astype(F32)).astype(BF16)
        t_hi = jnp.dot(y_hi, wr_ref[...], preferred_element_type=F32)
        t_lo = jnp.dot(y_lo, wr_ref[:, :ROUTER_COLS], preferred_element_type=F32)
        logits = t_hi[:, :ROUTER_COLS] + t_hi[:, ROUTER_COLS:] + t_lo + br_ref[...]
        e1, e2, g1, g2 = _route(logits)
        lane = lax.broadcasted_iota(I32, logits.shape, 1)
        rt_ref[rows, :] = jnp.where(lane == 0, e1.astype(F32),
                                    jnp.where(lane == 1, e2.astype(F32),
                                              jnp.where(lane == 2, g1, jnp.where(lane == 3, g2, 0.0))))
        hits = (lane == e1).astype(F32) + (lane == e2).astype(F32)
        cnt_ref[...] += jnp.sum(hits, 0, keepdims=True)


def _out_ln_router(o, w, layer, x, g, b, wr, br, tm=512):
    t, d = x.shape
    tm = min(tm, t)
    row = lambda i: (i, 0)
    const = lambda i: (0, 0)
    return pl.pallas_call(
        functools.partial(_out_ln_router_kernel, layer),
        grid=(t // tm,),
        in_specs=[pl.BlockSpec((tm, d), row), pl.BlockSpec(memory_space=pl.ANY), pl.BlockSpec((tm, d), row),
                  pl.BlockSpec((1, d), const), pl.BlockSpec((1, d), const),
                  pl.BlockSpec((d, 2 * ROUTER_COLS), const), pl.BlockSpec((1, ROUTER_COLS), const)],
        out_specs=[pl.BlockSpec((tm, d), row), pl.BlockSpec((tm,) + SLAB, lambda i: (i, 0, 0)),
                   pl.BlockSpec((tm, ROUTER_COLS), row), pl.BlockSpec((1, ROUTER_COLS), const)],
        out_shape=[jax.ShapeDtypeStruct((t, d), F32), jax.ShapeDtypeStruct((t,) + SLAB, U32),
                   jax.ShapeDtypeStruct((t, ROUTER_COLS), F32), jax.ShapeDtypeStruct((1, ROUTER_COLS), F32)],
        scratch_shapes=[pltpu.VMEM((d, d), F32), pltpu.VMEM((d, d), BF16), pltpu.SemaphoreType.DMA((1,))],
        compiler_params=_params("arbitrary"),
        name="out_ln_router",
    )(o, w, x, g.reshape(1, d), b.reshape(1, d), wr, br)


def _ln_moe_kernel(x_ref, y0_ref, y1_ref, rt_ref, g_ref, b_ref, xo_ref, xb_ref):
    rt = rt_ref[...]
    f = rt[:, 2:3] * _unpack_bf16_pairs(y0_ref[...]) + rt[:, 3:4] * _unpack_bf16_pairs(y1_ref[...])
    y = _layernorm(DEEPNORM_ALPHA * x_ref[...] + f, g_ref[...], b_ref[...])
    xo_ref[...] = y
    xb_ref[...] = y.astype(BF16)


def _ln_moe(x, y_planes, route, g, b, tm=512):
    t, d = x.shape
    tm = min(tm, t)
    row = lambda i: (i, 0)
    const = lambda i: (0, 0)
    return pl.pallas_call(
        _ln_moe_kernel,
        grid=(t // tm,),
        in_specs=[pl.BlockSpec((tm, d), row), pl.BlockSpec((tm,) + SLAB, lambda i: (i, 0, 0)),
                  pl.BlockSpec((tm,) + SLAB, lambda i: (i + t // tm, 0, 0)),
                  pl.BlockSpec((tm, ROUTER_COLS), row),
                  pl.BlockSpec((1, d), const), pl.BlockSpec((1, d), const)],
        out_specs=[pl.BlockSpec((tm, d), row), pl.BlockSpec((tm, d), row)],
        out_shape=[jax.ShapeDtypeStruct((t, d), F32), jax.ShapeDtypeStruct((t, d), BF16)],
        compiler_params=_params("arbitrary"),
        name="ln_moe",
    )(x, y_planes, y_planes, route, g.reshape(1, d), b.reshape(1, d))


def _hgrn_constants():
    c = HG_CHUNK
    t = np.arange(c)[:, None]
    s = np.arange(c)[None, :]
    masks = [(s == t).astype(np.float32)]
    for lvl in range(HG_LEVELS):
        m = 1 << lvl
        masks.append(((t // (2 * m) == s // (2 * m)) & ((t // m) % 2 == 1) & ((s // m) % 2 == 0)).astype(np.float32))
    sub = ((t // HG_SUB == s // HG_SUB) & (s <= t)).astype(np.float32)
    return (s <= t).astype(np.float32), np.stack(masks + [sub], axis=0)


def _rows_broadcast(b, first, period, reps):
    n = b.shape[0] // period
    return jnp.concatenate([jnp.broadcast_to(b[first + j * period:first + j * period + 1, :], (reps, b.shape[1]))
                            for j in range(n) for _ in range(period // reps)], axis=0)


def _level_decay(b, fclip, row, lvl):
    if lvl == 0:
        return jnp.where((row & 1) == 1, fclip, 1.0)
    m = 1 << lvl
    if 2 * m < 8:
        ref = jnp.where((row & m * 2) == 0, _rows_broadcast(b, m - 1, 8, 8), _rows_broadcast(b, 3 * m - 1, 8, 8))
    else:
        ref = _rows_broadcast(b, m - 1, 2 * m, 2 * m)
    return jnp.exp2(-jnp.abs(b - ref))


class _RowBlocks:
    def __init__(self, n):
        self.tiles = [None] * (n // SUBLANES)

    def add(self, first_row, val):
        for u in range(val.shape[0] // SUBLANES):
            piece = val[u * SUBLANES:(u + 1) * SUBLANES]
            i = first_row // SUBLANES + u
            self.tiles[i] = piece if self.tiles[i] is None else self.tiles[i] + piece

    def value(self):
        return jnp.concatenate(self.tiles, axis=0)


def _hgrn_scores(q, k, b, fclip, row, pm_ref, first_level):
    c = HG_CHUNK
    scores = _RowBlocks(c)
    if first_level == 0:
        scores.add(0, pm_ref[0] * lax.dot_general(q.astype(BF16), k.astype(BF16), NT_DIMS, preferred_element_type=F32))
    for lvl in range(first_level, HG_LEVELS):
        e = _level_decay(b, fclip, row, lvl)
        m = 1 << lvl
        if m < SUBLANES:
            s_l = lax.dot_general((q * e).astype(BF16), (k * e).astype(BF16), NT_DIMS, preferred_element_type=F32)
            scores.add(0, pm_ref[lvl + 1] * s_l)
        else:
            up = [slice(u * 2 * m + m, (u + 1) * 2 * m) for u in range(c // (2 * m))]
            lo = [slice(u * 2 * m, u * 2 * m + m) for u in range(c // (2 * m))]
            qe = jnp.concatenate([q[r] * e[r] for r in up], axis=0).astype(BF16)
            ke = jnp.concatenate([piece for r in lo for piece in (k[r] * e[r], jnp.zeros((m, LANES), F32))],
                                 axis=0).astype(BF16)
            s_l = lax.dot_general(qe, ke, NT_DIMS, preferred_element_type=F32)
            for u, r in enumerate(up):
                part = s_l[u * m:(u + 1) * m]
                scores.add(r.start, part if len(up) == 1 else pm_ref[lvl + 1, r, :] * part)
    return scores


def _sub_block_start(b):
    n = b.shape[0] // HG_SUB
    pieces = [jnp.zeros((HG_SUB, b.shape[1]), F32)]
    pieces += [jnp.broadcast_to(b[u * HG_SUB - 1:u * HG_SUB, :], (HG_SUB, b.shape[1])) for u in range(1, n)]
    return jnp.concatenate(pieces, axis=0)


def _hgrn_kernel(q_ref, f_ref, i_ref, g_ref, lb_ref, ng_ref, lm_ref, pm_ref, o_ref, st_ref, k_ref, b_ref, fc_ref):
    c = HG_CHUNK
    nh = HG_HEADS_PER_STEP

    @pl.when(pl.program_id(2) == 0)
    def _():
        st_ref[...] = jnp.zeros_like(st_ref)

    worst = jnp.zeros((1, LANES), F32)
    for j in range(nh):
        cols = slice(j * LANES, (j + 1) * LANES)
        lb = lb_ref[:, cols]
        sig = jax.nn.sigmoid(f_ref[:, cols])
        fclip = jnp.maximum(lb + (1.0 - lb) * sig, MIN_FORGET)
        logf = jnp.log(fclip) * LOG2_E
        g_hi = logf.astype(BF16)
        g_lo = (logf - g_hi.astype(F32)).astype(BF16)
        b2 = jnp.dot(lm_ref[...], jnp.concatenate([g_hi, g_lo], axis=1), preferred_element_type=F32)
        b = b2[:, :LANES] + b2[:, LANES:]
        k_ref[:, cols] = (1.0 - lb) * (1.0 - sig)
        b_ref[:, cols] = b
        fc_ref[:, cols] = fclip
        for u in range(c // HG_SUB):
            start = b[u * HG_SUB - 1:u * HG_SUB, :] if u else jnp.zeros((1, LANES), F32)
            worst = jnp.maximum(worst, start - b[(u + 1) * HG_SUB - 1:(u + 1) * HG_SUB, :])
    safe = jnp.max(worst) < HG_SAFE_LOG2_DECAY

    def heads(direct):
        row = lax.broadcasted_iota(I32, (c, LANES), 0)
        for j in range(nh):
            cols = slice(j * LANES, (j + 1) * LANES)
            q = q_ref[:, cols]
            v = i_ref[:, cols]
            k = k_ref[:, cols]
            b = b_ref[:, cols]
            b_last = b[c - 1:c, :]

            st = st_ref[j]
            o = lax.dot_general((q * jnp.exp2(b)).astype(BF16), st.astype(BF16), NT_DIMS, preferred_element_type=F32)

            if direct:
                rel = b - _sub_block_start(b)
                s_d = lax.dot_general((q * jnp.exp2(rel)).astype(BF16), (k * jnp.exp2(-rel)).astype(BF16),
                                      NT_DIMS, preferred_element_type=F32)
                scores = _hgrn_scores(q, k, b, None, row, pm_ref, HG_SUB.bit_length() - 1)
                scores.add(0, jnp.where(pm_ref[HG_LEVELS + 1] > 0.5, s_d, 0.0))
            else:
                scores = _hgrn_scores(q, k, b, fc_ref[:, cols], row, pm_ref, 0)
            o = o + jnp.dot(scores.value().astype(BF16), v.astype(BF16), preferred_element_type=F32)

            ks = (k * jnp.exp2(b_last - b)).astype(BF16)
            st_ref[j] = st * jnp.exp2(b_last) + jnp.dot(v.T.astype(BF16), ks, preferred_element_type=F32)

            o = o * lax.rsqrt(jnp.mean(o * o, -1, keepdims=True) + RMS_EPS) * ng_ref[:, cols]
            gate = g_ref[:, cols]
            o_ref[:, cols] = (o * (gate * jax.nn.sigmoid(gate))).astype(o_ref.dtype)

    @pl.when(safe)
    def _():
        heads(True)

    @pl.when(jnp.logical_not(safe))
    def _():
        heads(False)


def _hgrn(proj, lb, norm_g, bsz, seq):
    t = bsz * seq
    nc = seq // HG_CHUNK
    c = HG_CHUNK
    nh = HG_HEADS_PER_STEP
    w = nh * HG_DK
    hsteps = HG_HEADS // nh
    lm, pm = _hgrn_constants()

    def part(p):
        return pl.BlockSpec((c, w), lambda b, h, n: (b * nc + n, p * hsteps + h))

    head = pl.BlockSpec((1, w), lambda b, h, n: (0, h))
    return pl.pallas_call(
        _hgrn_kernel,
        grid=(bsz, hsteps, nc),
        in_specs=[part(0), part(1), part(2), part(3), head, head,
                  pl.BlockSpec((c, c), lambda b, h, n: (0, 0)),
                  pl.BlockSpec((HG_LEVELS + 2, c, c), lambda b, h, n: (0, 0, 0))],
        out_specs=pl.BlockSpec((c, w), lambda b, h, n: (b * nc + n, h)),
        out_shape=jax.ShapeDtypeStruct((t, D_MODEL), BF16),
        scratch_shapes=[pltpu.VMEM((nh, HG_DV, HG_DK), F32)] + [pltpu.VMEM((c, w), F32)] * 3,
        compiler_params=_params("arbitrary", "arbitrary", "arbitrary"),
        name="hgrn2",
    )(proj, proj, proj, proj, lb.reshape(1, D_MODEL), norm_g.reshape(1, D_MODEL),
      jnp.asarray(lm, BF16), jnp.asarray(pm, F32))


def _swa_kernel(sink_ref, q_ref, kvp_ref, kvc_ref, bias_ref, o_ref):
    w = WINDOW
    hd = ATT_HEAD_DIM
    kvw = ATT_KV_HEADS * hd
    ones = jnp.ones((2 * w, hd), F32)
    for g in range(ATT_KV_HEADS):
        kwin = jnp.concatenate([kvp_ref[:, g * hd:(g + 1) * hd], kvc_ref[:, g * hd:(g + 1) * hd]], axis=0)
        vwin = jnp.concatenate([kvp_ref[:, kvw + g * hd:kvw + (g + 1) * hd],
                                kvc_ref[:, kvw + g * hd:kvw + (g + 1) * hd]], axis=0)
        heads = range(g * ATT_GROUP, (g + 1) * ATT_GROUP)
        qg = jnp.concatenate([q_ref[:, h * hd:(h + 1) * hd] for h in heads], axis=0) * (hd ** -0.5)
        sink = jnp.concatenate([jnp.full((1, w), sink_ref[h], F32) for h in heads], axis=1)
        s = lax.dot_general(kwin, qg.astype(BF16), NT_DIMS, preferred_element_type=F32) + bias_ref[g]
        m = jnp.maximum(jnp.max(s, 0, keepdims=True), sink)
        p = jnp.exp(s - m).astype(BF16)
        vext_t = jnp.concatenate([vwin.astype(F32), ones], axis=1).T.astype(BF16)
        ov = jnp.dot(vext_t, p, preferred_element_type=F32)
        ov = ov / (ov[hd:hd + 1, :] + jnp.exp(sink - m))
        og = ov.T.astype(o_ref.dtype)
        o_ref[:, g * ATT_GROUP * hd:(g + 1) * ATT_GROUP * hd] = jnp.concatenate(
            [og[j * w:(j + 1) * w, :hd] for j in range(ATT_GROUP)], axis=1)


def _swa(q, kv, bias, sinks, bsz, seq):
    t = bsz * seq
    nb = seq // WINDOW
    kvw2 = 2 * ATT_KV_HEADS * ATT_HEAD_DIM
    return pl.pallas_call(
        _swa_kernel,
        grid=(bsz, nb),
        in_specs=[pl.BlockSpec(memory_space=pltpu.SMEM),
                  pl.BlockSpec((WINDOW, D_MODEL), lambda b, n: (b * nb + n, 0)),
                  pl.BlockSpec((WINDOW, kvw2), lambda b, n: (b * nb + jnp.maximum(n - 1, 0), 0)),
                  pl.BlockSpec((WINDOW, kvw2), lambda b, n: (b * nb + n, 0)),
                  pl.BlockSpec((None, ATT_KV_HEADS, 2 * WINDOW, ATT_GROUP * WINDOW),
                               lambda b, n: (jnp.minimum(n, 1), 0, 0, 0))],
        out_specs=pl.BlockSpec((WINDOW, D_MODEL), lambda b, n: (b * nb + n, 0)),
        out_shape=jax.ShapeDtypeStruct((t, D_MODEL), BF16),
        compiler_params=_params("arbitrary", "arbitrary"),
        name="swa",
    )(sinks, q, kv, kv, bias)


def _t5_bucket(dist):
    n = jnp.clip(dist, 0, REL_MAX_DISTANCE - 1)
    max_exact = N_BUCKETS // 2
    large = max_exact + (jnp.log(jnp.maximum(n, max_exact).astype(F32) / max_exact)
                         / math.log(REL_MAX_DISTANCE / max_exact)
                         * (N_BUCKETS - max_exact)).astype(I32)
    large = jnp.minimum(large, N_BUCKETS - 1)
    return jnp.where(n < max_exact, n, large)


def _band_bias_t(rel_bias):
    kj = jnp.arange(2 * WINDOW)[:, None]
    qi = jnp.arange(WINDOW)[None, :]
    bucket = _t5_bucket(qi + WINDOW - kj).reshape(-1)
    onehot = (bucket[:, None] == jnp.arange(N_BUCKETS)[None, :]).astype(F32)
    table = jnp.dot(onehot, rel_bias.astype(F32), precision=lax.Precision.HIGHEST)
    table = table.reshape(2 * WINDOW, WINDOW, ATT_KV_HEADS, ATT_GROUP)
    table = table.transpose(2, 0, 3, 1)
    dist = (qi + WINDOW - kj)[None, :, None, :]
    in_window = (dist >= 0) & (dist < WINDOW)
    own_block = (kj >= WINDOW)[None, :, None, :]
    both = jnp.stack([jnp.where(in_window & own_block, table, NEG_BIG), jnp.where(in_window, table, NEG_BIG)])
    return both.reshape(2, ATT_KV_HEADS, 2 * WINDOW, ATT_GROUP * WINDOW)


def _moe_kernel(layer, te_ref, nu_ref, nx_ref, nv_ref, src_ref, dst_ref,
                x_hbm, wg_hbm, wu_hbm, wd_hbm, y_hbm,
                xres, xbuf, ybuf, wgs_ref, wus_ref, wds_ref, wgb_ref, wub_ref, wdb_ref, in_sem, out_sem, w_sem):
    i = pl.program_id(0)
    n_used = nu_ref[0]
    rows = MOE_TILE
    half = MOE_TILE // 2

    def wait_scatter(n):
        pltpu.make_async_copy(ybuf.at[pl.ds(0, n)], ybuf.at[pl.ds(0, n)], out_sem.at[0]).wait()

    def wait_scatter_of(tile):
        @pl.when(nv_ref[tile] <= half)
        def _():
            wait_scatter(half)

        @pl.when(nv_ref[tile] > half)
        def _():
            wait_scatter(rows)

    def weight_copies(e):
        return (pltpu.make_async_copy(wg_hbm.at[layer, e], wgs_ref, w_sem.at[0]),
                pltpu.make_async_copy(wu_hbm.at[layer, e], wus_ref, w_sem.at[1]),
                pltpu.make_async_copy(wd_hbm.at[layer, e], wds_ref, w_sem.at[2]))

    def run_tile(n):
        for r in range(n):
            xbuf[r] = xres[src_ref[i * rows + r]]
        x = _unpack_bf16_pairs(xbuf[pl.ds(0, n)]).astype(BF16)
        hg = jnp.dot(x, wgb_ref[...], preferred_element_type=F32)
        hu = jnp.dot(x, wub_ref[...], preferred_element_type=F32)
        hidden = (hg * jax.nn.sigmoid(hg) * hu).astype(BF16)
        y = jnp.dot(hidden, wdb_ref[...], preferred_element_type=F32)

        @pl.when(i > 0)
        def _():
            wait_scatter_of(i - 1)

        ybuf[pl.ds(0, n)] = _pack_bf16_pairs(y)
        for r in range(n):
            pltpu.make_async_copy(ybuf.at[r], y_hbm.at[dst_ref[i * rows + r]], out_sem.at[0]).start(priority=r % 2)

    @pl.when(i < n_used)
    def _():
        @pl.when(i == 0)
        def _():
            for cp in weight_copies(te_ref[0]):
                cp.start()
            load_x = pltpu.make_async_copy(x_hbm, xres, in_sem.at[0])
            load_x.start()
            ybuf[...] = jnp.zeros_like(ybuf)
            dump = pltpu.make_async_copy(ybuf, y_hbm.at[pl.ds(y_hbm.shape[0] - rows, rows)], out_sem.at[0])
            dump.start()
            dump.wait()
            load_x.wait()

        @pl.when((i == 0) | (te_ref[i] != te_ref[jnp.maximum(i - 1, 0)]))
        def _():
            for cp in weight_copies(te_ref[i]):
                cp.wait()
            wgb_ref[...] = wgs_ref[...].astype(BF16)
            wub_ref[...] = wus_ref[...].astype(BF16)
            wdb_ref[...] = wds_ref[...].astype(BF16)

            @pl.when(nx_ref[i] >= 0)
            def _():
                for cp in weight_copies(nx_ref[i]):
                    cp.start()

        @pl.when(nv_ref[i] <= half)
        def _():
            run_tile(half)

        @pl.when(nv_ref[i] > half)
        def _():
            run_tile(rows)

        @pl.when(i == n_used - 1)
        def _():
            wait_scatter_of(i)


def _moe_experts(xp, tile_expert, n_used, next_expert, n_valid, slot_src, slot_dst, w_gate, w_up, w_down, layer):
    t = xp.shape[0]
    d = D_MODEL
    n_tiles = tile_expert.shape[0]
    any_spec = pl.BlockSpec(memory_space=pl.ANY)
    return pl.pallas_call(
        functools.partial(_moe_kernel, layer),
        grid_spec=pltpu.PrefetchScalarGridSpec(
            num_scalar_prefetch=6,
            grid=(n_tiles,),
            in_specs=[any_spec, any_spec, any_spec, any_spec],
            out_specs=any_spec,
            scratch_shapes=[pltpu.VMEM((t,) + SLAB, U32),
                            pltpu.VMEM((MOE_TILE,) + SLAB, U32), pltpu.VMEM((MOE_TILE,) + SLAB, U32),
                            pltpu.VMEM((d, D_EXPERT), F32), pltpu.VMEM((d, D_EXPERT), F32),
                            pltpu.VMEM((D_EXPERT, d), F32),
                            pltpu.VMEM((d, D_EXPERT), BF16), pltpu.VMEM((d, D_EXPERT), BF16),
                            pltpu.VMEM((D_EXPERT, d), BF16),
                            pltpu.SemaphoreType.DMA((1,)), pltpu.SemaphoreType.DMA((1,)),
                            pltpu.SemaphoreType.DMA((3,))]),
        out_shape=jax.ShapeDtypeStruct((TOP_K * t + MOE_TILE,) + SLAB, U32),
        compiler_params=pltpu.CompilerParams(dimension_semantics=("arbitrary",), vmem_limit_bytes=MOE_VMEM_LIMIT),
        name="moe_experts",
    )(tile_expert, n_used, next_expert, n_valid, slot_src, slot_dst, xp, w_gate, w_up, w_down)


def _hier_moe(xp, route, counts, w_gate, w_up, w_down, layer):
    n_tok = xp.shape[0]
    n_asg = n_tok * TOP_K
    e_flat = route[:, :TOP_K].astype(I32).reshape(-1)
    order = jnp.argsort(e_flat).astype(I32)
    counts = counts[0, :N_EXPERTS].astype(I32)
    starts = jnp.cumsum(counts) - counts
    padded = (counts + MOE_TILE - 1) // MOE_TILE * MOE_TILE
    pad_ends = jnp.cumsum(padded)
    pad_starts = pad_ends - padded
    n_tiles = -(-n_asg // MOE_TILE) + N_EXPERTS
    n_used = pad_ends[-1] // MOE_TILE

    tile_start = jnp.arange(n_tiles, dtype=I32) * MOE_TILE
    used = tile_start < pad_ends[-1]
    tile_expert = jnp.minimum(jnp.sum((pad_ends[None, :] <= tile_start[:, None]).astype(I32), 1), N_EXPERTS - 1)
    next_tile = pad_ends[tile_expert] // MOE_TILE
    next_expert = jnp.where(used & (next_tile < n_used), tile_expert[jnp.minimum(next_tile, n_tiles - 1)], -1)
    rank0 = tile_start - pad_starts[tile_expert]
    n_valid = jnp.where(used, jnp.clip(counts[tile_expert] - rank0, 0, MOE_TILE), 0)
    sorted0 = starts[tile_expert] + rank0

    r = jnp.arange(MOE_TILE, dtype=I32)[None, :]
    valid = r < n_valid[:, None]
    asg = order[jnp.clip(sorted0[:, None] + r, 0, n_asg - 1)]
    slot_src = jnp.where(valid, asg // TOP_K, 0).reshape(-1)
    slot_dst = jnp.where(valid, (asg % TOP_K) * n_tok + asg // TOP_K, TOP_K * n_tok + r).reshape(-1)
    return _moe_experts(xp, tile_expert, n_used.reshape(1).astype(I32), next_expert.astype(I32),
                        n_valid.astype(I32), slot_src, slot_dst, w_gate, w_up, w_down, layer)


def kernel(x, a_w_in, a_lower_bound, a_norm_g, a_w_out, b_w_kv, b_w_q, b_sinks, b_w_out, rel_bias,
           moe_w_rg, moe_b_rg, moe_w_re, moe_b_re, moe_w_gate, moe_w_up, moe_w_down, ln_g, ln_b):
    bsz, seq, d = x.shape
    t = bsz * seq
    lb_sm = jax.nn.softmax(a_lower_bound.astype(F32), axis=0)
    lower_bounds = jnp.cumsum(lb_sm, axis=0) - lb_sm[0]
    att_bias = _band_bias_t(rel_bias)
    pad = ROUTER_COLS - N_GROUPS - N_EXPERTS
    w_router = jnp.concatenate([moe_w_rg, moe_w_re, jnp.zeros((DEPTH, d, pad), F32)], axis=-1)
    w_router_hi = w_router.astype(BF16)
    w_router_lo = (w_router - w_router_hi.astype(F32)).astype(BF16)
    w_router = jnp.concatenate([w_router_hi, w_router_lo], axis=-1)
    b_router = jnp.concatenate([moe_b_rg, moe_b_re, jnp.zeros((DEPTH, pad), F32)], axis=-1)

    xf = x.reshape(t, d).astype(F32)
    xb = xf.astype(BF16)
    kv = None
    for layer in range(DEPTH):
        if layer < N_A_LAYERS:
            proj = _matmul(xb, a_w_in, layer, F32)
            o = _hgrn(proj, lower_bounds[layer], a_norm_g[layer], bsz, seq)
            w_out, w_layer = a_w_out, layer
        else:
            j = layer - N_A_LAYERS
            if kv is None:
                kv = _matmul(xb, b_w_kv[None], 0, BF16)
            q = _matmul(xb, b_w_q, j, BF16)
            o = _swa(q, kv, att_bias, b_sinks[j].astype(F32), bsz, seq)
            w_out, w_layer = b_w_out, j
        xf, xp, route, counts = _out_ln_router(o, w_out, w_layer, xf, ln_g[2 * layer], ln_b[2 * layer],
                                               w_router[layer], b_router[layer].reshape(1, ROUTER_COLS))
        y_planes = _hier_moe(xp, route, counts, moe_w_gate, moe_w_up, moe_w_down, layer)
        xf, xb = _ln_moe(xf, y_planes, route, ln_g[2 * layer + 1], ln_b[2 * layer + 1])
    return xf.reshape(bsz, seq, d).astype(x.dtype)
```

```python
import functools
import math

import numpy as np
import jax
import jax.numpy as jnp
from jax import lax
from jax.experimental import pallas as pl
from jax.experimental.pallas import tpu as pltpu

F32 = jnp.float32
BF16 = jnp.bfloat16
U32 = jnp.uint32
I32 = jnp.int32

D_MODEL = 2048
DEPTH = 4
N_A_LAYERS = DEPTH // 2
HG_HEADS = 16
HG_DK = 128
HG_DV = 128
ATT_HEAD_DIM = 64
ATT_Q_HEADS = 32
ATT_KV_HEADS = 4
ATT_GROUP = ATT_Q_HEADS // ATT_KV_HEADS
WINDOW = 128
N_BUCKETS = 32
REL_MAX_DISTANCE = 128
N_GROUPS = 4
EXPERTS_PER_GROUP = 8
N_EXPERTS = N_GROUPS * EXPERTS_PER_GROUP
TOP_K = 2
D_EXPERT = D_MODEL // 4
DEEPNORM_ALPHA = (2 * DEPTH) ** 0.25
LN_EPS = 1e-5
RMS_EPS = 1e-6
NEG_BIG = -1e30
MIN_FORGET = 1e-30
LOG2_E = 1.0 / math.log(2.0)

LANES = 128
HG_CHUNK = 128
HG_LEVELS = 7
HG_HEADS_PER_STEP = 16
HG_SUB = 32
HG_SAFE_LOG2_DECAY = 100.0
MOE_TILE = 256
OUT_LN_CHUNK = 512
ROUTER_COLS = 128
SUBLANES = 8
SLAB = (SUBLANES, LANES)
assert D_MODEL // 2 == SUBLANES * LANES
VMEM_LIMIT = 56 * 1024 * 1024
MOE_VMEM_LIMIT = 60 * 1024 * 1024
NT_DIMS = (((1,), (1,)), ((), ()))


def _params(*sem):
    return pltpu.CompilerParams(dimension_semantics=sem, vmem_limit_bytes=VMEM_LIMIT)


def _pack_bf16_pairs(y):
    half = y.shape[1] // 2
    lo = lax.bitcast_convert_type(y[:, :half].astype(BF16).astype(F32), U32)
    hi = lax.bitcast_convert_type(y[:, half:].astype(BF16).astype(F32), U32)
    w = (hi & jnp.uint32(0xFFFF0000)) | (lo >> 16)
    return jnp.swapaxes(jnp.stack([w[:, s * LANES:(s + 1) * LANES] for s in range(SUBLANES)], axis=0), 0, 1)


def _unpack_bf16_pairs(slabs):
    cols = jnp.swapaxes(slabs, 0, 1)
    w = jnp.concatenate([cols[s] for s in range(SUBLANES)], axis=1)
    lo = lax.bitcast_convert_type(w << 16, F32)
    hi = lax.bitcast_convert_type(w & jnp.uint32(0xFFFF0000), F32)
    return jnp.concatenate([lo, hi], axis=1)


def _mm_kernel(x_ref, w_ref, o_ref, wb_ref):
    @pl.when(pl.program_id(1) == 0)
    def _():
        wb_ref[...] = w_ref[...].astype(BF16)

    o_ref[...] = jnp.dot(x_ref[...], wb_ref[...], preferred_element_type=F32).astype(o_ref.dtype)


def _matmul(x, w, layer, out_dtype, tm=1024, tn=1024):
    m, k = x.shape
    n = w.shape[2]
    tm, tn = min(tm, m), min(tn, n)
    return pl.pallas_call(
        _mm_kernel,
        grid=(n // tn, m // tm),
        in_specs=[pl.BlockSpec((tm, k), lambda j, i: (i, 0)),
                  pl.BlockSpec((None, k, tn), lambda j, i: (layer, 0, j))],
        out_specs=pl.BlockSpec((tm, tn), lambda j, i: (i, j)),
        out_shape=jax.ShapeDtypeStruct((m, n), out_dtype),
        scratch_shapes=[pltpu.VMEM((k, tn), BF16)],
        compiler_params=_params("arbitrary", "arbitrary"),
        name="matmul",
    )(x, w)


def _layernorm(v, g, b):
    mu = jnp.mean(v, -1, keepdims=True)
    d = v - mu
    var = jnp.mean(d * d, -1, keepdims=True)
    return d * lax.rsqrt(var + LN_EPS) * g + b


def _route(logits):
    lane = lax.broadcasted_iota(I32, logits.shape, 1)
    big = jnp.int32(ROUTER_COLS)
    is_grp = lane < N_GROUPS
    gl = jnp.where(is_grp, logits, -jnp.inf)
    g_max = jnp.max(gl, -1, keepdims=True)
    grp = jnp.min(jnp.where(is_grp & (gl == g_max), lane, big), -1, keepdims=True)
    p_grp = 1.0 / jnp.sum(jnp.exp(gl - g_max), -1, keepdims=True)
    in_grp = (lane >= N_GROUPS) & (((lane - N_GROUPS) >> 3) == grp)
    el = jnp.where(in_grp, logits, -jnp.inf)
    v1 = jnp.max(el, -1, keepdims=True)
    i1 = jnp.min(jnp.where(in_grp & (el == v1), lane, big), -1, keepdims=True)
    el2 = jnp.where(lane == i1, -jnp.inf, el)
    v2 = jnp.max(el2, -1, keepdims=True)
    i2 = jnp.min(jnp.where(in_grp & (lane != i1) & (el2 == v2), lane, big), -1, keepdims=True)
    ex = jnp.exp(v2 - v1)
    w1 = 1.0 / (1.0 + ex)
    return i1 - N_GROUPS, i2 - N_GROUPS, p_grp * w1, p_grp * (ex * w1)


def _out_ln_router_kernel(layer, o_ref, w_hbm, x_ref, g_ref, b_ref, wr_ref, br_ref,
                          xo_ref, xp_ref, rt_ref, cnt_ref, ws_ref, wb_ref, w_sem):
    @pl.when(pl.program_id(0) == 0)
    def _():
        load_w = pltpu.make_async_copy(w_hbm.at[layer], ws_ref, w_sem.at[0])
        load_w.start()
        cnt_ref[...] = jnp.zeros_like(cnt_ref)
        load_w.wait()
        wb_ref[...] = ws_ref[...].astype(BF16)

    tm = o_ref.shape[0]
    chunk = min(OUT_LN_CHUNK, tm)
    for r0 in range(0, tm, chunk):
        rows = slice(r0, r0 + chunk)
        h = jnp.dot(o_ref[rows, :], wb_ref[...], preferred_element_type=F32)
        y = _layernorm(DEEPNORM_ALPHA * x_ref[rows, :] + h, g_ref[...], b_ref[...])
        xo_ref[rows, :] = y
        xp_ref[rows] = _pack_bf16_pairs(y)
        y_hi = y.astype(BF16)
        y_lo = (y - y_hi.astype(F32)).astype(BF16)
        t_hi = jnp.dot(y_hi, wr_ref[...], preferred_element_type=F32)
        t_lo = jnp.dot(y_lo, wr_ref[:, :ROUTER_COLS], preferred_element_type=F32)
        logits = t_hi[:, :ROUTER_COLS] + t_hi[:, ROUTER_COLS:] + t_lo + br_ref[...]
        e1, e2, g1, g2 = _route(logits)
        lane = lax.broadcasted_iota(I32, logits.shape, 1)
        rt_ref[rows, :] = jnp.where(lane == 0, e1.astype(F32),
                                    jnp.where(lane == 1, e2.astype(F32),
                                              jnp.where(lane == 2, g1, jnp.where(lane == 3, g2, 0.0))))
        hits = (lane == e1).astype(F32) + (lane == e2).astype(F32)
        cnt_ref[...] += jnp.sum(hits, 0, keepdims=True)


def _out_ln_router(o, w, layer, x, g, b, wr, br, tm=512):
    t, d = x.shape
    tm = min(tm, t)
    row = lambda i: (i, 0)
    const = lambda i: (0, 0)
    return pl.pallas_call(
        functools.partial(_out_ln_router_kernel, layer),
        grid=(t // tm,),
        in_specs=[pl.BlockSpec((tm, d), row), pl.BlockSpec(memory_space=pl.ANY), pl.BlockSpec((tm, d), row),
                  pl.BlockSpec((1, d), const), pl.BlockSpec((1, d), const),
                  pl.BlockSpec((d, 2 * ROUTER_COLS), const), pl.BlockSpec((1, ROUTER_COLS), const)],
        out_specs=[pl.BlockSpec((tm, d), row), pl.BlockSpec((tm,) + SLAB, lambda i: (i, 0, 0)),
                   pl.BlockSpec((tm, ROUTER_COLS), row), pl.BlockSpec((1, ROUTER_COLS), const)],
        out_shape=[jax.ShapeDtypeStruct((t, d), F32), jax.ShapeDtypeStruct((t,) + SLAB, U32),
                   jax.ShapeDtypeStruct((t, ROUTER_COLS), F32), jax.ShapeDtypeStruct((1, ROUTER_COLS), F32)],
        scratch_shapes=[pltpu.VMEM((d, d), F32), pltpu.VMEM((d, d), BF16), pltpu.SemaphoreType.DMA((1,))],
        compiler_params=_params("arbitrary"),
        name="out_ln_router",
    )(o, w, x, g.reshape(1, d), b.reshape(1, d), wr, br)


def _ln_moe_kernel(x_ref, y0_ref, y1_ref, rt_ref, g_ref, b_ref, xo_ref, xb_ref):
    rt = rt_ref[...]
    f = rt[:, 2:3] * _unpack_bf16_pairs(y0_ref[...]) + rt[:, 3:4] * _unpack_bf16_pairs(y1_ref[...])
    y = _layernorm(DEEPNORM_ALPHA * x_ref[...] + f, g_ref[...], b_ref[...])
    xo_ref[...] = y
    xb_ref[...] = y.astype(BF16)


def _ln_moe(x, y_planes, route, g, b, tm=512):
    t, d = x.shape
    tm = min(tm, t)
    row = lambda i: (i, 0)
    const = lambda i: (0, 0)
    return pl.pallas_call(
        _ln_moe_kernel,
        grid=(t // tm,),
        in_specs=[pl.BlockSpec((tm, d), row), pl.BlockSpec((tm,) + SLAB, lambda i: (i, 0, 0)),
                  pl.BlockSpec((tm,) + SLAB, lambda i: (i + t // tm, 0, 0)),
                  pl.BlockSpec((tm, ROUTER_COLS), row),
                  pl.BlockSpec((1, d), const), pl.BlockSpec((1, d), const)],
        out_specs=[pl.BlockSpec((tm, d), row), pl.BlockSpec((tm, d), row)],
        out_shape=[jax.ShapeDtypeStruct((t, d), F32), jax.ShapeDtypeStruct((t, d), BF16)],
        compiler_params=_params("arbitrary"),
        name="ln_moe",
    )(x, y_planes, y_planes, route, g.reshape(1, d), b.reshape(1, d))


def _hgrn_constants():
    c = HG_CHUNK
    t = np.arange(c)[:, None]
    s = np.arange(c)[None, :]
    masks = [(s == t).astype(np.float32)]
    for lvl in range(HG_LEVELS):
        m = 1 << lvl
        masks.append(((t // (2 * m) == s // (2 * m)) & ((t // m) % 2 == 1) & ((s // m) % 2 == 0)).astype(np.float32))
    sub = ((t // HG_SUB == s // HG_SUB) & (s <= t)).astype(np.float32)
    return (s <= t).astype(np.float32), np.stack(masks + [sub], axis=0)


def _rows_broadcast(b, first, period, reps):
    n = b.shape[0] // period
    return jnp.concatenate([jnp.broadcast_to(b[first + j * period:first + j * period + 1, :], (reps, b.shape[1]))
                            for j in range(n) for _ in range(period // reps)], axis=0)


def _level_decay(b, fclip, row, lvl):
    if lvl == 0:
        return jnp.where((row & 1) == 1, fclip, 1.0)
    m = 1 << lvl
    if 2 * m < 8:
        ref = jnp.where((row & m * 2) == 0, _rows_broadcast(b, m - 1, 8, 8), _rows_broadcast(b, 3 * m - 1, 8, 8))
    else:
        ref = _rows_broadcast(b, m - 1, 2 * m, 2 * m)
    return jnp.exp2(-jnp.abs(b - ref))


class _RowBlocks:
    def __init__(self, n):
        self.tiles = [None] * (n // SUBLANES)

    def add(self, first_row, val):
        for u in range(val.shape[0] // SUBLANES):
            piece = val[u * SUBLANES:(u + 1) * SUBLANES]
            i = first_row // SUBLANES + u
            self.tiles[i] = piece if self.tiles[i] is None else self.tiles[i] + piece

    def value(self):
        return jnp.concatenate(self.tiles, axis=0)


def _hgrn_scores(q, k, b, fclip, row, pm_ref, first_level):
    c = HG_CHUNK
    scores = _RowBlocks(c)
    if first_level == 0:
        scores.add(0, pm_ref[0] * lax.dot_general(q.astype(BF16), k.astype(BF16), NT_DIMS, preferred_element_type=F32))
    for lvl in range(first_level, HG_LEVELS):
        e = _level_decay(b, fclip, row, lvl)
        m = 1 << lvl
        if m < SUBLANES:
            s_l = lax.dot_general((q * e).astype(BF16), (k * e).astype(BF16), NT_DIMS, preferred_element_type=F32)
            scores.add(0, pm_ref[lvl + 1] * s_l)
        else:
            up = [slice(u * 2 * m + m, (u + 1) * 2 * m) for u in range(c // (2 * m))]
            lo = [slice(u * 2 * m, u * 2 * m + m) for u in range(c // (2 * m))]
            qe = jnp.concatenate([q[r] * e[r] for r in up], axis=0).astype(BF16)
            ke = jnp.concatenate([piece for r in lo for piece in (k[r] * e[r], jnp.zeros((m, LANES), F32))],
                                 axis=0).astype(BF16)
            s_l = lax.dot_general(qe, ke, NT_DIMS, preferred_element_type=F32)
            for u, r in enumerate(up):
                part = s_l[u * m:(u + 1) * m]
                scores.add(r.start, part if len(up) == 1 else pm_ref[lvl + 1, r, :] * part)
    return scores


def _sub_block_start(b):
    n = b.shape[0] // HG_SUB
    pieces = [jnp.zeros((HG_SUB, b.shape[1]), F32)]
    pieces += [jnp.broadcast_to(b[u * HG_SUB - 1:u * HG_SUB, :], (HG_SUB, b.shape[1])) for u in range(1, n)]
    return jnp.concatenate(pieces, axis=0)


def _hgrn_kernel(q_ref, f_ref, i_ref, g_ref, lb_ref, ng_ref, lm_ref, pm_ref, o_ref, st_ref, k_ref, b_ref, fc_ref):
    c = HG_CHUNK
    nh = HG_HEADS_PER_STEP

    @pl.when(pl.program_id(2) == 0)
    def _():
        st_ref[...] = jnp.zeros_like(st_ref)

    worst = jnp.zeros((1, LANES), F32)
    for j in range(nh):
        cols = slice(j * LANES, (j + 1) * LANES)
        lb = lb_ref[:, cols]
        sig = jax.nn.sigmoid(f_ref[:, cols])
        fclip = jnp.maximum(lb + (1.0 - lb) * sig, MIN_FORGET)
        logf = jnp.log(fclip) * LOG2_E
        g_hi = logf.astype(BF16)
        g_lo = (logf - g_hi.astype(F32)).astype(BF16)
        b2 = jnp.dot(lm_ref[...], jnp.concatenate([g_hi, g_lo], axis=1), preferred_element_type=F32)
        b = b2[:, :LANES] + b2[:, LANES:]
        k_ref[:, cols] = (1.0 - lb) * (1.0 - sig)
        b_ref[:, cols] = b
        fc_ref[:, cols] = fclip
        for u in range(c // HG_SUB):
            start = b[u * HG_SUB - 1:u * HG_SUB, :] if u else jnp.zeros((1, LANES), F32)
            worst = jnp.maximum(worst, start - b[(u + 1) * HG_SUB - 1:(u + 1) * HG_SUB, :])
    safe = jnp.max(worst) < HG_SAFE_LOG2_DECAY

    def heads(direct):
        row = lax.broadcasted_iota(I32, (c, LANES), 0)
        for j in range(nh):
            cols = slice(j * LANES, (j + 1) * LANES)
            q = q_ref[:, cols]
            v = i_ref[:, cols]
            k = k_ref[:, cols]
            b = b_ref[:, cols]
            b_last = b[c - 1:c, :]

            st = st_ref[j]
            o = lax.dot_general((q * jnp.exp2(b)).astype(BF16), st.astype(BF16), NT_DIMS, preferred_element_type=F32)

            if direct:
                rel = b - _sub_block_start(b)
                s_d = lax.dot_general((q * jnp.exp2(rel)).astype(BF16), (k * jnp.exp2(-rel)).astype(BF16),
                                      NT_DIMS, preferred_element_type=F32)
                scores = _hgrn_scores(q, k, b, None, row, pm_ref, HG_SUB.bit_length() - 1)
                scores.add(0, jnp.where(pm_ref[HG_LEVELS + 1] > 0.5, s_d, 0.0))
            else:
                scores = _hgrn_scores(q, k, b, fc_ref[:, cols], row, pm_ref, 0)
            o = o + jnp.dot(scores.value().astype(BF16), v.astype(BF16), preferred_element_type=F32)

            ks = (k * jnp.exp2(b_last - b)).astype(BF16)
            st_ref[j] = st * jnp.exp2(b_last) + jnp.dot(v.T.astype(BF16), ks, preferred_element_type=F32)

            o = o * lax.rsqrt(jnp.mean(o * o, -1, keepdims=True) + RMS_EPS) * ng_ref[:, cols]
            gate = g_ref[:, cols]
            o_ref[:, cols] = (o * (gate * jax.nn.sigmoid(gate))).astype(o_ref.dtype)

    @pl.when(safe)
    def _():
        heads(True)

    @pl.when(jnp.logical_not(safe))
    def _():
        heads(False)


def _hgrn(proj, lb, norm_g, bsz, seq):
    t = bsz * seq
    nc = seq // HG_CHUNK
    c = HG_CHUNK
    nh = HG_HEADS_PER_STEP
    w = nh * HG_DK
    hsteps = HG_HEADS // nh
    lm, pm = _hgrn_constants()

    def part(p):
        return pl.BlockSpec((c, w), lambda b, h, n: (b * nc + n, p * hsteps + h))

    head = pl.BlockSpec((1, w), lambda b, h, n: (0, h))
    return pl.pallas_call(
        _hgrn_kernel,
        grid=(bsz, hsteps, nc),
        in_specs=[part(0), part(1), part(2), part(3), head, head,
                  pl.BlockSpec((c, c), lambda b, h, n: (0, 0)),
                  pl.BlockSpec((HG_LEVELS + 2, c, c), lambda b, h, n: (0, 0, 0))],
        out_specs=pl.BlockSpec((c, w), lambda b, h, n: (b * nc + n, h)),
        out_shape=jax.ShapeDtypeStruct((t, D_MODEL), BF16),
        scratch_shapes=[pltpu.VMEM((nh, HG_DV, HG_DK), F32)] + [pltpu.VMEM((c, w), F32)] * 3,
        compiler_params=_params("arbitrary", "arbitrary", "arbitrary"),
        name="hgrn2",
    )(proj, proj, proj, proj, lb.reshape(1, D_MODEL), norm_g.reshape(1, D_MODEL),
      jnp.asarray(lm, BF16), jnp.asarray(pm, F32))


def _swa_kernel(sink_ref, q_ref, kvp_ref, kvc_ref, bias_ref, o_ref):
    w = WINDOW
    hd = ATT_HEAD_DIM
    kvw = ATT_KV_HEADS * hd
    ones = jnp.ones((2 * w, hd), F32)
    for g in range(ATT_KV_HEADS):
        kwin = jnp.concatenate([kvp_ref[:, g * hd:(g + 1) * hd], kvc_ref[:, g * hd:(g + 1) * hd]], axis=0)
        vwin = jnp.concatenate([kvp_ref[:, kvw + g * hd:kvw + (g + 1) * hd],
                                kvc_ref[:, kvw + g * hd:kvw + (g + 1) * hd]], axis=0)
        heads = range(g * ATT_GROUP, (g + 1) * ATT_GROUP)
        qg = jnp.concatenate([q_ref[:, h * hd:(h + 1) * hd] for h in heads], axis=0) * (hd ** -0.5)
        sink = jnp.concatenate([jnp.full((1, w), sink_ref[h], F32) for h in heads], axis=1)
        s = lax.dot_general(kwin, qg.astype(BF16), NT_DIMS, preferred_element_type=F32) + bias_ref[g]
        m = jnp.maximum(jnp.max(s, 0, keepdims=True), sink)
        p = jnp.exp(s - m).astype(BF16)
        vext_t = jnp.concatenate([vwin.astype(F32), ones], axis=1).T.astype(BF16)
        ov = jnp.dot(vext_t, p, preferred_element_type=F32)
        ov = ov / (ov[hd:hd + 1, :] + jnp.exp(sink - m))
        og = ov.T.astype(o_ref.dtype)
        o_ref[:, g * ATT_GROUP * hd:(g + 1) * ATT_GROUP * hd] = jnp.concatenate(
            [og[j * w:(j + 1) * w, :hd] for j in range(ATT_GROUP)], axis=1)


def _swa(q, kv, bias, sinks, bsz, seq):
    t = bsz * seq
    nb = seq // WINDOW
    kvw2 = 2 * ATT_KV_HEADS * ATT_HEAD_DIM
    return pl.pallas_call(
        _swa_kernel,
        grid=(bsz, nb),
        in_specs=[pl.BlockSpec(memory_space=pltpu.SMEM),
                  pl.BlockSpec((WINDOW, D_MODEL), lambda b, n: (b * nb + n, 0)),
                  pl.BlockSpec((WINDOW, kvw2), lambda b, n: (b * nb + jnp.maximum(n - 1, 0), 0)),
                  pl.BlockSpec((WINDOW, kvw2), lambda b, n: (b * nb + n, 0)),
                  pl.BlockSpec((None, ATT_KV_HEADS, 2 * WINDOW, ATT_GROUP * WINDOW),
                               lambda b, n: (jnp.minimum(n, 1), 0, 0, 0))],
        out_specs=pl.BlockSpec((WINDOW, D_MODEL), lambda b, n: (b * nb + n, 0)),
        out_shape=jax.ShapeDtypeStruct((t, D_MODEL), BF16),
        compiler_params=_params("arbitrary", "arbitrary"),
        name="swa",
    )(sinks, q, kv, kv, bias)


def _t5_bucket(dist):
    n = jnp.clip(dist, 0, REL_MAX_DISTANCE - 1)
    max_exact = N_BUCKETS // 2
    large = max_exact + (jnp.log(jnp.maximum(n, max_exact).astype(F32) / max_exact)
                         / math.log(REL_MAX_DISTANCE / max_exact)
                         * (N_BUCKETS - max_exact)).astype(I32)
    large = jnp.minimum(large, N_BUCKETS - 1)
    return jnp.where(n < max_exact, n, large)


def _band_bias_t(rel_bias):
    kj = jnp.arange(2 * WINDOW)[:, None]
    qi = jnp.arange(WINDOW)[None, :]
    bucket = _t5_bucket(qi + WINDOW - kj).reshape(-1)
    onehot = (bucket[:, None] == jnp.arange(N_BUCKETS)[None, :]).astype(F32)
    table = jnp.dot(onehot, rel_bias.astype(F32), precision=lax.Precision.HIGHEST)
    table = table.reshape(2 * WINDOW, WINDOW, ATT_KV_HEADS, ATT_GROUP)
    table = table.transpose(2, 0, 3, 1)
    dist = (qi + WINDOW - kj)[None, :, None, :]
    in_window = (dist >= 0) & (dist < WINDOW)
    own_block = (kj >= WINDOW)[None, :, None, :]
    both = jnp.stack([jnp.where(in_window & own_block, table, NEG_BIG), jnp.where(in_window, table, NEG_BIG)])
    return both.reshape(2, ATT_KV_HEADS, 2 * WINDOW, ATT_GROUP * WINDOW)


def _moe_kernel(layer, te_ref, nu_ref, nx_ref, nv_ref, src_ref, dst_ref,
                x_hbm, wg_hbm, wu_hbm, wd_hbm, y_hbm,
                xres, xbuf, ybuf, wgs_ref, wus_ref, wds_ref, wgb_ref, wub_ref, wdb_ref, in_sem, out_sem, w_sem):
    i = pl.program_id(0)
    n_used = nu_ref[0]
    rows = MOE_TILE
    half = MOE_TILE // 2

    def wait_scatter(n):
        pltpu.make_async_copy(ybuf.at[pl.ds(0, n)], ybuf.at[pl.ds(0, n)], out_sem.at[0]).wait()

    def wait_scatter_of(tile):
        @pl.when(nv_ref[tile] <= half)
        def _():
            wait_scatter(half)

        @pl.when(nv_ref[tile] > half)
        def _():
            wait_scatter(rows)

    def weight_copies(e):
        return (pltpu.make_async_copy(wg_hbm.at[layer, e], wgs_ref, w_sem.at[0]),
                pltpu.make_async_copy(wu_hbm.at[layer, e], wus_ref, w_sem.at[1]),
                pltpu.make_async_copy(wd_hbm.at[layer, e], wds_ref, w_sem.at[2]))

    def run_tile(n):
        for r in range(n):
            xbuf[r] = xres[src_ref[i * rows + r]]
        x = _unpack_bf16_pairs(xbuf[pl.ds(0, n)]).astype(BF16)
        hg = jnp.dot(x, wgb_ref[...], preferred_element_type=F32)
        hu = jnp.dot(x, wub_ref[...], preferred_element_type=F32)
        hidden = (hg * jax.nn.sigmoid(hg) * hu).astype(BF16)
        y = jnp.dot(hidden, wdb_ref[...], preferred_element_type=F32)

        @pl.when(i > 0)
        def _():
            wait_scatter_of(i - 1)

        ybuf[pl.ds(0, n)] = _pack_bf16_pairs(y)
        for r in range(n):
            pltpu.make_async_copy(ybuf.at[r], y_hbm.at[dst_ref[i * rows + r]], out_sem.at[0]).start(priority=r % 2)

    @pl.when(i < n_used)
    def _():
        @pl.when(i == 0)
        def _():
            for cp in weight_copies(te_ref[0]):
                cp.start()
            load_x = pltpu.make_async_copy(x_hbm, xres, in_sem.at[0])
            load_x.start()
            ybuf[...] = jnp.zeros_like(ybuf)
            dump = pltpu.make_async_copy(ybuf, y_hbm.at[pl.ds(y_hbm.shape[0] - rows, rows)], out_sem.at[0])
            dump.start()
            dump.wait()
            load_x.wait()

        @pl.when((i == 0) | (te_ref[i] != te_ref[jnp.maximum(i - 1, 0)]))
        def _():
            for cp in weight_copies(te_ref[i]):
                cp.wait()
            wgb_ref[...] = wgs_ref[...].astype(BF16)
            wub_ref[...] = wus_ref[...].astype(BF16)
            wdb_ref[...] = wds_ref[...].astype(BF16)

            @pl.when(nx_ref[i] >= 0)
            def _():
                for cp in weight_copies(nx_ref[i]):
                    cp.start()

        @pl.when(nv_ref[i] <= half)
        def _():
            run_tile(half)

        @pl.when(nv_ref[i] > half)
        def _():
            run_tile(rows)

        @pl.when(i == n_used - 1)
        def _():
            wait_scatter_of(i)


def _moe_experts(xp, tile_expert, n_used, next_expert, n_valid, slot_src, slot_dst, w_gate, w_up, w_down, layer):
    t = xp.shape[0]
    d = D_MODEL
    n_tiles = tile_expert.shape[0]
    any_spec = pl.BlockSpec(memory_space=pl.ANY)
    return pl.pallas_call(
        functools.partial(_moe_kernel, layer),
        grid_spec=pltpu.PrefetchScalarGridSpec(
            num_scalar_prefetch=6,
            grid=(n_tiles,),
            in_specs=[any_spec, any_spec, any_spec, any_spec],
            out_specs=any_spec,
            scratch_shapes=[pltpu.VMEM((t,) + SLAB, U32),
                            pltpu.VMEM((MOE_TILE,) + SLAB, U32), pltpu.VMEM((MOE_TILE,) + SLAB, U32),
                            pltpu.VMEM((d, D_EXPERT), F32), pltpu.VMEM((d, D_EXPERT), F32),
                            pltpu.VMEM((D_EXPERT, d), F32),
                            pltpu.VMEM((d, D_EXPERT), BF16), pltpu.VMEM((d, D_EXPERT), BF16),
                            pltpu.VMEM((D_EXPERT, d), BF16),
                            pltpu.SemaphoreType.DMA((1,)), pltpu.SemaphoreType.DMA((1,)),
                            pltpu.SemaphoreType.DMA((3,))]),
        out_shape=jax.ShapeDtypeStruct((TOP_K * t + MOE_TILE,) + SLAB, U32),
        compiler_params=pltpu.CompilerParams(dimension_semantics=("arbitrary",), vmem_limit_bytes=MOE_VMEM_LIMIT),
        name="moe_experts",
    )(tile_expert, n_used, next_expert, n_valid, slot_src, slot_dst, xp, w_gate, w_up, w_down)


def _hier_moe(xp, route, counts, w_gate, w_up, w_down, layer):
    n_tok = xp.shape[0]
    n_asg = n_tok * TOP_K
    e_flat = route[:, :TOP_K].astype(I32).reshape(-1)
    counts = counts[0, :N_EXPERTS].astype(I32)
    padded = (counts + MOE_TILE - 1) // MOE_TILE * MOE_TILE
    pad_ends = jnp.cumsum(padded)
    pad_starts = pad_ends - padded
    n_tiles = -(-n_asg // MOE_TILE) + N_EXPERTS
    n_slots = n_tiles * MOE_TILE
    n_used = pad_ends[-1] // MOE_TILE

    fill_ends = jnp.cumsum(padded - counts)
    filler = jnp.arange(n_slots - n_asg, dtype=I32)
    filler_key = jnp.sum((fill_ends[None, :] <= filler[:, None]).astype(I32), 1)
    keys = jnp.concatenate([e_flat, filler_key])
    ids = jnp.concatenate([jnp.arange(n_asg, dtype=I32), jnp.full((n_slots - n_asg,), -1, I32)])
    _, asg = lax.sort((keys, ids), num_keys=1)
    valid = asg >= 0
    slot_src = jnp.where(valid, asg // TOP_K, 0)
    slot_dst = jnp.where(valid, (asg % TOP_K) * n_tok + asg // TOP_K,
                         TOP_K * n_tok + jnp.arange(n_slots, dtype=I32) % MOE_TILE)

    tile_start = jnp.arange(n_tiles, dtype=I32) * MOE_TILE
    used = tile_start < pad_ends[-1]
    tile_expert = jnp.minimum(jnp.sum((pad_ends[None, :] <= tile_start[:, None]).astype(I32), 1), N_EXPERTS - 1)
    next_tile = pad_ends[tile_expert] // MOE_TILE
    next_expert = jnp.where(used & (next_tile < n_used), tile_expert[jnp.minimum(next_tile, n_tiles - 1)], -1)
    rank0 = tile_start - pad_starts[tile_expert]
    n_valid = jnp.where(used, jnp.clip(counts[tile_expert] - rank0, 0, MOE_TILE), 0)
    return _moe_experts(xp, tile_expert, n_used.reshape(1).astype(I32), next_expert.astype(I32),
                        n_valid.astype(I32), slot_src, slot_dst, w_gate, w_up, w_down, layer)


def kernel(x, a_w_in, a_lower_bound, a_norm_g, a_w_out, b_w_kv, b_w_q, b_sinks, b_w_out, rel_bias,
           moe_w_rg, moe_b_rg, moe_w_re, moe_b_re, moe_w_gate, moe_w_up, moe_w_down, ln_g, ln_b):
    bsz, seq, d = x.shape
    t = bsz * seq
    lb_sm = jax.nn.softmax(a_lower_bound.astype(F32), axis=0)
    lower_bounds = jnp.cumsum(lb_sm, axis=0) - lb_sm[0]
    att_bias = _band_bias_t(rel_bias)
    pad = ROUTER_COLS - N_GROUPS - N_EXPERTS
    w_router = jnp.concatenate([moe_w_rg, moe_w_re, jnp.zeros((DEPTH, d, pad), F32)], axis=-1)
    w_router_hi = w_router.astype(BF16)
    w_router_lo = (w_router - w_router_hi.astype(F32)).astype(BF16)
    w_router = jnp.concatenate([w_router_hi, w_router_lo], axis=-1)
    b_router = jnp.concatenate([moe_b_rg, moe_b_re, jnp.zeros((DEPTH, pad), F32)], axis=-1)

    xf = x.reshape(t, d).astype(F32)
    xb = xf.astype(BF16)
    kv = None
    for layer in range(DEPTH):
        if layer < N_A_LAYERS:
            proj = _matmul(xb, a_w_in, layer, F32)
            o = _hgrn(proj, lower_bounds[layer], a_norm_g[layer], bsz, seq)
            w_out, w_layer = a_w_out, layer
        else:
            j = layer - N_A_LAYERS
            if kv is None:
                kv = _matmul(xb, b_w_kv[None], 0, BF16)
            q = _matmul(xb, b_w_q, j, BF16)
            o = _swa(q, kv, att_bias, b_sinks[j].astype(F32), bsz, seq)
            w_out, w_layer = b_w_out, j
        xf, xp, route, counts = _out_ln_router(o, w_out, w_layer, xf, ln_g[2 * layer], ln_b[2 * layer],
                                               w_router[layer], b_router[layer].reshape(1, ROUTER_COLS))
        y_planes = _hier_moe(xp, route, counts, moe_w_gate, moe_w_up, moe_w_down, layer)
        xf, xb = _ln_moe(xf, y_planes, route, ln_g[2 * layer + 1], ln_b[2 * layer + 1])
    return xf.reshape(bsz, seq, d).astype(x.dtype)
```

```python
import functools
import math

import numpy as np
import jax
import jax.numpy as jnp
from jax import lax
from jax.experimental import pallas as pl
from jax.experimental.pallas import tpu as pltpu

F32 = jnp.float32
BF16 = jnp.bfloat16
U32 = jnp.uint32
I32 = jnp.int32

D_MODEL = 2048
DEPTH = 4
N_A_LAYERS = DEPTH // 2
HG_HEADS = 16
HG_DK = 128
HG_DV = 128
ATT_HEAD_DIM = 64
ATT_Q_HEADS = 32
ATT_KV_HEADS = 4
ATT_GROUP = ATT_Q_HEADS // ATT_KV_HEADS
WINDOW = 128
N_BUCKETS = 32
REL_MAX_DISTANCE = 128
N_GROUPS = 4
EXPERTS_PER_GROUP = 8
N_EXPERTS = N_GROUPS * EXPERTS_PER_GROUP
TOP_K = 2
D_EXPERT = D_MODEL // 4
DEEPNORM_ALPHA = (2 * DEPTH) ** 0.25
LN_EPS = 1e-5
RMS_EPS = 1e-6
NEG_BIG = -1e30
MIN_FORGET = 1e-30
LOG2_E = 1.0 / math.log(2.0)

LANES = 128
HG_CHUNK = 128
HG_LEVELS = 7
HG_HEADS_PER_STEP = 16
HG_SUB = 64
HG_SAFE_LOG2_DECAY = 120.0
MOE_TILE = 256
OUT_LN_CHUNK = 512
ROUTER_COLS = 128
SUBLANES = 8
SLAB = (SUBLANES, LANES)
assert D_MODEL // 2 == SUBLANES * LANES
VMEM_LIMIT = 56 * 1024 * 1024
MOE_VMEM_LIMIT = 60 * 1024 * 1024
NT_DIMS = (((1,), (1,)), ((), ()))


def _params(*sem):
    return pltpu.CompilerParams(dimension_semantics=sem, vmem_limit_bytes=VMEM_LIMIT)


def _pack_bf16_pairs(y):
    half = y.shape[1] // 2
    lo = lax.bitcast_convert_type(y[:, :half].astype(BF16).astype(F32), U32)
    hi = lax.bitcast_convert_type(y[:, half:].astype(BF16).astype(F32), U32)
    w = (hi & jnp.uint32(0xFFFF0000)) | (lo >> 16)
    return jnp.swapaxes(jnp.stack([w[:, s * LANES:(s + 1) * LANES] for s in range(SUBLANES)], axis=0), 0, 1)


def _unpack_bf16_pairs(slabs):
    cols = jnp.swapaxes(slabs, 0, 1)
    w = jnp.concatenate([cols[s] for s in range(SUBLANES)], axis=1)
    lo = lax.bitcast_convert_type(w << 16, F32)
    hi = lax.bitcast_convert_type(w & jnp.uint32(0xFFFF0000), F32)
    return jnp.concatenate([lo, hi], axis=1)


def _mm_kernel(x_ref, w_ref, o_ref, wb_ref):
    @pl.when(pl.program_id(1) == 0)
    def _():
        wb_ref[...] = w_ref[...].astype(BF16)

    o_ref[...] = jnp.dot(x_ref[...], wb_ref[...], preferred_element_type=F32).astype(o_ref.dtype)


def _matmul(x, w, layer, out_dtype, tm=1024, tn=1024):
    m, k = x.shape
    n = w.shape[2]
    tm, tn = min(tm, m), min(tn, n)
    return pl.pallas_call(
        _mm_kernel,
        grid=(n // tn, m // tm),
        in_specs=[pl.BlockSpec((tm, k), lambda j, i: (i, 0)),
                  pl.BlockSpec((None, k, tn), lambda j, i: (layer, 0, j))],
        out_specs=pl.BlockSpec((tm, tn), lambda j, i: (i, j)),
        out_shape=jax.ShapeDtypeStruct((m, n), out_dtype),
        scratch_shapes=[pltpu.VMEM((k, tn), BF16)],
        compiler_params=_params("arbitrary", "arbitrary"),
        name="matmul",
    )(x, w)


def _layernorm(v, g, b):
    mu = jnp.mean(v, -1, keepdims=True)
    d = v - mu
    var = jnp.mean(d * d, -1, keepdims=True)
    return d * lax.rsqrt(var + LN_EPS) * g + b


def _route(logits):
    lane = lax.broadcasted_iota(I32, logits.shape, 1)
    big = jnp.int32(ROUTER_COLS)
    is_grp = lane < N_GROUPS
    gl = jnp.where(is_grp, logits, -jnp.inf)
    g_max = jnp.max(gl, -1, keepdims=True)
    grp = jnp.min(jnp.where(is_grp & (gl == g_max), lane, big), -1, keepdims=True)
    p_grp = 1.0 / jnp.sum(jnp.exp(gl - g_max), -1, keepdims=True)
    in_grp = (lane >= N_GROUPS) & (((lane - N_GROUPS) >> 3) == grp)
    el = jnp.where(in_grp, logits, -jnp.inf)
    v1 = jnp.max(el, -1, keepdims=True)
    i1 = jnp.min(jnp.where(in_grp & (el == v1), lane, big), -1, keepdims=True)
    el2 = jnp.where(lane == i1, -jnp.inf, el)
    v2 = jnp.max(el2, -1, keepdims=True)
    i2 = jnp.min(jnp.where(in_grp & (lane != i1) & (el2 == v2), lane, big), -1, keepdims=True)
    ex = jnp.exp(v2 - v1)
    w1 = 1.0 / (1.0 + ex)
    return i1 - N_GROUPS, i2 - N_GROUPS, p_grp * w1, p_grp * (ex * w1)


def _out_ln_router_kernel(layer, o_ref, w_hbm, x_ref, g_ref, b_ref, wr_ref, br_ref,
                          xo_ref, xp_ref, rt_ref, cnt_ref, ws_ref, wb_ref, w_sem):
    @pl.when(pl.program_id(0) == 0)
    def _():
        load_w = pltpu.make_async_copy(w_hbm.at[layer], ws_ref, w_sem.at[0])
        load_w.start()
        cnt_ref[...] = jnp.zeros_like(cnt_ref)
        load_w.wait()
        wb_ref[...] = ws_ref[...].astype(BF16)

    tm = o_ref.shape[0]
    chunk = min(OUT_LN_CHUNK, tm)
    for r0 in range(0, tm, chunk):
        rows = slice(r0, r0 + chunk)
        h = jnp.dot(o_ref[rows, :], wb_ref[...], preferred_element_type=F32)
        y = _layernorm(DEEPNORM_ALPHA * x_ref[rows, :] + h, g_ref[...], b_ref[...])
        xo_ref[rows, :] = y
        xp_ref[rows] = _pack_bf16_pairs(y)
        y_hi = y.astype(BF16)
        y_lo = (y - y_hi.astype(F32)).astype(BF16)
        t_hi = jnp.dot(y_hi, wr_ref[...], preferred_element_type=F32)
        t_lo = jnp.dot(y_lo, wr_ref[:, :ROUTER_COLS], preferred_element_type=F32)
        logits = t_hi[:, :ROUTER_COLS] + t_hi[:, ROUTER_COLS:] + t_lo + br_ref[...]
        e1, e2, g1, g2 = _route(logits)
        lane = lax.broadcasted_iota(I32, logits.shape, 1)
        rt_ref[rows, :] = jnp.where(lane == 0, e1.astype(F32),
                                    jnp.where(lane == 1, e2.astype(F32),
                                              jnp.where(lane == 2, g1, jnp.where(lane == 3, g2, 0.0))))
        hits = (lane == e1).astype(F32) + (lane == e2).astype(F32)
        cnt_ref[...] += jnp.sum(hits, 0, keepdims=True)


def _out_ln_router(o, w, layer, x, g, b, wr, br, tm=512):
    t, d = x.shape
    tm = min(tm, t)
    row = lambda i: (i, 0)
    const = lambda i: (0, 0)
    return pl.pallas_call(
        functools.partial(_out_ln_router_kernel, layer),
        grid=(t // tm,),
        in_specs=[pl.BlockSpec((tm, d), row), pl.BlockSpec(memory_space=pl.ANY), pl.BlockSpec((tm, d), row),
                  pl.BlockSpec((1, d), const), pl.BlockSpec((1, d), const),
                  pl.BlockSpec((d, 2 * ROUTER_COLS), const), pl.BlockSpec((1, ROUTER_COLS), const)],
        out_specs=[pl.BlockSpec((tm, d), row), pl.BlockSpec((tm,) + SLAB, lambda i: (i, 0, 0)),
                   pl.BlockSpec((tm, ROUTER_COLS), row), pl.BlockSpec((1, ROUTER_COLS), const)],
        out_shape=[jax.ShapeDtypeStruct((t, d), F32), jax.ShapeDtypeStruct((t,) + SLAB, U32),
                   jax.ShapeDtypeStruct((t, ROUTER_COLS), F32), jax.ShapeDtypeStruct((1, ROUTER_COLS), F32)],
        scratch_shapes=[pltpu.VMEM((d, d), F32), pltpu.VMEM((d, d), BF16), pltpu.SemaphoreType.DMA((1,))],
        compiler_params=_params("arbitrary"),
        name="out_ln_router",
    )(o, w, x, g.reshape(1, d), b.reshape(1, d), wr, br)


def _ln_moe_kernel(x_ref, y0_ref, y1_ref, rt_ref, g_ref, b_ref, xo_ref, xb_ref):
    rt = rt_ref[...]
    f = rt[:, 2:3] * _unpack_bf16_pairs(y0_ref[...]) + rt[:, 3:4] * _unpack_bf16_pairs(y1_ref[...])
    y = _layernorm(DEEPNORM_ALPHA * x_ref[...] + f, g_ref[...], b_ref[...])
    xo_ref[...] = y
    xb_ref[...] = y.astype(BF16)


def _ln_moe(x, y_planes, route, g, b, tm=512):
    t, d = x.shape
    tm = min(tm, t)
    row = lambda i: (i, 0)
    const = lambda i: (0, 0)
    return pl.pallas_call(
        _ln_moe_kernel,
        grid=(t // tm,),
        in_specs=[pl.BlockSpec((tm, d), row), pl.BlockSpec((tm,) + SLAB, lambda i: (i, 0, 0)),
                  pl.BlockSpec((tm,) + SLAB, lambda i: (i + t // tm, 0, 0)),
                  pl.BlockSpec((tm, ROUTER_COLS), row),
                  pl.BlockSpec((1, d), const), pl.BlockSpec((1, d), const)],
        out_specs=[pl.BlockSpec((tm, d), row), pl.BlockSpec((tm, d), row)],
        out_shape=[jax.ShapeDtypeStruct((t, d), F32), jax.ShapeDtypeStruct((t, d), BF16)],
        compiler_params=_params("arbitrary"),
        name="ln_moe",
    )(x, y_planes, y_planes, route, g.reshape(1, d), b.reshape(1, d))


def _hgrn_constants():
    c = HG_CHUNK
    t = np.arange(c)[:, None]
    s = np.arange(c)[None, :]
    masks = [(s == t).astype(np.float32)]
    for lvl in range(HG_LEVELS):
        m = 1 << lvl
        masks.append(((t // (2 * m) == s // (2 * m)) & ((t // m) % 2 == 1) & ((s // m) % 2 == 0)).astype(np.float32))
    sub = ((t // HG_SUB == s // HG_SUB) & (s <= t)).astype(np.float32)
    return (s <= t).astype(np.float32), np.stack(masks + [sub], axis=0)


def _rows_broadcast(b, first, period, reps):
    n = b.shape[0] // period
    return jnp.concatenate([jnp.broadcast_to(b[first + j * period:first + j * period + 1, :], (reps, b.shape[1]))
                            for j in range(n) for _ in range(period // reps)], axis=0)


def _level_decay(b, fclip, row, lvl):
    if lvl == 0:
        return jnp.where((row & 1) == 1, fclip, 1.0)
    m = 1 << lvl
    if 2 * m < 8:
        ref = jnp.where((row & m * 2) == 0, _rows_broadcast(b, m - 1, 8, 8), _rows_broadcast(b, 3 * m - 1, 8, 8))
    else:
        ref = _rows_broadcast(b, m - 1, 2 * m, 2 * m)
    return jnp.exp2(-jnp.abs(b - ref))


class _RowBlocks:
    def __init__(self, n):
        self.tiles = [None] * (n // SUBLANES)

    def add(self, first_row, val):
        for u in range(val.shape[0] // SUBLANES):
            piece = val[u * SUBLANES:(u + 1) * SUBLANES]
            i = first_row // SUBLANES + u
            self.tiles[i] = piece if self.tiles[i] is None else self.tiles[i] + piece

    def value(self):
        return jnp.concatenate(self.tiles, axis=0)


def _hgrn_scores(q, k, b, fclip, row, pm_ref, first_level):
    c = HG_CHUNK
    scores = _RowBlocks(c)
    if first_level == 0:
        scores.add(0, pm_ref[0] * lax.dot_general(q.astype(BF16), k.astype(BF16), NT_DIMS, preferred_element_type=F32))
    for lvl in range(first_level, HG_LEVELS):
        e = _level_decay(b, fclip, row, lvl)
        m = 1 << lvl
        if m < SUBLANES:
            s_l = lax.dot_general((q * e).astype(BF16), (k * e).astype(BF16), NT_DIMS, preferred_element_type=F32)
            scores.add(0, pm_ref[lvl + 1] * s_l)
        else:
            up = [slice(u * 2 * m + m, (u + 1) * 2 * m) for u in range(c // (2 * m))]
            lo = [slice(u * 2 * m, u * 2 * m + m) for u in range(c // (2 * m))]
            qe = jnp.concatenate([q[r] * e[r] for r in up], axis=0).astype(BF16)
            ke = jnp.concatenate([piece for r in lo for piece in (k[r] * e[r], jnp.zeros((m, LANES), F32))],
                                 axis=0).astype(BF16)
            s_l = lax.dot_general(qe, ke, NT_DIMS, preferred_element_type=F32)
            for u, r in enumerate(up):
                part = s_l[u * m:(u + 1) * m]
                scores.add(r.start, part if len(up) == 1 else pm_ref[lvl + 1, r, :] * part)
    return scores


def _sub_block_start(b):
    n = b.shape[0] // HG_SUB
    pieces = [jnp.zeros((HG_SUB, b.shape[1]), F32)]
    pieces += [jnp.broadcast_to(b[u * HG_SUB - 1:u * HG_SUB, :], (HG_SUB, b.shape[1])) for u in range(1, n)]
    return jnp.concatenate(pieces, axis=0)


def _hgrn_kernel(q_ref, f_ref, i_ref, g_ref, lb_ref, ng_ref, lm_ref, pm_ref, o_ref, st_ref, k_ref, b_ref, fc_ref):
    c = HG_CHUNK
    nh = HG_HEADS_PER_STEP

    @pl.when(pl.program_id(2) == 0)
    def _():
        st_ref[...] = jnp.zeros_like(st_ref)

    worst = jnp.zeros((1, LANES), F32)
    for j in range(nh):
        cols = slice(j * LANES, (j + 1) * LANES)
        lb = lb_ref[:, cols]
        sig = jax.nn.sigmoid(f_ref[:, cols])
        fclip = jnp.maximum(lb + (1.0 - lb) * sig, MIN_FORGET)
        logf = jnp.log(fclip) * LOG2_E
        g_hi = logf.astype(BF16)
        g_lo = (logf - g_hi.astype(F32)).astype(BF16)
        b2 = jnp.dot(lm_ref[...], jnp.concatenate([g_hi, g_lo], axis=1), preferred_element_type=F32)
        b = b2[:, :LANES] + b2[:, LANES:]
        k_ref[:, cols] = (1.0 - lb) * (1.0 - sig)
        b_ref[:, cols] = b
        fc_ref[:, cols] = fclip
        for u in range(c // HG_SUB):
            start = b[u * HG_SUB - 1:u * HG_SUB, :] if u else jnp.zeros((1, LANES), F32)
            worst = jnp.maximum(worst, start - b[(u + 1) * HG_SUB - 1:(u + 1) * HG_SUB, :])
    safe = jnp.max(worst) < HG_SAFE_LOG2_DECAY

    def heads(direct):
        row = lax.broadcasted_iota(I32, (c, LANES), 0)
        for j in range(nh):
            cols = slice(j * LANES, (j + 1) * LANES)
            q = q_ref[:, cols]
            v = i_ref[:, cols]
            k = k_ref[:, cols]
            b = b_ref[:, cols]
            b_last = b[c - 1:c, :]

            st = st_ref[j]
            o = lax.dot_general((q * jnp.exp2(b)).astype(BF16), st.astype(BF16), NT_DIMS, preferred_element_type=F32)

            if direct:
                rel = b - _sub_block_start(b)
                s_d = lax.dot_general((q * jnp.exp2(rel)).astype(BF16), (k * jnp.exp2(-rel)).astype(BF16),
                                      NT_DIMS, preferred_element_type=F32)
                scores = _hgrn_scores(q, k, b, None, row, pm_ref, HG_SUB.bit_length() - 1)
                scores.add(0, jnp.where(pm_ref[HG_LEVELS + 1] > 0.5, s_d, 0.0))
            else:
                scores = _hgrn_scores(q, k, b, fc_ref[:, cols], row, pm_ref, 0)
            o = o + jnp.dot(scores.value().astype(BF16), v.astype(BF16), preferred_element_type=F32)

            ks = (k * jnp.exp2(b_last - b)).astype(BF16)
            st_ref[j] = st * jnp.exp2(b_last) + jnp.dot(v.T.astype(BF16), ks, preferred_element_type=F32)

            o = o * lax.rsqrt(jnp.mean(o * o, -1, keepdims=True) + RMS_EPS) * ng_ref[:, cols]
            gate = g_ref[:, cols]
            o_ref[:, cols] = (o * (gate * jax.nn.sigmoid(gate))).astype(o_ref.dtype)

    @pl.when(safe)
    def _():
        heads(True)

    @pl.when(jnp.logical_not(safe))
    def _():
        heads(False)


def _hgrn(proj, lb, norm_g, bsz, seq):
    t = bsz * seq
    nc = seq // HG_CHUNK
    c = HG_CHUNK
    nh = HG_HEADS_PER_STEP
    w = nh * HG_DK
    hsteps = HG_HEADS // nh
    lm, pm = _hgrn_constants()

    def part(p):
        return pl.BlockSpec((c, w), lambda b, h, n: (b * nc + n, p * hsteps + h))

    head = pl.BlockSpec((1, w), lambda b, h, n: (0, h))
    return pl.pallas_call(
        _hgrn_kernel,
        grid=(bsz, hsteps, nc),
        in_specs=[part(0), part(1), part(2), part(3), head, head,
                  pl.BlockSpec((c, c), lambda b, h, n: (0, 0)),
                  pl.BlockSpec((HG_LEVELS + 2, c, c), lambda b, h, n: (0, 0, 0))],
        out_specs=pl.BlockSpec((c, w), lambda b, h, n: (b * nc + n, h)),
        out_shape=jax.ShapeDtypeStruct((t, D_MODEL), BF16),
        scratch_shapes=[pltpu.VMEM((nh, HG_DV, HG_DK), F32)] + [pltpu.VMEM((c, w), F32)] * 3,
        compiler_params=_params("arbitrary", "arbitrary", "arbitrary"),
        name="hgrn2",
    )(proj, proj, proj, proj, lb.reshape(1, D_MODEL), norm_g.reshape(1, D_MODEL),
      jnp.asarray(lm, BF16), jnp.asarray(pm, F32))


def _swa_kernel(sink_ref, q_ref, kvp_ref, kvc_ref, bias_ref, o_ref):
    w = WINDOW
    hd = ATT_HEAD_DIM
    kvw = ATT_KV_HEADS * hd
    ones = jnp.ones((2 * w, hd), F32)
    for g in range(ATT_KV_HEADS):
        kwin = jnp.concatenate([kvp_ref[:, g * hd:(g + 1) * hd], kvc_ref[:, g * hd:(g + 1) * hd]], axis=0)
        vwin = jnp.concatenate([kvp_ref[:, kvw + g * hd:kvw + (g + 1) * hd],
                                kvc_ref[:, kvw + g * hd:kvw + (g + 1) * hd]], axis=0)
        heads = range(g * ATT_GROUP, (g + 1) * ATT_GROUP)
        qg = jnp.concatenate([q_ref[:, h * hd:(h + 1) * hd] for h in heads], axis=0) * (hd ** -0.5)
        sink = jnp.concatenate([jnp.full((1, w), sink_ref[h], F32) for h in heads], axis=1)
        s = lax.dot_general(kwin, qg.astype(BF16), NT_DIMS, preferred_element_type=F32) + bias_ref[g]
        m = jnp.maximum(jnp.max(s, 0, keepdims=True), sink)
        p = jnp.exp(s - m).astype(BF16)
        vext_t = jnp.concatenate([vwin.astype(F32), ones], axis=1).T.astype(BF16)
        ov = jnp.dot(vext_t, p, preferred_element_type=F32)
        ov = ov / (ov[hd:hd + 1, :] + jnp.exp(sink - m))
        og = ov.T.astype(o_ref.dtype)
        o_ref[:, g * ATT_GROUP * hd:(g + 1) * ATT_GROUP * hd] = jnp.concatenate(
            [og[j * w:(j + 1) * w, :hd] for j in range(ATT_GROUP)], axis=1)


def _swa(q, kv, bias, sinks, bsz, seq):
    t = bsz * seq
    nb = seq // WINDOW
    kvw2 = 2 * ATT_KV_HEADS * ATT_HEAD_DIM
    return pl.pallas_call(
        _swa_kernel,
        grid=(bsz, nb),
        in_specs=[pl.BlockSpec(memory_space=pltpu.SMEM),
                  pl.BlockSpec((WINDOW, D_MODEL), lambda b, n: (b * nb + n, 0)),
                  pl.BlockSpec((WINDOW, kvw2), lambda b, n: (b * nb + jnp.maximum(n - 1, 0), 0)),
                  pl.BlockSpec((WINDOW, kvw2), lambda b, n: (b * nb + n, 0)),
                  pl.BlockSpec((None, ATT_KV_HEADS, 2 * WINDOW, ATT_GROUP * WINDOW),
                               lambda b, n: (jnp.minimum(n, 1), 0, 0, 0))],
        out_specs=pl.BlockSpec((WINDOW, D_MODEL), lambda b, n: (b * nb + n, 0)),
        out_shape=jax.ShapeDtypeStruct((t, D_MODEL), BF16),
        compiler_params=_params("arbitrary", "arbitrary"),
        name="swa",
    )(sinks, q, kv, kv, bias)


def _t5_bucket(dist):
    n = jnp.clip(dist, 0, REL_MAX_DISTANCE - 1)
    max_exact = N_BUCKETS // 2
    large = max_exact + (jnp.log(jnp.maximum(n, max_exact).astype(F32) / max_exact)
                         / math.log(REL_MAX_DISTANCE / max_exact)
                         * (N_BUCKETS - max_exact)).astype(I32)
    large = jnp.minimum(large, N_BUCKETS - 1)
    return jnp.where(n < max_exact, n, large)


def _band_bias_t(rel_bias):
    kj = jnp.arange(2 * WINDOW)[:, None]
    qi = jnp.arange(WINDOW)[None, :]
    bucket = _t5_bucket(qi + WINDOW - kj).reshape(-1)
    onehot = (bucket[:, None] == jnp.arange(N_BUCKETS)[None, :]).astype(F32)
    table = jnp.dot(onehot, rel_bias.astype(F32), precision=lax.Precision.HIGHEST)
    table = table.reshape(2 * WINDOW, WINDOW, ATT_KV_HEADS, ATT_GROUP)
    table = table.transpose(2, 0, 3, 1)
    dist = (qi + WINDOW - kj)[None, :, None, :]
    in_window = (dist >= 0) & (dist < WINDOW)
    own_block = (kj >= WINDOW)[None, :, None, :]
    both = jnp.stack([jnp.where(in_window & own_block, table, NEG_BIG), jnp.where(in_window, table, NEG_BIG)])
    return both.reshape(2, ATT_KV_HEADS, 2 * WINDOW, ATT_GROUP * WINDOW)


def _moe_kernel(layer, te_ref, nu_ref, nx_ref, nv_ref, src_ref, dst_ref,
                x_hbm, wg_hbm, wu_hbm, wd_hbm, y_hbm,
                xres, xbuf, ybuf, wgs_ref, wus_ref, wds_ref, wgb_ref, wub_ref, wdb_ref, in_sem, out_sem, w_sem):
    i = pl.program_id(0)
    n_used = nu_ref[0]
    rows = MOE_TILE
    half = MOE_TILE // 2

    def wait_scatter(n):
        pltpu.make_async_copy(ybuf.at[pl.ds(0, n)], ybuf.at[pl.ds(0, n)], out_sem.at[0]).wait()

    def wait_scatter_of(tile):
        @pl.when(nv_ref[tile] <= half)
        def _():
            wait_scatter(half)

        @pl.when(nv_ref[tile] > half)
        def _():
            wait_scatter(rows)

    def weight_copies(e):
        return (pltpu.make_async_copy(wg_hbm.at[layer, e], wgs_ref, w_sem.at[0]),
                pltpu.make_async_copy(wu_hbm.at[layer, e], wus_ref, w_sem.at[1]),
                pltpu.make_async_copy(wd_hbm.at[layer, e], wds_ref, w_sem.at[2]))

    def run_tile(n):
        for r in range(n):
            xbuf[r] = xres[src_ref[i * rows + r]]
        x = _unpack_bf16_pairs(xbuf[pl.ds(0, n)]).astype(BF16)
        hg = jnp.dot(x, wgb_ref[...], preferred_element_type=F32)
        hu = jnp.dot(x, wub_ref[...], preferred_element_type=F32)
        hidden = (hg * jax.nn.sigmoid(hg) * hu).astype(BF16)
        y = jnp.dot(hidden, wdb_ref[...], preferred_element_type=F32)

        @pl.when(i > 0)
        def _():
            wait_scatter_of(i - 1)

        ybuf[pl.ds(0, n)] = _pack_bf16_pairs(y)
        for r in range(n):
            pltpu.make_async_copy(ybuf.at[r], y_hbm.at[dst_ref[i * rows + r]], out_sem.at[0]).start(priority=r % 2)

    @pl.when(i < n_used)
    def _():
        @pl.when(i == 0)
        def _():
            for cp in weight_copies(te_ref[0]):
                cp.start()
            load_x = pltpu.make_async_copy(x_hbm, xres, in_sem.at[0])
            load_x.start()
            ybuf[...] = jnp.zeros_like(ybuf)
            dump = pltpu.make_async_copy(ybuf, y_hbm.at[pl.ds(y_hbm.shape[0] - rows, rows)], out_sem.at[0])
            dump.start()
            dump.wait()
            load_x.wait()

        @pl.when((i == 0) | (te_ref[i] != te_ref[jnp.maximum(i - 1, 0)]))
        def _():
            for cp in weight_copies(te_ref[i]):
                cp.wait()
            wgb_ref[...] = wgs_ref[...].astype(BF16)
            wub_ref[...] = wus_ref[...].astype(BF16)
            wdb_ref[...] = wds_ref[...].astype(BF16)

            @pl.when(nx_ref[i] >= 0)
            def _():
                for cp in weight_copies(nx_ref[i]):
                    cp.start()

        @pl.when(nv_ref[i] <= half)
        def _():
            run_tile(half)

        @pl.when(nv_ref[i] > half)
        def _():
            run_tile(rows)

        @pl.when(i == n_used - 1)
        def _():
            wait_scatter_of(i)


def _moe_experts(xp, tile_expert, n_used, next_expert, n_valid, slot_src, slot_dst, w_gate, w_up, w_down, layer):
    t = xp.shape[0]
    d = D_MODEL
    n_tiles = tile_expert.shape[0]
    any_spec = pl.BlockSpec(memory_space=pl.ANY)
    return pl.pallas_call(
        functools.partial(_moe_kernel, layer),
        grid_spec=pltpu.PrefetchScalarGridSpec(
            num_scalar_prefetch=6,
            grid=(n_tiles,),
            in_specs=[any_spec, any_spec, any_spec, any_spec],
            out_specs=any_spec,
            scratch_shapes=[pltpu.VMEM((t,) + SLAB, U32),
                            pltpu.VMEM((MOE_TILE,) + SLAB, U32), pltpu.VMEM((MOE_TILE,) + SLAB, U32),
                            pltpu.VMEM((d, D_EXPERT), F32), pltpu.VMEM((d, D_EXPERT), F32),
                            pltpu.VMEM((D_EXPERT, d), F32),
                            pltpu.VMEM((d, D_EXPERT), BF16), pltpu.VMEM((d, D_EXPERT), BF16),
                            pltpu.VMEM((D_EXPERT, d), BF16),
                            pltpu.SemaphoreType.DMA((1,)), pltpu.SemaphoreType.DMA((1,)),
                            pltpu.SemaphoreType.DMA((3,))]),
        out_shape=jax.ShapeDtypeStruct((TOP_K * t + MOE_TILE,) + SLAB, U32),
        compiler_params=pltpu.CompilerParams(dimension_semantics=("arbitrary",), vmem_limit_bytes=MOE_VMEM_LIMIT),
        name="moe_experts",
    )(tile_expert, n_used, next_expert, n_valid, slot_src, slot_dst, xp, w_gate, w_up, w_down)


def _hier_moe(xp, route, counts, w_gate, w_up, w_down, layer):
    n_tok = xp.shape[0]
    n_asg = n_tok * TOP_K
    e_flat = route[:, :TOP_K].astype(I32).reshape(-1)
    counts = counts[0, :N_EXPERTS].astype(I32)
    padded = (counts + MOE_TILE - 1) // MOE_TILE * MOE_TILE
    pad_ends = jnp.cumsum(padded)
    pad_starts = pad_ends - padded
    n_tiles = -(-n_asg // MOE_TILE) + N_EXPERTS
    n_slots = n_tiles * MOE_TILE
    n_used = pad_ends[-1] // MOE_TILE

    fill_ends = jnp.cumsum(padded - counts)
    filler = jnp.arange(n_slots - n_asg, dtype=I32)
    filler_key = jnp.sum((fill_ends[None, :] <= filler[:, None]).astype(I32), 1)
    keys = jnp.concatenate([e_flat, filler_key])
    ids = jnp.concatenate([jnp.arange(n_asg, dtype=I32), jnp.full((n_slots - n_asg,), -1, I32)])
    _, asg = lax.sort((keys, ids), num_keys=1)
    valid = asg >= 0
    slot_src = jnp.where(valid, asg // TOP_K, 0)
    slot_dst = jnp.where(valid, (asg % TOP_K) * n_tok + asg // TOP_K,
                         TOP_K * n_tok + jnp.arange(n_slots, dtype=I32) % MOE_TILE)

    tile_start = jnp.arange(n_tiles, dtype=I32) * MOE_TILE
    used = tile_start < pad_ends[-1]
    tile_expert = jnp.minimum(jnp.sum((pad_ends[None, :] <= tile_start[:, None]).astype(I32), 1), N_EXPERTS - 1)
    next_tile = pad_ends[tile_expert] // MOE_TILE
    next_expert = jnp.where(used & (next_tile < n_used), tile_expert[jnp.minimum(next_tile, n_tiles - 1)], -1)
    rank0 = tile_start - pad_starts[tile_expert]
    n_valid = jnp.where(used, jnp.clip(counts[tile_expert] - rank0, 0, MOE_TILE), 0)
    return _moe_experts(xp, tile_expert, n_used.reshape(1).astype(I32), next_expert.astype(I32),
                        n_valid.astype(I32), slot_src, slot_dst, w_gate, w_up, w_down, layer)


def kernel(x, a_w_in, a_lower_bound, a_norm_g, a_w_out, b_w_kv, b_w_q, b_sinks, b_w_out, rel_bias,
           moe_w_rg, moe_b_rg, moe_w_re, moe_b_re, moe_w_gate, moe_w_up, moe_w_down, ln_g, ln_b):
    bsz, seq, d = x.shape
    t = bsz * seq
    lb_sm = jax.nn.softmax(a_lower_bound.astype(F32), axis=0)
    lower_bounds = jnp.cumsum(lb_sm, axis=0) - lb_sm[0]
    att_bias = _band_bias_t(rel_bias)
    pad = ROUTER_COLS - N_GROUPS - N_EXPERTS
    w_router = jnp.concatenate([moe_w_rg, moe_w_re, jnp.zeros((DEPTH, d, pad), F32)], axis=-1)
    w_router_hi = w_router.astype(BF16)
    w_router_lo = (w_router - w_router_hi.astype(F32)).astype(BF16)
    w_router = jnp.concatenate([w_router_hi, w_router_lo], axis=-1)
    b_router = jnp.concatenate([moe_b_rg, moe_b_re, jnp.zeros((DEPTH, pad), F32)], axis=-1)

    xf = x.reshape(t, d).astype(F32)
    xb = xf.astype(BF16)
    kv = None
    for layer in range(DEPTH):
        if layer < N_A_LAYERS:
            proj = _matmul(xb, a_w_in, layer, F32)
            o = _hgrn(proj, lower_bounds[layer], a_norm_g[layer], bsz, seq)
            w_out, w_layer = a_w_out, layer
        else:
            j = layer - N_A_LAYERS
            if kv is None:
                kv = _matmul(xb, b_w_kv[None], 0, BF16)
            q = _matmul(xb, b_w_q, j, BF16)
            o = _swa(q, kv, att_bias, b_sinks[j].astype(F32), bsz, seq)
            w_out, w_layer = b_w_out, j
        xf, xp, route, counts = _out_ln_router(o, w_out, w_layer, xf, ln_g[2 * layer], ln_b[2 * layer],
                                               w_router[layer], b_router[layer].reshape(1, ROUTER_COLS))
        y_planes = _hier_moe(xp, route, counts, moe_w_gate, moe_w_up, moe_w_down, layer)
        xf, xb = _ln_moe(xf, y_planes, route, ln_g[2 * layer + 1], ln_b[2 * layer + 1])
    return xf.reshape(bsz, seq, d).astype(x.dtype)
```

```python
import functools
import math

import numpy as np
import jax
import jax.numpy as jnp
from jax import lax
from jax.experimental import pallas as pl
from jax.experimental.pallas import tpu as pltpu

F32 = jnp.float32
BF16 = jnp.bfloat16
U32 = jnp.uint32
I32 = jnp.int32

D_MODEL = 2048
DEPTH = 4
N_A_LAYERS = DEPTH // 2
HG_HEADS = 16
HG_DK = 128
HG_DV = 128
ATT_HEAD_DIM = 64
ATT_Q_HEADS = 32
ATT_KV_HEADS = 4
ATT_GROUP = ATT_Q_HEADS // ATT_KV_HEADS
WINDOW = 128
N_BUCKETS = 32
REL_MAX_DISTANCE = 128
N_GROUPS = 4
EXPERTS_PER_GROUP = 8
N_EXPERTS = N_GROUPS * EXPERTS_PER_GROUP
TOP_K = 2
D_EXPERT = D_MODEL // 4
DEEPNORM_ALPHA = (2 * DEPTH) ** 0.25
LN_EPS = 1e-5
RMS_EPS = 1e-6
NEG_BIG = -1e30
MIN_FORGET = 1e-30
LOG2_E = 1.0 / math.log(2.0)

LANES = 128
HG_CHUNK = 128
HG_LEVELS = 7
HG_HEADS_PER_STEP = 16
HG_SUB = 64
HG_SAFE_LOG2_DECAY = 120.0
MOE_TILE = 256
OUT_LN_CHUNK = 512
ROUTER_COLS = 128
SUBLANES = 8
SLAB = (SUBLANES, LANES)
assert D_MODEL // 2 == SUBLANES * LANES
VMEM_LIMIT = 56 * 1024 * 1024
MOE_VMEM_LIMIT = 60 * 1024 * 1024
NT_DIMS = (((1,), (1,)), ((), ()))


def _params(*sem):
    return pltpu.CompilerParams(dimension_semantics=sem, vmem_limit_bytes=VMEM_LIMIT)


def _pack_bf16_pairs(y):
    half = y.shape[1] // 2
    lo = lax.bitcast_convert_type(y[:, :half].astype(BF16).astype(F32), U32)
    hi = lax.bitcast_convert_type(y[:, half:].astype(BF16).astype(F32), U32)
    w = (hi & jnp.uint32(0xFFFF0000)) | (lo >> 16)
    return jnp.swapaxes(jnp.stack([w[:, s * LANES:(s + 1) * LANES] for s in range(SUBLANES)], axis=0), 0, 1)


def _unpack_bf16_pairs(slabs):
    cols = jnp.swapaxes(slabs, 0, 1)
    w = jnp.concatenate([cols[s] for s in range(SUBLANES)], axis=1)
    lo = lax.bitcast_convert_type(w << 16, F32)
    hi = lax.bitcast_convert_type(w & jnp.uint32(0xFFFF0000), F32)
    return jnp.concatenate([lo, hi], axis=1)


def _mm_kernel(x_ref, w_ref, o_ref, wb_ref):
    @pl.when(pl.program_id(1) == 0)
    def _():
        wb_ref[...] = w_ref[...].astype(BF16)

    o_ref[...] = jnp.dot(x_ref[...], wb_ref[...], preferred_element_type=F32).astype(o_ref.dtype)


def _matmul(x, w, layer, out_dtype, tm=1024, tn=1024):
    m, k = x.shape
    n = w.shape[2]
    tm, tn = min(tm, m), min(tn, n)
    return pl.pallas_call(
        _mm_kernel,
        grid=(n // tn, m // tm),
        in_specs=[pl.BlockSpec((tm, k), lambda j, i: (i, 0)),
                  pl.BlockSpec((None, k, tn), lambda j, i: (layer, 0, j))],
        out_specs=pl.BlockSpec((tm, tn), lambda j, i: (i, j)),
        out_shape=jax.ShapeDtypeStruct((m, n), out_dtype),
        scratch_shapes=[pltpu.VMEM((k, tn), BF16)],
        compiler_params=_params("arbitrary", "arbitrary"),
        name="matmul",
    )(x, w)


def _layernorm(v, g, b):
    mu = jnp.mean(v, -1, keepdims=True)
    d = v - mu
    var = jnp.mean(d * d, -1, keepdims=True)
    return d * lax.rsqrt(var + LN_EPS) * g + b


def _route(logits):
    lane = lax.broadcasted_iota(I32, logits.shape, 1)
    big = jnp.int32(ROUTER_COLS)
    is_grp = lane < N_GROUPS
    gl = jnp.where(is_grp, logits, -jnp.inf)
    g_max = jnp.max(gl, -1, keepdims=True)
    grp = jnp.min(jnp.where(is_grp & (gl == g_max), lane, big), -1, keepdims=True)
    p_grp = 1.0 / jnp.sum(jnp.exp(gl - g_max), -1, keepdims=True)
    in_grp = (lane >= N_GROUPS) & (((lane - N_GROUPS) >> 3) == grp)
    el = jnp.where(in_grp, logits, -jnp.inf)
    v1 = jnp.max(el, -1, keepdims=True)
    i1 = jnp.min(jnp.where(in_grp & (el == v1), lane, big), -1, keepdims=True)
    el2 = jnp.where(lane == i1, -jnp.inf, el)
    v2 = jnp.max(el2, -1, keepdims=True)
    i2 = jnp.min(jnp.where(in_grp & (lane != i1) & (el2 == v2), lane, big), -1, keepdims=True)
    ex = jnp.exp(v2 - v1)
    w1 = 1.0 / (1.0 + ex)
    return i1 - N_GROUPS, i2 - N_GROUPS, p_grp * w1, p_grp * (ex * w1)


def _out_ln_router_kernel(layer, o_ref, w_hbm, x_ref, g_ref, b_ref, wr_ref, br_ref,
                          xo_ref, xp_ref, rt_ref, cnt_ref, ws_ref, wb_ref, w_sem):
    @pl.when(pl.program_id(0) == 0)
    def _():
        load_w = pltpu.make_async_copy(w_hbm.at[layer], ws_ref, w_sem.at[0])
        load_w.start()
        cnt_ref[...] = jnp.zeros_like(cnt_ref)
        load_w.wait()
        wb_ref[...] = ws_ref[...].astype(BF16)

    tm = o_ref.shape[0]
    chunk = min(OUT_LN_CHUNK, tm)
    for r0 in range(0, tm, chunk):
        rows = slice(r0, r0 + chunk)
        h = jnp.dot(o_ref[rows, :], wb_ref[...], preferred_element_type=F32)
        y = _layernorm(DEEPNORM_ALPHA * x_ref[rows, :] + h, g_ref[...], b_ref[...])
        xo_ref[rows, :] = y
        xp_ref[rows] = _pack_bf16_pairs(y)
        y_hi = y.astype(BF16)
        y_lo = (y - y_hi.astype(F32)).astype(BF16)
        t_hi = jnp.dot(y_hi, wr_ref[...], preferred_element_type=F32)
        t_lo = jnp.dot(y_lo, wr_ref[:, :ROUTER_COLS], preferred_element_type=F32)
        logits = t_hi[:, :ROUTER_COLS] + t_hi[:, ROUTER_COLS:] + t_lo + br_ref[...]
        e1, e2, g1, g2 = _route(logits)
        lane = lax.broadcasted_iota(I32, logits.shape, 1)
        rt_ref[rows, :] = jnp.where(lane == 0, e1.astype(F32),
                                    jnp.where(lane == 1, e2.astype(F32),
                                              jnp.where(lane == 2, g1, jnp.where(lane == 3, g2, 0.0))))
        hits = (lane == e1).astype(F32) + (lane == e2).astype(F32)
        cnt_ref[...] += jnp.sum(hits, 0, keepdims=True)


def _out_ln_router(o, w, layer, x, g, b, wr, br, tm=512):
    t, d = x.shape
    tm = min(tm, t)
    row = lambda i: (i, 0)
    const = lambda i: (0, 0)
    return pl.pallas_call(
        functools.partial(_out_ln_router_kernel, layer),
        grid=(t // tm,),
        in_specs=[pl.BlockSpec((tm, d), row), pl.BlockSpec(memory_space=pl.ANY), pl.BlockSpec((tm, d), row),
                  pl.BlockSpec((1, d), const), pl.BlockSpec((1, d), const),
                  pl.BlockSpec((d, 2 * ROUTER_COLS), const), pl.BlockSpec((1, ROUTER_COLS), const)],
        out_specs=[pl.BlockSpec((tm, d), row), pl.BlockSpec((tm,) + SLAB, lambda i: (i, 0, 0)),
                   pl.BlockSpec((tm, ROUTER_COLS), row), pl.BlockSpec((1, ROUTER_COLS), const)],
        out_shape=[jax.ShapeDtypeStruct((t, d), F32), jax.ShapeDtypeStruct((t,) + SLAB, U32),
                   jax.ShapeDtypeStruct((t, ROUTER_COLS), F32), jax.ShapeDtypeStruct((1, ROUTER_COLS), F32)],
        scratch_shapes=[pltpu.VMEM((d, d), F32), pltpu.VMEM((d, d), BF16), pltpu.SemaphoreType.DMA((1,))],
        compiler_params=_params("arbitrary"),
        name="out_ln_router",
    )(o, w, x, g.reshape(1, d), b.reshape(1, d), wr, br)


def _ln_moe_kernel(x_ref, y0_ref, y1_ref, rt_ref, g_ref, b_ref, xo_ref, *maybe_xb_ref):
    rt = rt_ref[...]
    f = rt[:, 2:3] * _unpack_bf16_pairs(y0_ref[...]) + rt[:, 3:4] * _unpack_bf16_pairs(y1_ref[...])
    y = _layernorm(DEEPNORM_ALPHA * x_ref[...] + f, g_ref[...], b_ref[...])
    xo_ref[...] = y
    for xb_ref in maybe_xb_ref:
        xb_ref[...] = y.astype(BF16)


def _ln_moe(x, y_planes, route, g, b, with_bf16, tm=512):
    t, d = x.shape
    tm = min(tm, t)
    row = lambda i: (i, 0)
    const = lambda i: (0, 0)
    dtypes = (F32, BF16) if with_bf16 else (F32,)
    return pl.pallas_call(
        _ln_moe_kernel,
        grid=(t // tm,),
        in_specs=[pl.BlockSpec((tm, d), row), pl.BlockSpec((tm,) + SLAB, lambda i: (i, 0, 0)),
                  pl.BlockSpec((tm,) + SLAB, lambda i: (i + t // tm, 0, 0)),
                  pl.BlockSpec((tm, ROUTER_COLS), row),
                  pl.BlockSpec((1, d), const), pl.BlockSpec((1, d), const)],
        out_specs=[pl.BlockSpec((tm, d), row) for _ in dtypes],
        out_shape=[jax.ShapeDtypeStruct((t, d), dt) for dt in dtypes],
        compiler_params=_params("arbitrary"),
        name="ln_moe",
    )(x, y_planes, y_planes, route, g.reshape(1, d), b.reshape(1, d))


def _hgrn_constants():
    c = HG_CHUNK
    t = np.arange(c)[:, None]
    s = np.arange(c)[None, :]
    masks = [(s == t).astype(np.float32)]
    for lvl in range(HG_LEVELS):
        m = 1 << lvl
        masks.append(((t // (2 * m) == s // (2 * m)) & ((t // m) % 2 == 1) & ((s // m) % 2 == 0)).astype(np.float32))
    sub = ((t // HG_SUB == s // HG_SUB) & (s <= t)).astype(np.float32)
    return (s <= t).astype(np.float32), np.stack(masks + [sub], axis=0)


def _rows_broadcast(b, first, period, reps):
    n = b.shape[0] // period
    return jnp.concatenate([jnp.broadcast_to(b[first + j * period:first + j * period + 1, :], (reps, b.shape[1]))
                            for j in range(n) for _ in range(period // reps)], axis=0)


def _level_decay(b, fclip, row, lvl):
    if lvl == 0:
        return jnp.where((row & 1) == 1, fclip, 1.0)
    m = 1 << lvl
    if 2 * m < 8:
        ref = jnp.where((row & m * 2) == 0, _rows_broadcast(b, m - 1, 8, 8), _rows_broadcast(b, 3 * m - 1, 8, 8))
    else:
        ref = _rows_broadcast(b, m - 1, 2 * m, 2 * m)
    return jnp.exp2(-jnp.abs(b - ref))


class _RowBlocks:
    def __init__(self, n):
        self.tiles = [None] * (n // SUBLANES)

    def add(self, first_row, val):
        for u in range(val.shape[0] // SUBLANES):
            piece = val[u * SUBLANES:(u + 1) * SUBLANES]
            i = first_row // SUBLANES + u
            self.tiles[i] = piece if self.tiles[i] is None else self.tiles[i] + piece

    def value(self):
        return jnp.concatenate(self.tiles, axis=0)


def _hgrn_scores(q, k, b, fclip, row, pm_ref, first_level):
    c = HG_CHUNK
    scores = _RowBlocks(c)
    if first_level == 0:
        scores.add(0, pm_ref[0] * lax.dot_general(q.astype(BF16), k.astype(BF16), NT_DIMS, preferred_element_type=F32))
    for lvl in range(first_level, HG_LEVELS):
        e = _level_decay(b, fclip, row, lvl)
        m = 1 << lvl
        if m < SUBLANES:
            s_l = lax.dot_general((q * e).astype(BF16), (k * e).astype(BF16), NT_DIMS, preferred_element_type=F32)
            scores.add(0, pm_ref[lvl + 1] * s_l)
        else:
            up = [slice(u * 2 * m + m, (u + 1) * 2 * m) for u in range(c // (2 * m))]
            lo = [slice(u * 2 * m, u * 2 * m + m) for u in range(c // (2 * m))]
            qe = jnp.concatenate([q[r] * e[r] for r in up], axis=0).astype(BF16)
            ke = jnp.concatenate([piece for r in lo for piece in (k[r] * e[r], jnp.zeros((m, LANES), F32))],
                                 axis=0).astype(BF16)
            s_l = lax.dot_general(qe, ke, NT_DIMS, preferred_element_type=F32)
            for u, r in enumerate(up):
                part = s_l[u * m:(u + 1) * m]
                scores.add(r.start, part if len(up) == 1 else pm_ref[lvl + 1, r, :] * part)
    return scores


def _sub_block_start(b):
    n = b.shape[0] // HG_SUB
    pieces = [jnp.zeros((HG_SUB, b.shape[1]), F32)]
    pieces += [jnp.broadcast_to(b[u * HG_SUB - 1:u * HG_SUB, :], (HG_SUB, b.shape[1])) for u in range(1, n)]
    return jnp.concatenate(pieces, axis=0)


def _hgrn_kernel(q_ref, f_ref, i_ref, g_ref, lb_ref, ng_ref, lm_ref, pm_ref, o_ref, st_ref, k_ref, b_ref, fc_ref):
    c = HG_CHUNK
    nh = HG_HEADS_PER_STEP

    @pl.when(pl.program_id(2) == 0)
    def _():
        st_ref[...] = jnp.zeros_like(st_ref)

    worst = jnp.zeros((1, LANES), F32)
    for j in range(nh):
        cols = slice(j * LANES, (j + 1) * LANES)
        lb = lb_ref[:, cols]
        sig = jax.nn.sigmoid(f_ref[:, cols])
        fclip = jnp.maximum(lb + (1.0 - lb) * sig, MIN_FORGET)
        logf = jnp.log(fclip) * LOG2_E
        g_hi = logf.astype(BF16)
        g_lo = (logf - g_hi.astype(F32)).astype(BF16)
        b2 = jnp.dot(lm_ref[...], jnp.concatenate([g_hi, g_lo], axis=1), preferred_element_type=F32)
        b = b2[:, :LANES] + b2[:, LANES:]
        k_ref[:, cols] = (1.0 - lb) * (1.0 - sig)
        b_ref[:, cols] = b
        fc_ref[:, cols] = fclip
        for u in range(c // HG_SUB):
            start = b[u * HG_SUB - 1:u * HG_SUB, :] if u else jnp.zeros((1, LANES), F32)
            worst = jnp.maximum(worst, start - b[(u + 1) * HG_SUB - 1:(u + 1) * HG_SUB, :])
    safe = jnp.max(worst) < HG_SAFE_LOG2_DECAY

    def heads(direct):
        row = lax.broadcasted_iota(I32, (c, LANES), 0)
        for j in range(nh):
            cols = slice(j * LANES, (j + 1) * LANES)
            q = q_ref[:, cols]
            v = i_ref[:, cols]
            k = k_ref[:, cols]
            b = b_ref[:, cols]
            b_last = b[c - 1:c, :]

            st = st_ref[j]
            o = lax.dot_general((q * jnp.exp2(b)).astype(BF16), st.astype(BF16), NT_DIMS, preferred_element_type=F32)

            if direct:
                rel = b - _sub_block_start(b)
                s_d = lax.dot_general((q * jnp.exp2(rel)).astype(BF16), (k * jnp.exp2(-rel)).astype(BF16),
                                      NT_DIMS, preferred_element_type=F32)
                scores = _hgrn_scores(q, k, b, None, row, pm_ref, HG_SUB.bit_length() - 1)
                scores.add(0, jnp.where(pm_ref[HG_LEVELS + 1] > 0.5, s_d, 0.0))
            else:
                scores = _hgrn_scores(q, k, b, fc_ref[:, cols], row, pm_ref, 0)
            o = o + jnp.dot(scores.value().astype(BF16), v.astype(BF16), preferred_element_type=F32)

            ks = (k * jnp.exp2(b_last - b)).astype(BF16)
            st_ref[j] = st * jnp.exp2(b_last) + jnp.dot(v.T.astype(BF16), ks, preferred_element_type=F32)

            o = o * lax.rsqrt(jnp.mean(o * o, -1, keepdims=True) + RMS_EPS) * ng_ref[:, cols]
            gate = g_ref[:, cols]
            o_ref[:, cols] = (o * (gate * jax.nn.sigmoid(gate))).astype(o_ref.dtype)

    @pl.when(safe)
    def _():
        heads(True)

    @pl.when(jnp.logical_not(safe))
    def _():
        heads(False)


def _hgrn(proj, lb, norm_g, bsz, seq):
    t = bsz * seq
    nc = seq // HG_CHUNK
    c = HG_CHUNK
    nh = HG_HEADS_PER_STEP
    w = nh * HG_DK
    hsteps = HG_HEADS // nh
    lm, pm = _hgrn_constants()

    def part(p):
        return pl.BlockSpec((c, w), lambda b, h, n: (b * nc + n, p * hsteps + h))

    head = pl.BlockSpec((1, w), lambda b, h, n: (0, h))
    return pl.pallas_call(
        _hgrn_kernel,
        grid=(bsz, hsteps, nc),
        in_specs=[part(0), part(1), part(2), part(3), head, head,
                  pl.BlockSpec((c, c), lambda b, h, n: (0, 0)),
                  pl.BlockSpec((HG_LEVELS + 2, c, c), lambda b, h, n: (0, 0, 0))],
        out_specs=pl.BlockSpec((c, w), lambda b, h, n: (b * nc + n, h)),
        out_shape=jax.ShapeDtypeStruct((t, D_MODEL), BF16),
        scratch_shapes=[pltpu.VMEM((nh, HG_DV, HG_DK), F32)] + [pltpu.VMEM((c, w), F32)] * 3,
        compiler_params=_params("arbitrary", "arbitrary", "arbitrary"),
        name="hgrn2",
    )(proj, proj, proj, proj, lb.reshape(1, D_MODEL), norm_g.reshape(1, D_MODEL),
      jnp.asarray(lm, BF16), jnp.asarray(pm, F32))


def _swa_kernel(sink_ref, q_ref, kvp_ref, kvc_ref, bias_ref, o_ref):
    w = WINDOW
    hd = ATT_HEAD_DIM
    kvw = ATT_KV_HEADS * hd
    ones = jnp.ones((2 * w, hd), F32)
    for g in range(ATT_KV_HEADS):
        kwin = jnp.concatenate([kvp_ref[:, g * hd:(g + 1) * hd], kvc_ref[:, g * hd:(g + 1) * hd]], axis=0)
        vwin = jnp.concatenate([kvp_ref[:, kvw + g * hd:kvw + (g + 1) * hd],
                                kvc_ref[:, kvw + g * hd:kvw + (g + 1) * hd]], axis=0)
        heads = range(g * ATT_GROUP, (g + 1) * ATT_GROUP)
        qg = jnp.concatenate([q_ref[:, h * hd:(h + 1) * hd] for h in heads], axis=0) * (hd ** -0.5)
        sink = jnp.concatenate([jnp.full((1, w), sink_ref[h], F32) for h in heads], axis=1)
        s = lax.dot_general(kwin, qg.astype(BF16), NT_DIMS, preferred_element_type=F32) + bias_ref[g]
        m = jnp.maximum(jnp.max(s, 0, keepdims=True), sink)
        p = jnp.exp(s - m).astype(BF16)
        vext_t = jnp.concatenate([vwin.astype(F32), ones], axis=1).T.astype(BF16)
        ov = jnp.dot(vext_t, p, preferred_element_type=F32)
        ov = ov / (ov[hd:hd + 1, :] + jnp.exp(sink - m))
        og = ov.T.astype(o_ref.dtype)
        o_ref[:, g * ATT_GROUP * hd:(g + 1) * ATT_GROUP * hd] = jnp.concatenate(
            [og[j * w:(j + 1) * w, :hd] for j in range(ATT_GROUP)], axis=1)


def _swa(q, kv, bias, sinks, bsz, seq):
    t = bsz * seq
    nb = seq // WINDOW
    kvw2 = 2 * ATT_KV_HEADS * ATT_HEAD_DIM
    return pl.pallas_call(
        _swa_kernel,
        grid=(bsz, nb),
        in_specs=[pl.BlockSpec(memory_space=pltpu.SMEM),
                  pl.BlockSpec((WINDOW, D_MODEL), lambda b, n: (b * nb + n, 0)),
                  pl.BlockSpec((WINDOW, kvw2), lambda b, n: (b * nb + jnp.maximum(n - 1, 0), 0)),
                  pl.BlockSpec((WINDOW, kvw2), lambda b, n: (b * nb + n, 0)),
                  pl.BlockSpec((None, ATT_KV_HEADS, 2 * WINDOW, ATT_GROUP * WINDOW),
                               lambda b, n: (jnp.minimum(n, 1), 0, 0, 0))],
        out_specs=pl.BlockSpec((WINDOW, D_MODEL), lambda b, n: (b * nb + n, 0)),
        out_shape=jax.ShapeDtypeStruct((t, D_MODEL), BF16),
        compiler_params=_params("arbitrary", "arbitrary"),
        name="swa",
    )(sinks, q, kv, kv, bias)


def _t5_bucket(dist):
    n = jnp.clip(dist, 0, REL_MAX_DISTANCE - 1)
    max_exact = N_BUCKETS // 2
    large = max_exact + (jnp.log(jnp.maximum(n, max_exact).astype(F32) / max_exact)
                         / math.log(REL_MAX_DISTANCE / max_exact)
                         * (N_BUCKETS - max_exact)).astype(I32)
    large = jnp.minimum(large, N_BUCKETS - 1)
    return jnp.where(n < max_exact, n, large)


def _band_bias_t(rel_bias):
    kj = jnp.arange(2 * WINDOW)[:, None]
    qi = jnp.arange(WINDOW)[None, :]
    bucket = _t5_bucket(qi + WINDOW - kj).reshape(-1)
    onehot = (bucket[:, None] == jnp.arange(N_BUCKETS)[None, :]).astype(F32)
    table = jnp.dot(onehot, rel_bias.astype(F32), precision=lax.Precision.HIGHEST)
    table = table.reshape(2 * WINDOW, WINDOW, ATT_KV_HEADS, ATT_GROUP)
    table = table.transpose(2, 0, 3, 1)
    dist = (qi + WINDOW - kj)[None, :, None, :]
    in_window = (dist >= 0) & (dist < WINDOW)
    own_block = (kj >= WINDOW)[None, :, None, :]
    both = jnp.stack([jnp.where(in_window & own_block, table, NEG_BIG), jnp.where(in_window, table, NEG_BIG)])
    return both.reshape(2, ATT_KV_HEADS, 2 * WINDOW, ATT_GROUP * WINDOW)


def _moe_kernel(layer, te_ref, nu_ref, nx_ref, nv_ref, src_ref, dst_ref,
                x_hbm, wg_hbm, wu_hbm, wd_hbm, y_hbm,
                xres, xbuf, ybuf, wgs_ref, wus_ref, wds_ref, wgb_ref, wub_ref, wdb_ref, in_sem, out_sem, w_sem):
    i = pl.program_id(0)
    n_used = nu_ref[0]
    rows = MOE_TILE
    half = MOE_TILE // 2

    def wait_scatter(n):
        pltpu.make_async_copy(ybuf.at[pl.ds(0, n)], ybuf.at[pl.ds(0, n)], out_sem.at[0]).wait()

    def wait_scatter_of(tile):
        @pl.when(nv_ref[tile] <= half)
        def _():
            wait_scatter(half)

        @pl.when(nv_ref[tile] > half)
        def _():
            wait_scatter(rows)

    def weight_copies(e):
        return (pltpu.make_async_copy(wg_hbm.at[layer, e], wgs_ref, w_sem.at[0]),
                pltpu.make_async_copy(wu_hbm.at[layer, e], wus_ref, w_sem.at[1]),
                pltpu.make_async_copy(wd_hbm.at[layer, e], wds_ref, w_sem.at[2]))

    def run_tile(n):
        for r in range(n):
            xbuf[r] = xres[src_ref[i * rows + r]]
        x = _unpack_bf16_pairs(xbuf[pl.ds(0, n)]).astype(BF16)
        hg = jnp.dot(x, wgb_ref[...], preferred_element_type=F32)
        hu = jnp.dot(x, wub_ref[...], preferred_element_type=F32)
        hidden = (hg * jax.nn.sigmoid(hg) * hu).astype(BF16)
        y = jnp.dot(hidden, wdb_ref[...], preferred_element_type=F32)

        @pl.when(i > 0)
        def _():
            wait_scatter_of(i - 1)

        ybuf[pl.ds(0, n)] = _pack_bf16_pairs(y)
        for r in range(n):
            pltpu.make_async_copy(ybuf.at[r], y_hbm.at[dst_ref[i * rows + r]], out_sem.at[0]).start(priority=r % 2)

    @pl.when(i < n_used)
    def _():
        @pl.when(i == 0)
        def _():
            for cp in weight_copies(te_ref[0]):
                cp.start()
            load_x = pltpu.make_async_copy(x_hbm, xres, in_sem.at[0])
            load_x.start()
            ybuf[...] = jnp.zeros_like(ybuf)
            dump = pltpu.make_async_copy(ybuf, y_hbm.at[pl.ds(y_hbm.shape[0] - rows, rows)], out_sem.at[0])
            dump.start()
            dump.wait()
            load_x.wait()

        @pl.when((i == 0) | (te_ref[i] != te_ref[jnp.maximum(i - 1, 0)]))
        def _():
            for cp in weight_copies(te_ref[i]):
                cp.wait()
            wgb_ref[...] = wgs_ref[...].astype(BF16)
            wub_ref[...] = wus_ref[...].astype(BF16)
            wdb_ref[...] = wds_ref[...].astype(BF16)

            @pl.when(nx_ref[i] >= 0)
            def _():
                for cp in weight_copies(nx_ref[i]):
                    cp.start()

        @pl.when(nv_ref[i] <= half)
        def _():
            run_tile(half)

        @pl.when(nv_ref[i] > half)
        def _():
            run_tile(rows)

        @pl.when(i == n_used - 1)
        def _():
            wait_scatter_of(i)


def _moe_experts(xp, tile_expert, n_used, next_expert, n_valid, slot_src, slot_dst, w_gate, w_up, w_down, layer):
    t = xp.shape[0]
    d = D_MODEL
    n_tiles = tile_expert.shape[0]
    any_spec = pl.BlockSpec(memory_space=pl.ANY)
    return pl.pallas_call(
        functools.partial(_moe_kernel, layer),
        grid_spec=pltpu.PrefetchScalarGridSpec(
            num_scalar_prefetch=6,
            grid=(n_tiles,),
            in_specs=[any_spec, any_spec, any_spec, any_spec],
            out_specs=any_spec,
            scratch_shapes=[pltpu.VMEM((t,) + SLAB, U32),
                            pltpu.VMEM((MOE_TILE,) + SLAB, U32), pltpu.VMEM((MOE_TILE,) + SLAB, U32),
                            pltpu.VMEM((d, D_EXPERT), F32), pltpu.VMEM((d, D_EXPERT), F32),
                            pltpu.VMEM((D_EXPERT, d), F32),
                            pltpu.VMEM((d, D_EXPERT), BF16), pltpu.VMEM((d, D_EXPERT), BF16),
                            pltpu.VMEM((D_EXPERT, d), BF16),
                            pltpu.SemaphoreType.DMA((1,)), pltpu.SemaphoreType.DMA((1,)),
                            pltpu.SemaphoreType.DMA((3,))]),
        out_shape=jax.ShapeDtypeStruct((TOP_K * t + MOE_TILE,) + SLAB, U32),
        compiler_params=pltpu.CompilerParams(dimension_semantics=("arbitrary",), vmem_limit_bytes=MOE_VMEM_LIMIT),
        name="moe_experts",
    )(tile_expert, n_used, next_expert, n_valid, slot_src, slot_dst, xp, w_gate, w_up, w_down)


def _hier_moe(xp, route, counts, w_gate, w_up, w_down, layer):
    n_tok = xp.shape[0]
    n_asg = n_tok * TOP_K
    e_flat = route[:, :TOP_K].astype(I32).reshape(-1)
    counts = counts[0, :N_EXPERTS].astype(I32)
    padded = (counts + MOE_TILE - 1) // MOE_TILE * MOE_TILE
    pad_ends = jnp.cumsum(padded)
    pad_starts = pad_ends - padded
    n_tiles = -(-n_asg // MOE_TILE) + N_EXPERTS
    n_slots = n_tiles * MOE_TILE
    n_used = pad_ends[-1] // MOE_TILE

    fill_ends = jnp.cumsum(padded - counts)
    filler = jnp.arange(n_slots - n_asg, dtype=I32)
    filler_key = jnp.sum((fill_ends[None, :] <= filler[:, None]).astype(I32), 1)
    keys = jnp.concatenate([e_flat, filler_key])
    ids = jnp.concatenate([jnp.arange(n_asg, dtype=I32), jnp.full((n_slots - n_asg,), -1, I32)])
    _, asg = lax.sort((keys, ids), num_keys=1)
    valid = asg >= 0
    slot_src = jnp.where(valid, asg // TOP_K, 0)
    slot_dst = jnp.where(valid, (asg % TOP_K) * n_tok + asg // TOP_K,
                         TOP_K * n_tok + jnp.arange(n_slots, dtype=I32) % MOE_TILE)

    tile_start = jnp.arange(n_tiles, dtype=I32) * MOE_TILE
    used = tile_start < pad_ends[-1]
    tile_expert = jnp.minimum(jnp.sum((pad_ends[None, :] <= tile_start[:, None]).astype(I32), 1), N_EXPERTS - 1)
    next_tile = pad_ends[tile_expert] // MOE_TILE
    next_expert = jnp.where(used & (next_tile < n_used), tile_expert[jnp.minimum(next_tile, n_tiles - 1)], -1)
    rank0 = tile_start - pad_starts[tile_expert]
    n_valid = jnp.where(used, jnp.clip(counts[tile_expert] - rank0, 0, MOE_TILE), 0)
    return _moe_experts(xp, tile_expert, n_used.reshape(1).astype(I32), next_expert.astype(I32),
                        n_valid.astype(I32), slot_src, slot_dst, w_gate, w_up, w_down, layer)


def kernel(x, a_w_in, a_lower_bound, a_norm_g, a_w_out, b_w_kv, b_w_q, b_sinks, b_w_out, rel_bias,
           moe_w_rg, moe_b_rg, moe_w_re, moe_b_re, moe_w_gate, moe_w_up, moe_w_down, ln_g, ln_b):
    bsz, seq, d = x.shape
    t = bsz * seq
    lb_sm = jax.nn.softmax(a_lower_bound.astype(F32), axis=0)
    lower_bounds = jnp.cumsum(lb_sm, axis=0) - lb_sm[0]
    att_bias = _band_bias_t(rel_bias)
    pad = ROUTER_COLS - N_GROUPS - N_EXPERTS
    w_router = jnp.concatenate([moe_w_rg, moe_w_re, jnp.zeros((DEPTH, d, pad), F32)], axis=-1)
    w_router_hi = w_router.astype(BF16)
    w_router_lo = (w_router - w_router_hi.astype(F32)).astype(BF16)
    w_router = jnp.concatenate([w_router_hi, w_router_lo], axis=-1)
    b_router = jnp.concatenate([moe_b_rg, moe_b_re, jnp.zeros((DEPTH, pad), F32)], axis=-1)

    xf = x.reshape(t, d).astype(F32)
    xb = xf.astype(BF16)
    kv = None
    for layer in range(DEPTH):
        if layer < N_A_LAYERS:
            proj = _matmul(xb, a_w_in, layer, F32)
            o = _hgrn(proj, lower_bounds[layer], a_norm_g[layer], bsz, seq)
            w_out, w_layer = a_w_out, layer
        else:
            j = layer - N_A_LAYERS
            if kv is None:
                kv = _matmul(xb, b_w_kv[None], 0, BF16)
            q = _matmul(xb, b_w_q, j, BF16)
            o = _swa(q, kv, att_bias, b_sinks[j].astype(F32), bsz, seq)
            w_out, w_layer = b_w_out, j
        xf, xp, route, counts = _out_ln_router(o, w_out, w_layer, xf, ln_g[2 * layer], ln_b[2 * layer],
                                               w_router[layer], b_router[layer].reshape(1, ROUTER_COLS))
        y_planes = _hier_moe(xp, route, counts, moe_w_gate, moe_w_up, moe_w_down, layer)
        outs = _ln_moe(xf, y_planes, route, ln_g[2 * layer + 1], ln_b[2 * layer + 1], with_bf16=layer + 1 < DEPTH)
        xf, xb = outs[0], outs[-1]
    return xf.reshape(bsz, seq, d).astype(x.dtype)
```
